```python
import math
import jax, jax.numpy as jnp
from jax import lax
import numpy as np

D_MODEL = 1024
BATCH = 16
SEQ = 4096
DEPTH = 4

CTX_LEN = 256
GRID_W = 64
HEAD_DIM = 64
EPS = 1e-6
GDN_HEADS = 4
GDN_DK = 64
GDN_DV = 64
GDN_CONV = 3
GDN_CHUNK = 64
ATT_HEADS = 8
ATT_KV_HEADS = 2
ATT_GROUPS = ATT_HEADS // ATT_KV_HEADS
Q_BLOCK = 128
ROPE_THETA = 10000.0
SC_CH = 256
SC_CONV = 3
MIX_WIDTH = GDN_HEADS * GDN_DV + ATT_HEADS * HEAD_DIM + SC_CH
D_FF = 2816
N_EXPERTS = 8
TOP_K = 2
D_FF_EXPERT = 1408
N_DENSE = (DEPTH + 1) // 2
N_MOE = DEPTH // 2
A_QKV = GDN_HEADS * (2 * GDN_DK + GDN_DV)
A_Z = GDN_HEADS * GDN_DV
_SIZES = (A_QKV, A_Z, 2 * GDN_HEADS, 2 * GDN_HEADS,
          ATT_HEADS * HEAD_DIM, ATT_KV_HEADS * HEAD_DIM, ATT_KV_HEADS * HEAD_DIM,
          SC_CH, SC_CH, SC_CH)
IN_COLS = sum(_SIZES)
SPLIT_IDX = tuple(sum(_SIZES[:i + 1]) for i in range(len(_SIZES) - 1))

kernel_name = "hybrid_parallel_gdn_gqa_shortconv_moe_dit"

F32 = jnp.float32


def rmsnorm(x, g):
    xf = x.astype(F32)
    y = xf * lax.rsqrt(jnp.mean(jnp.square(xf), axis=-1, keepdims=True) + EPS)
    return (y * g.astype(F32)).astype(x.dtype)


def l2norm(x):
    return x * lax.rsqrt(jnp.sum(jnp.square(x), axis=-1, keepdims=True) + EPS)


def modulate(h, shift, scale):
    return h * (1 + scale) + shift


def dwconv(x, w):
    k = w.shape[0]
    return lax.conv_general_dilated(x, w[:, None, :].astype(x.dtype), (1,), [(k // 2, k // 2)],
                                    dimension_numbers=('NWC', 'WIO', 'NWC'),
                                    feature_group_count=x.shape[-1])


def grid_positions(n):
    rows = n // GRID_W
    t = jnp.arange(rows * GRID_W, dtype=jnp.int32)
    return t // GRID_W, t % GRID_W


def axial_rope(x, row, col):
    half = x.shape[-1] // 2
    quarter = half // 2
    inv_freq = ROPE_THETA ** (-jnp.arange(quarter, dtype=F32) / quarter)

    def rotate(xa, pos):
        ang = pos.astype(F32)[:, None] * inv_freq[None, :]
        cos = jnp.cos(ang)[None, :, None, :]
        sin = jnp.sin(ang)[None, :, None, :]
        x1, x2 = xa[..., :quarter], xa[..., quarter:]
        return jnp.concatenate([x1 * cos - x2 * sin, x2 * cos + x1 * sin], axis=-1)

    xf = x.astype(F32)
    return jnp.concatenate([rotate(xf[..., :half], row), rotate(xf[..., half:], col)], axis=-1).astype(x.dtype)


def block_attention(q, k, v):
    b, lq, _, dh = q.shape
    nb = lq // Q_BLOCK
    qb = jnp.moveaxis(q.reshape(b, nb, Q_BLOCK, ATT_KV_HEADS, ATT_GROUPS, dh), 1, 0)
    scale = dh ** -0.5

    def one_block(qi):
        s = jnp.einsum('bqkgd,bskd->bkgqs', qi, k).astype(F32) * scale
        p = jax.nn.softmax(s, axis=-1).astype(v.dtype)
        return jnp.einsum('bkgqs,bskd->bqkgd', p, v)

    o = lax.map(one_block, qb)
    return jnp.moveaxis(o, 0, 1).reshape(b, lq, ATT_HEADS * dh)


def gated_delta_chunked(q, k, v, log_a, beta, s0):
    b, h, l, dk = q.shape
    dv = v.shape[-1]
    cs = GDN_CHUNK
    n = l // cs
    q = (q * dk ** -0.5).reshape(b, h, n, cs, dk)
    k = k.reshape(b, h, n, cs, dk)
    v = v.reshape(b, h, n, cs, dv)
    beta = beta.reshape(b, h, n, cs)
    gc = jnp.cumsum(log_a.reshape(b, h, n, cs), axis=-1)
    incl = jnp.tril(jnp.ones((cs, cs), dtype=bool))
    strict = jnp.tril(jnp.ones((cs, cs), dtype=bool), -1)
    decay = jnp.where(incl, jnp.exp(jnp.where(incl, gc[..., :, None] - gc[..., None, :], 0.0)), 0.0)
    kk = jnp.einsum('bhnid,bhnjd->bhnij', k, k)
    lower = jnp.where(strict, beta[..., :, None] * kk * decay, 0.0)
    rhs = jnp.concatenate([v * beta[..., None], k * (beta * jnp.exp(gc))[..., None]], axis=-1)
    sol = lax.linalg.triangular_solve(jnp.eye(cs, dtype=F32) + lower, rhs,
                                      left_side=True, lower=True, unit_diagonal=True)
    u, w = sol[..., :dv], sol[..., dv:]
    qk = jnp.where(incl, jnp.einsum('bhnid,bhnjd->bhnij', q, k) * decay, 0.0)
    q_dec = q * jnp.exp(gc)[..., None]
    k_dec = k * jnp.exp(gc[..., -1:] - gc)[..., None]
    a_last = jnp.exp(gc[..., -1])

    def step(s, xs):
        qk_i, u_i, w_i, qd_i, kd_i, al_i = xs
        v_new = u_i - jnp.einsum('bhik,bhkv->bhiv', w_i, s)
        o = jnp.einsum('bhik,bhkv->bhiv', qd_i, s) + jnp.einsum('bhij,bhjv->bhiv', qk_i, v_new)
        s = s * al_i[..., None, None] + jnp.einsum('bhik,bhiv->bhkv', kd_i, v_new)
        return s, o

    xs = tuple(jnp.moveaxis(t, 2, 0) for t in (qk, u, w, q_dec, k_dec, a_last))
    s_fin, o = lax.scan(step, s0, xs)
    return jnp.moveaxis(o, 0, 2).reshape(b, h, l, dv), s_fin


def gdn_inputs(qkv, a, bt, conv_w, a_log, dt_bias):
    b, l, _ = qkv.shape
    qkv = jax.nn.silu(dwconv(qkv, conv_w)).astype(F32)
    q, k, v = jnp.split(qkv, [GDN_HEADS * GDN_DK, 2 * GDN_HEADS * GDN_DK], axis=-1)

    def heads(t, d):
        return jnp.swapaxes(t.reshape(b, l, GDN_HEADS, d), 1, 2)

    q = l2norm(heads(q, GDN_DK))
    k = l2norm(heads(k, GDN_DK))
    v = heads(v, GDN_DV)
    a = a.astype(F32).reshape(b, l, 2, GDN_HEADS).transpose(2, 0, 3, 1)
    bt = bt.astype(F32).reshape(b, l, 2, GDN_HEADS).transpose(2, 0, 3, 1)
    a_log = a_log.astype(F32)[:, None, :, None]
    dt_bias = dt_bias.astype(F32)[:, None, :, None]
    log_a = -jnp.exp(a_log) * jax.nn.softplus(a + dt_bias)
    beta = jax.nn.sigmoid(bt)
    return q, k, v, log_a, beta


def gdn_bidirectional(ctx_in, lat_in):
    qc, kc, vc, lac, bec = ctx_in
    ql, kl, vl, lal, bel = lat_in
    b = ql.shape[0]
    o_ctx = 0.0
    o_lat = 0.0
    for d in range(2):
        f = (lambda t: jnp.flip(t, axis=2)) if d == 1 else (lambda t: t)
        s0 = jnp.zeros((b, GDN_HEADS, GDN_DK, GDN_DV), F32)
        oc, s_ctx = gated_delta_chunked(f(qc), f(kc), f(vc), f(lac[d]), f(bec[d]), s0)
        ol, _ = gated_delta_chunked(f(ql), f(kl), f(vl), f(lal[d]), f(bel[d]), s_ctx)
        o_ctx = o_ctx + f(oc)
        o_lat = o_lat + f(ol)
    return o_ctx, o_lat


def gdn_output(o, z, g):
    b, h, l, dv = o.shape
    o = rmsnorm(jnp.swapaxes(o, 1, 2), g) * jax.nn.silu(z.astype(F32)).reshape(b, l, h, dv)
    return o.reshape(b, l, h * dv).astype(z.dtype)


def attn_heads(q, k, v, qn, kn):
    b, l, _ = q.shape
    q = rmsnorm(q.reshape(b, l, ATT_HEADS, HEAD_DIM), qn)
    k = rmsnorm(k.reshape(b, l, ATT_KV_HEADS, HEAD_DIM), kn)
    v = v.reshape(b, l, ATT_KV_HEADS, HEAD_DIM)
    return q, k, v


def short_conv_mix(bg, cg, hh, w):
    return bg * dwconv(cg * hh, w)


def token_mixers(hl, hc, w_in, conv_a, a_log, dt_bias, gdn_g, q_n, k_n, conv_c, w_out, row, col, need_ctx_out):
    (l_qkv, l_z, l_a, l_b, l_q, l_k, l_v, l_cb, l_cc, l_ch) = jnp.split(hl @ w_in, SPLIT_IDX, axis=-1)
    (c_qkv, c_z, c_a, c_b, c_q, c_k, c_v, c_cb, c_cc, c_ch) = jnp.split(hc @ w_in, SPLIT_IDX, axis=-1)
    a_ctx = gdn_inputs(c_qkv, c_a, c_b, conv_a, a_log, dt_bias)
    a_lat = gdn_inputs(l_qkv, l_a, l_b, conv_a, a_log, dt_bias)
    oa_c, oa_l = gdn_bidirectional(a_ctx, a_lat)
    ya_l = gdn_output(oa_l, l_z, gdn_g)
    qc, kc, vc = attn_heads(c_q, c_k, c_v, q_n, k_n)
    ql, kl, vl = attn_heads(l_q, l_k, l_v, q_n, k_n)
    ql = axial_rope(ql, row, col)
    kl = axial_rope(kl, row, col)
    k_all = jnp.concatenate([kc, kl], axis=1)
    v_all = jnp.concatenate([vc, vl], axis=1)
    yb_l = block_attention(ql, k_all, v_all)
    yc_l = short_conv_mix(l_cb, l_cc, l_ch, conv_c)
    out_l = jnp.concatenate([ya_l, yb_l, yc_l], axis=-1) @ w_out
    if not need_ctx_out:
        return out_l, None
    ya_c = gdn_output(oa_c, c_z, gdn_g)
    yb_c = block_attention(qc, kc, vc)
    yc_c = short_conv_mix(c_cb, c_cc, c_ch, conv_c)
    out_c = jnp.concatenate([ya_c, yb_c, yc_c], axis=-1) @ w_out
    return out_l, out_c


def swiglu(h, wg, wu, wd):
    return (jax.nn.silu(h @ wg) * (h @ wu)) @ wd


def moe_swiglu(h, router, wg, wu, wd):
    probs = jax.nn.softmax((h @ router).astype(F32), axis=-1)
    top_p, top_i = lax.top_k(probs, TOP_K)
    top_p = top_p / jnp.sum(top_p, axis=-1, keepdims=True)
    comb = jnp.sum(jax.nn.one_hot(top_i, N_EXPERTS, dtype=F32) * top_p[..., None], axis=-2)
    comb = comb.astype(h.dtype)
    out = jnp.zeros_like(h)
    for e in range(N_EXPERTS):
        out = out + comb[..., e:e + 1] * swiglu(h, wg[e], wu[e], wd[e])
    return out


def channel_mixer(h, l, ffn_w_gate, ffn_w_up, ffn_w_down, router, moe_w_gate, moe_w_up, moe_w_down):
    i = l // 2
    if l % 2 == 0:
        return swiglu(h, ffn_w_gate[i], ffn_w_up[i], ffn_w_down[i])
    return moe_swiglu(h, router[i], moe_w_gate[i], moe_w_up[i], moe_w_down[i])


def setup_inputs(seed: int = 0) -> dict:
    key = jax.random.key(seed)
    ks = jax.random.split(key, 26)
    d = D_MODEL

    def nrm(k, shape, s):
        return jax.random.normal(k, shape, F32) * s

    dt = jnp.exp(jax.random.uniform(ks[11], (DEPTH, 2, GDN_HEADS), F32, math.log(1e-3), math.log(1e-1)))
    return {
        "x": nrm(ks[0], (BATCH, SEQ, d), 1.0),
        "c": nrm(ks[1], (BATCH, d), 1.0),
        "ctx": nrm(ks[2], (BATCH, CTX_LEN, d), 1.0),
        "c_ctx": nrm(ks[3], (d,), 1.0),
        "w_mod": nrm(ks[4], (DEPTH, d, 6 * d), 0.5 * d ** -0.5),
        "b_mod": nrm(ks[5], (DEPTH, 6 * d), 0.02),
        "norm1": 1.0 + nrm(ks[6], (DEPTH, d), 0.02),
        "norm2": 1.0 + nrm(ks[7], (DEPTH, d), 0.02),
        "w_in": nrm(ks[8], (DEPTH, d, IN_COLS), d ** -0.5),
        "conv_a": nrm(ks[9], (DEPTH, GDN_CONV, A_QKV), GDN_CONV ** -0.5),
        "a_log": jnp.log(jax.random.uniform(ks[10], (DEPTH, 2, GDN_HEADS), F32, 1.0, 16.0)),
        "dt_bias": dt + jnp.log(-jnp.expm1(-dt)),
        "gdn_norm": 1.0 + nrm(ks[12], (DEPTH, GDN_DV), 0.02),
        "q_norm": 1.0 + nrm(ks[13], (DEPTH, HEAD_DIM), 0.02),
        "k_norm": 1.0 + nrm(ks[14], (DEPTH, HEAD_DIM), 0.02),
        "conv_c": nrm(ks[15], (DEPTH, SC_CONV, SC_CH), SC_CONV ** -0.5),
        "w_out": nrm(ks[16], (DEPTH, MIX_WIDTH, d), MIX_WIDTH ** -0.5),
        "ffn_w_gate": nrm(ks[17], (N_DENSE, d, D_FF), d ** -0.5),
        "ffn_w_up": nrm(ks[18], (N_DENSE, d, D_FF), d ** -0.5),
        "ffn_w_down": nrm(ks[19], (N_DENSE, D_FF, d), D_FF ** -0.5),
        "router": nrm(ks[20], (N_MOE, d, N_EXPERTS), d ** -0.5),
        "moe_w_gate": nrm(ks[21], (N_MOE, N_EXPERTS, d, D_FF_EXPERT), d ** -0.5),
        "moe_w_up": nrm(ks[22], (N_MOE, N_EXPERTS, d, D_FF_EXPERT), d ** -0.5),
        "moe_w_down": nrm(ks[23], (N_MOE, N_EXPERTS, D_FF_EXPERT, d), D_FF_EXPERT ** -0.5),
        "norm_f": 1.0 + nrm(ks[24], (d,), 0.02),
    }


def reference(x, c, ctx, c_ctx, w_mod, b_mod, norm1, norm2, w_in, conv_a, a_log, dt_bias, gdn_norm,
              q_norm, k_norm, conv_c, w_out, ffn_w_gate, ffn_w_up, ffn_w_down, router,
              moe_w_gate, moe_w_up, moe_w_down, norm_f):
    row, col = grid_positions(x.shape[1])
    s_lat = jax.nn.silu(c)
    s_ctx = jax.nn.silu(c_ctx)
    hc_stream = ctx
    for l in range(DEPTH):
        last = l == DEPTH - 1
        m_lat = (s_lat @ w_mod[l] + b_mod[l])[:, None, :]
        m_ctx = s_ctx @ w_mod[l] + b_mod[l]
        sh1, sc1, g1, sh2, sc2, g2 = jnp.split(m_lat, 6, axis=-1)
        csh1, csc1, cg1, csh2, csc2, cg2 = jnp.split(m_ctx, 6, axis=-1)
        hl = modulate(rmsnorm(x, norm1[l]), sh1, sc1)
        hc = modulate(rmsnorm(hc_stream, norm1[l]), csh1, csc1)
        out_l, out_c = token_mixers(hl, hc, w_in[l], conv_a[l], a_log[l], dt_bias[l], gdn_norm[l],
                                    q_norm[l], k_norm[l], conv_c[l], w_out[l], row, col, not last)
        x = x + g1 * out_l
        hl2 = modulate(rmsnorm(x, norm2[l]), sh2, sc2)
        x = x + g2 * channel_mixer(hl2, l, ffn_w_gate, ffn_w_up, ffn_w_down, router,
                                   moe_w_gate, moe_w_up, moe_w_down)
        if not last:
            hc_stream = hc_stream + cg1 * out_c
            hc2 = modulate(rmsnorm(hc_stream, norm2[l]), csh2, csc2)
            hc_stream = hc_stream + cg2 * channel_mixer(hc2, l, ffn_w_gate, ffn_w_up, ffn_w_down, router,
                                                        moe_w_gate, moe_w_up, moe_w_down)
    return rmsnorm(x, norm_f)
```

```python
import functools
import math

import jax
import jax.numpy as jnp
import numpy as np
from jax import lax
from jax.experimental import pallas as pl
from jax.experimental.pallas import tpu as pltpu

F32 = jnp.float32
BF16 = jnp.bfloat16
EPS = 1e-6

GRID_W = 64
HEAD_DIM = 64
GDN_HEADS = 4
GDN_DK = 64
GDN_DV = 64
GDN_CHUNK = 64
ATT_HEADS = 8
ATT_KV_HEADS = 2
ATT_GROUPS = ATT_HEADS // ATT_KV_HEADS
ROPE_THETA = 10000.0
SC_CH = 256
N_EXPERTS = 8
TOP_K = 2
A_QKV = GDN_HEADS * (2 * GDN_DK + GDN_DV)
A_Z = GDN_HEADS * GDN_DV
GW = GDN_HEADS * GDN_DK
ATT_Q = ATT_HEADS * HEAD_DIM
ATT_KV = ATT_KV_HEADS * HEAD_DIM

C_QKV = 0
C_Z = 768
C_Q = 1024
C_K = 1536
C_V = 1664
C_CB = 1792
C_CC = 2048
C_CH = 2304
C_AB = 2560
P_COLS = 2688

VMEM_LIMIT = 56 * 1024 * 1024
LANE = 128

TM_MM = 512
TM_CV = 256
TQ = 256
TK = 256


def _cparams(sem):
    return pltpu.CompilerParams(dimension_semantics=sem, vmem_limit_bytes=VMEM_LIMIT)


def _sigmoid(x):
    return 1.0 / (1.0 + jnp.exp(-x))


def _silu(x):
    return x * _sigmoid(x)


def _dot(a, b):
    return jnp.dot(a, b, preferred_element_type=F32)


def _group_sum(x2, gmat):
    hi = x2.astype(BF16)
    lo = (x2 - hi.astype(F32)).astype(BF16)
    return _dot(hi, gmat) + _dot(lo, gmat)


def _group_ones(width):
    r = lax.broadcasted_iota(jnp.int32, (width, width), 0) // 64
    c = lax.broadcasted_iota(jnp.int32, (width, width), 1) // 64
    return jnp.where(r == c, 1.0, 0.0).astype(BF16)


def _mod_kernel(s_ref, w_ref, b_ref, o_ref):
    s = _silu(s_ref[...])
    o_ref[...] = jnp.dot(s, w_ref[...], precision=lax.Precision.HIGHEST,
                         preferred_element_type=F32) + b_ref[...]


def _modulation(c, c_ctx, w_mod, b_mod):
    depth, d, d6 = w_mod.shape
    rows = c.shape[0] + 1
    rpad = -(-rows // 8) * 8
    s = jnp.concatenate([c_ctx[None, :], c, jnp.zeros((rpad - rows, d), F32)], axis=0)
    tn = 1536
    out = pl.pallas_call(
        _mod_kernel,
        grid=(depth, d6 // tn),
        in_specs=[
            pl.BlockSpec((rpad, d), lambda l, j: (0, 0)),
            pl.BlockSpec((None, d, tn), lambda l, j: (l, 0, j)),
            pl.BlockSpec((None, 1, tn), lambda l, j: (l, 0, j)),
        ],
        out_specs=pl.BlockSpec((None, rpad, tn), lambda l, j: (l, 0, j)),
        out_shape=jax.ShapeDtypeStruct((depth, rpad, d6), F32),
        compiler_params=_cparams(("parallel", "parallel")),
        name="modulation",
    )(s, w_mod, b_mod.reshape(depth, 1, d6))
    m = out[:, :rows].reshape(depth, rows, 6, d)
    return jnp.pad(m, ((0, 0), (0, 0), (0, 2), (0, 0)))


def _norm_mod(x, g, shift, scale):
    ms = jnp.mean(x * x, axis=-1, keepdims=True)
    y = x * lax.rsqrt(ms + EPS) * g
    return y * (1.0 + scale) + shift


def _inproj_kernel(x_ref, g_ref, mod_ref, w_ref, o_ref):
    h = _norm_mod(x_ref[...], g_ref[...], mod_ref[0:1, :], mod_ref[1:2, :]).astype(BF16)
    ncol = o_ref.shape[1]
    for c0 in range(0, ncol, 256):
        c1 = min(c0 + 256, ncol)
        o_ref[:, c0:c1] = _dot(h, w_ref[:, c0:c1])


def _mod_row(j, n_lat_tiles, tiles_per_batch):
    return jnp.where(j < n_lat_tiles, 1 + j // tiles_per_batch, 0)


def _inproj(xs, g, mod, w, geom):
    n, d = xs.shape
    tm = geom["tm_mm"]
    nlt, tpb = geom["nl"] // tm, geom["l"] // tm
    return pl.pallas_call(
        _inproj_kernel,
        grid=(n // tm,),
        in_specs=[
            pl.BlockSpec((tm, d), lambda j: (j, 0)),
            pl.BlockSpec((1, d), lambda j: (0, 0)),
            pl.BlockSpec((None, 8, d), lambda j: (_mod_row(j, nlt, tpb), 0, 0)),
            pl.BlockSpec((d, P_COLS), lambda j: (0, 0)),
        ],
        out_specs=pl.BlockSpec((tm, P_COLS), lambda j: (j, 0)),
        out_shape=jax.ShapeDtypeStruct((n, P_COLS), F32),
        compiler_params=_cparams(("parallel",)),
        name="inproj",
    )(xs, g, mod, w)


def _seq_flags(j, tm, geom):
    row0 = j * tm
    is_lat = row0 < geom["nl"]
    pos = jnp.where(is_lat, row0 % geom["l"], (row0 - geom["nl"]) % geom["lc"])
    slen = jnp.where(is_lat, geom["l"], geom["lc"])
    return pos == 0, pos + tm == slen


def _conv3(u, prev_row, next_row, w_ref):
    tm = u.shape[0]
    rid = lax.broadcasted_iota(jnp.int32, u.shape, 0)
    up = jnp.where(rid == 0, prev_row, pltpu.roll(u, 1, 0))
    un = jnp.where(rid == tm - 1, next_row, pltpu.roll(u, tm - 1, 0))
    return w_ref[0:1, :] * up + w_ref[1:2, :] * u + w_ref[2:3, :] * un


def _softplus(x):
    return jnp.maximum(x, 0.0) + jnp.log(1.0 + jnp.exp(-jnp.abs(x)))


def _prep_kernel(qkv_ref, qkvp_ref, qkvn_ref, ab_ref, q_ref, k_ref, v_ref, cos_ref, sin_ref,
                 cw_ref, gp_ref, qg_ref, kg_ref,
                 gqkv_ref, gate_ref, qt_ref, kh_ref, vt_ref, *, geom):
    j = pl.program_id(0)
    tm = qkv_ref.shape[0]
    first, last = _seq_flags(j, tm, geom)
    u = qkv_ref[...]
    hp = jnp.where(first, 0.0, qkvp_ref[7:8, :])
    hn = jnp.where(last, 0.0, qkvn_ref[0:1, :])
    s = _silu(_conv3(u, hp, hn, cw_ref))
    g256 = _group_ones(GW)
    qg = s[:, 0:GW]
    kg = s[:, GW:2 * GW]
    qg = qg * lax.rsqrt(_group_sum(qg * qg, g256) + EPS) * (GDN_DK ** -0.5)
    kg = kg * lax.rsqrt(_group_sum(kg * kg, g256) + EPS)
    gqkv_ref[:, 0:GW] = qg
    gqkv_ref[:, GW:2 * GW] = kg
    gqkv_ref[:, 2 * GW:3 * GW] = s[:, 2 * GW:3 * GW]
    ab = ab_ref[...]
    lane = lax.broadcasted_iota(jnp.int32, ab.shape, 1)
    log_a = -jnp.exp(gp_ref[0:1, :]) * _softplus(ab + gp_ref[1:2, :])
    gate_ref[...] = jnp.where(lane < 2 * GDN_HEADS, log_a, _sigmoid(ab))
    cos = cos_ref[...]
    sin = sin_ref[...]
    lane128 = lax.broadcasted_iota(jnp.int32, cos.shape, 1)
    first_half = (lane128 % 32) < 16

    def rope(xb):
        r_lo = pltpu.roll(xb, 16, 1)
        r_hi = pltpu.roll(xb, 112, 1)
        return xb * cos + jnp.where(first_half, r_hi, r_lo) * sin

    q = q_ref[...]
    q = q * lax.rsqrt(_group_sum(q * q, _group_ones(ATT_Q)) * (1.0 / HEAD_DIM) + EPS) * qg_ref[0:1, :]
    qr = jnp.concatenate([rope(q[:, c:c + LANE]) for c in range(0, ATT_Q, LANE)], axis=1)
    qt_ref[...] = (qr * (HEAD_DIM ** -0.5)).T.astype(BF16)
    k = k_ref[...]
    k = k * lax.rsqrt(_group_sum(k * k, _group_ones(ATT_KV)) * (1.0 / HEAD_DIM) + EPS) * kg_ref[0:1, :]
    kr = rope(k).astype(BF16)
    for h in range(ATT_KV_HEADS):
        kh_ref[h] = kr[:, h * HEAD_DIM:(h + 1) * HEAD_DIM]
    vt_ref[...] = v_ref[...].T.astype(BF16)


def _prep(p, cos_t, sin_t, conv_a, gparams, qg, kg, geom):
    n = p.shape[0]
    tm = geom["tm_cv"]
    nlt, tpb = geom["nl"] // tm, geom["l"] // tm
    hb = tm // 8
    nblk8 = n // 8

    def rope_idx(j):
        return jnp.where(j < nlt, 1 + j % tpb, 0)

    col = lambda w, c: pl.BlockSpec((tm, w), lambda j: (j, c // w))
    return pl.pallas_call(
        functools.partial(_prep_kernel, geom=geom),
        grid=(n // tm,),
        in_specs=[
            col(A_QKV, C_QKV),
            pl.BlockSpec((8, A_QKV), lambda j: (jnp.maximum(j * hb - 1, 0), 0)),
            pl.BlockSpec((8, A_QKV), lambda j: (jnp.minimum((j + 1) * hb, nblk8 - 1), 0)),
            col(LANE, C_AB),
            col(ATT_Q, C_Q),
            col(ATT_KV, C_K),
            col(ATT_KV, C_V),
            pl.BlockSpec((tm, LANE), lambda j: (rope_idx(j), 0)),
            pl.BlockSpec((tm, LANE), lambda j: (rope_idx(j), 0)),
            pl.BlockSpec((8, A_QKV), lambda j: (0, 0)),
            pl.BlockSpec((8, LANE), lambda j: (0, 0)),
            pl.BlockSpec((8, ATT_Q), lambda j: (0, 0)),
            pl.BlockSpec((8, ATT_KV), lambda j: (0, 0)),
        ],
        out_specs=[
            pl.BlockSpec((tm, A_QKV), lambda j: (j, 0)),
            pl.BlockSpec((tm, LANE), lambda j: (j, 0)),
            pl.BlockSpec((ATT_Q, tm), lambda j: (0, j)),
            pl.BlockSpec((ATT_KV_HEADS, tm, HEAD_DIM), lambda j: (0, j, 0)),
            pl.BlockSpec((ATT_KV, tm), lambda j: (0, j)),
        ],
        out_shape=[
            jax.ShapeDtypeStruct((n, A_QKV), F32),
            jax.ShapeDtypeStruct((n, LANE), F32),
            jax.ShapeDtypeStruct((ATT_Q, n), BF16),
            jax.ShapeDtypeStruct((ATT_KV_HEADS, n, HEAD_DIM), BF16),
            jax.ShapeDtypeStruct((ATT_KV, n), BF16),
        ],
        compiler_params=_cparams(("parallel",)),
        name="mixer_prep",
    )(p, p, p, p, p, p, p, cos_t, sin_t, conv_a, gparams, qg, kg)


def _head_of_lane(shape):
    return lax.broadcasted_iota(jnp.int32, shape, 1) // 64


def _block_diag(xb):
    hl = _head_of_lane(xb.shape)
    zero = jnp.zeros_like(xb)
    return jnp.concatenate([jnp.where(hl == h, xb, zero) for h in range(GDN_HEADS)], axis=0)


def _dot_exact3(a_bf, x):
    h1 = x.astype(BF16)
    r1 = x - h1.astype(F32)
    h2 = r1.astype(BF16)
    h3 = (r1 - h2.astype(F32)).astype(BF16)
    return _dot(a_bf, h1) + _dot(a_bf, h2) + _dot(a_bf, h3)


def _gdn_direction(q, k, v, gates, kk_w, qk_w, s_ref, rev):
    cs = GDN_CHUNK
    d = 1 if rev else 0
    hl = _head_of_lane((cs, GW))
    row = lax.broadcasted_iota(jnp.int32, (cs, GW), 0)
    col = lax.broadcasted_iota(jnp.int32, (cs, GW), 1) % 64

    def widen(base):
        out = jnp.zeros((cs, GW), F32)
        for h in range(GDN_HEADS):
            out = jnp.where(hl == h, gates[:, base + h:base + h + 1], out)
        return out

    la_w = widen(GDN_HEADS * d)
    beta_w = widen(2 * GDN_HEADS + GDN_HEADS * d)
    ti = lax.broadcasted_iota(jnp.int32, (cs, cs), 0)
    tj = lax.broadcasted_iota(jnp.int32, (cs, cs), 1)
    tri = jnp.where((tj >= ti) if rev else (tj <= ti), 1.0, 0.0).astype(BF16)
    gcc = _dot_exact3(tri, la_w)
    gcr = jnp.sum(jnp.where((row >= col) if rev else (row <= col), la_w, 0.0), axis=0, keepdims=True)
    incl = (row <= col) if rev else (row >= col)
    strict = (row < col) if rev else (row > col)
    decay = jnp.where(incl, jnp.exp(jnp.where(incl, gcc - gcr, 0.0)), 0.0)
    lower = jnp.where(strict, beta_w * kk_w * decay, 0.0)
    same16 = (row // 16) == (col // 16)
    same32 = (row // 32) == (col // 32)
    pw = jnp.where(same16, lower, 0.0)
    t = jnp.where(row == col, 1.0, 0.0) - pw
    pw_bd = _block_diag(pw.astype(BF16))
    for _ in range(3):
        pw = _dot(pw.astype(BF16), pw_bd)
        pw_bd = _block_diag(pw.astype(BF16))
        t = t + _dot(t.astype(BF16), pw_bd)
    for off_diag in (jnp.where(same32 & jnp.logical_not(same16), lower, 0.0),
                     jnp.where(same32, 0.0, lower)):
        tb = t.astype(BF16)
        tc = _dot(tb, _block_diag(off_diag.astype(BF16)))
        t = t - _dot(tc.astype(BF16), _block_diag(tb))
    tb = t.astype(BF16)
    egc = jnp.exp(gcc)
    u = _dot(tb, _block_diag((v * beta_w).astype(BF16)))
    w = _dot(tb, _block_diag((k * (beta_w * egc)).astype(BF16)))
    qkm = jnp.where(incl, qk_w * decay, 0.0)
    q_dec = q * egc
    gl = gcc[0:1, :] if rev else gcc[cs - 1:cs, :]
    k_dec = k * jnp.exp(gl - gcc)
    a_last = jnp.exp(gl)
    s = s_ref[...]
    sb = s.astype(BF16)
    v_new = u - _dot(w.astype(BF16), sb)
    vb = v_new.astype(BF16)
    o = _dot(q_dec.astype(BF16), sb) + _dot(qkm.astype(BF16), _block_diag(vb))
    upd = lax.dot_general(k_dec.astype(BF16), vb, (((0,), (0,)), ((), ())), preferred_element_type=F32)
    r2 = lax.broadcasted_iota(jnp.int32, (GW, GW), 0) // 64
    c2 = lax.broadcasted_iota(jnp.int32, (GW, GW), 1) // 64
    s_ref[...] = s * a_last + jnp.where(r2 == c2, upd, 0.0)
    return o


def _gdn_kernel(xf_ref, gf_ref, xb_ref, gb_ref, of_ref, ob_ref, sf_ref, sb_ref):
    @pl.when(pl.program_id(1) == 0)
    def _():
        sf_ref[...] = jnp.zeros_like(sf_ref)
        sb_ref[...] = jnp.zeros_like(sb_ref)

    for x_ref, g_ref, o_ref, s_ref, rev in ((xf_ref, gf_ref, of_ref, sf_ref, False),
                                            (xb_ref, gb_ref, ob_ref, sb_ref, True)):
        q = x_ref[:, 0:GW]
        k = x_ref[:, GW:2 * GW]
        v = x_ref[:, 2 * GW:3 * GW]
        kb = k.astype(BF16)
        kbd = _block_diag(kb)
        nt = (((1,), (1,)), ((), ()))
        kk_w = lax.dot_general(kb, kbd, nt, preferred_element_type=F32)
        qk_w = lax.dot_general(q.astype(BF16), kbd, nt, preferred_element_type=F32)
        o_ref[...] = _gdn_direction(q, k, v, g_ref[...], kk_w, qk_w, s_ref, rev)


def _gdn(gqkv, gates, geom):
    n = gqkv.shape[0]
    cs = GDN_CHUNK
    nl_c, nc_c = geom["l"] // cs, geom["lc"] // cs
    lat_blocks = geom["nl"] // cs
    steps = nl_c + nc_c

    def fwd(b, s):
        return jnp.where(s < nc_c, lat_blocks + b * nc_c + s, b * nl_c + s - nc_c)

    def bwd(b, s):
        return jnp.where(s < nc_c, lat_blocks + b * nc_c + (nc_c - 1 - s), b * nl_c + (nl_c - 1 - (s - nc_c)))

    return pl.pallas_call(
        _gdn_kernel,
        grid=(geom["b"], steps),
        in_specs=[
            pl.BlockSpec((cs, A_QKV), lambda b, s: (fwd(b, s), 0)),
            pl.BlockSpec((cs, LANE), lambda b, s: (fwd(b, s), 0)),
            pl.BlockSpec((cs, A_QKV), lambda b, s: (bwd(b, s), 0)),
            pl.BlockSpec((cs, LANE), lambda b, s: (bwd(b, s), 0)),
        ],
        out_specs=[
            pl.BlockSpec((cs, GW), lambda b, s: (fwd(b, s), 0)),
            pl.BlockSpec((cs, GW), lambda b, s: (bwd(b, s), 0)),
        ],
        out_shape=[jax.ShapeDtypeStruct((n, GW), F32), jax.ShapeDtypeStruct((n, GW), F32)],
        scratch_shapes=[pltpu.VMEM((GW, GW), F32), pltpu.VMEM((GW, GW), F32)],
        compiler_params=_cparams(("parallel", "arbitrary")),
        name="gdn_scan",
    )(gqkv, gates, gqkv, gates)


def _attn_kernel(qt_ref, kc_ref, kl_ref, vc_ref, vl_ref, o_ref, m_ref, l_ref, acc_ref, *, n_lat_q):
    m_ref[...] = jnp.full(m_ref.shape, -jnp.inf, F32)
    l_ref[...] = jnp.zeros(l_ref.shape, F32)
    acc_ref[...] = jnp.zeros(acc_ref.shape, F32)

    def chunk(k_ref, v_ref, off):
        kc = k_ref[pl.ds(off, TK), :]
        vc = v_ref[:, pl.ds(off, TK)]
        for g in range(ATT_GROUPS):
            r = slice(g * HEAD_DIM, (g + 1) * HEAD_DIM)
            st = _dot(kc, qt_ref[r, :])
            m_old = m_ref[g:g + 1, :]
            m_new = jnp.maximum(m_old, jnp.max(st, axis=0, keepdims=True))
            p = jnp.exp(st - m_new)
            alpha = jnp.exp(m_old - m_new)
            l_ref[g:g + 1, :] = alpha * l_ref[g:g + 1, :] + jnp.sum(p, axis=0, keepdims=True)
            acc_ref[r, :] = alpha * acc_ref[r, :] + _dot(vc, p.astype(BF16))
            m_ref[g:g + 1, :] = m_new

    def segment(k_ref, v_ref):
        def body(i, carry):
            chunk(k_ref, v_ref, pl.multiple_of(i * TK, TK))
            return carry
        lax.fori_loop(0, k_ref.shape[0] // TK, body, 0)

    segment(kc_ref, vc_ref)

    @pl.when(pl.program_id(2) < n_lat_q)
    def _():
        segment(kl_ref, vl_ref)

    for g in range(ATT_GROUPS):
        r = slice(g * HEAD_DIM, (g + 1) * HEAD_DIM)
        acc_ref[r, :] = acc_ref[r, :] / l_ref[g:g + 1, :]
    o_ref[...] = acc_ref[...].T.astype(o_ref.dtype)


def _attention(qt, kh, vt, with_ctx_queries, geom):
    b, l, lc, nl = geom["b"], geom["l"], geom["lc"], geom["nl"]
    tq = geom["tq"]
    nq_l, nq_c = l // tq, lc // tq
    nq = nq_l + (nq_c if with_ctx_queries else 0)
    n_rows = nl + (geom["nc"] if with_ctx_queries else 0)
    gw = ATT_GROUPS * HEAD_DIM

    def qblock(bb, i):
        return jnp.where(i < nq_l, bb * nq_l + i, nl // tq + bb * nq_c + (i - nq_l))

    return pl.pallas_call(
        functools.partial(_attn_kernel, n_lat_q=nq_l),
        grid=(b, ATT_KV_HEADS, nq),
        in_specs=[
            pl.BlockSpec((gw, tq), lambda bb, h, i: (h, qblock(bb, i))),
            pl.BlockSpec((None, lc, HEAD_DIM), lambda bb, h, i: (h, nl // lc + bb, 0)),
            pl.BlockSpec((None, l, HEAD_DIM), lambda bb, h, i: (h, bb, 0)),
            pl.BlockSpec((HEAD_DIM, lc), lambda bb, h, i: (h, nl // lc + bb)),
            pl.BlockSpec((HEAD_DIM, l), lambda bb, h, i: (h, bb)),
        ],
        out_specs=pl.BlockSpec((tq, gw), lambda bb, h, i: (qblock(bb, i), h)),
        out_shape=jax.ShapeDtypeStruct((n_rows, ATT_Q), BF16),
        scratch_shapes=[pltpu.VMEM((8, tq), F32), pltpu.VMEM((8, tq), F32), pltpu.VMEM((gw, tq), F32)],
        compiler_params=_cparams(("parallel", "parallel", "arbitrary")),
        name="attention",
    )(qt, kh, kh, vt, vt)


def _outproj_kernel(x_ref, of_ref, ob_ref, z_ref, yb_ref, cb_ref, cc_ref, ch_ref,
                    ccp_ref, chp_ref, ccn_ref, chn_ref, mod_ref, w_ref, gg_ref, cw_ref,
                    o_ref, *, geom):
    j = pl.program_id(0)
    tm = x_ref.shape[0]
    first, last = _seq_flags(j, tm, geom)
    o = of_ref[...] + ob_ref[...]
    ms = _group_sum(o * o, _group_ones(GW)) * (1.0 / GDN_DV)
    ya = o * lax.rsqrt(ms + EPS) * gg_ref[0:1, :] * _silu(z_ref[...])
    u = cc_ref[...] * ch_ref[...]
    hp = jnp.where(first, 0.0, ccp_ref[7:8, :] * chp_ref[7:8, :])
    hn = jnp.where(last, 0.0, ccn_ref[0:1, :] * chn_ref[0:1, :])
    yc = cb_ref[...] * _conv3(u, hp, hn, cw_ref)
    acc = _dot(ya.astype(BF16), w_ref[0:A_Z, :])
    acc += _dot(yb_ref[...], w_ref[A_Z:A_Z + ATT_Q, :])
    acc += _dot(yc.astype(BF16), w_ref[A_Z + ATT_Q:, :])
    o_ref[...] = x_ref[...] + mod_ref[2:3, :] * acc


def _outproj(xs, of, ob, p, yb, mod, w, gg, conv_c, n_rows, geom):
    d = xs.shape[1]
    tm = geom["tm_cv"]
    nlt, tpb = geom["nl"] // tm, geom["l"] // tm
    hb = tm // 8
    nblk8 = xs.shape[0] // 8
    col = lambda w_, c: pl.BlockSpec((tm, w_), lambda j: (j, c // w_))
    prev = lambda c: pl.BlockSpec((8, SC_CH), lambda j: (jnp.maximum(j * hb - 1, 0), c // SC_CH))
    nxt = lambda c: pl.BlockSpec((8, SC_CH), lambda j: (jnp.minimum((j + 1) * hb, nblk8 - 1), c // SC_CH))
    return pl.pallas_call(
        functools.partial(_outproj_kernel, geom=geom),
        grid=(n_rows // tm,),
        in_specs=[
            pl.BlockSpec((tm, d), lambda j: (j, 0)),
            pl.BlockSpec((tm, GW), lambda j: (j, 0)),
            pl.BlockSpec((tm, GW), lambda j: (j, 0)),
            col(A_Z, C_Z),
            pl.BlockSpec((tm, ATT_Q), lambda j: (j, 0)),
            col(SC_CH, C_CB), col(SC_CH, C_CC), col(SC_CH, C_CH),
            prev(C_CC), prev(C_CH), nxt(C_CC), nxt(C_CH),
            pl.BlockSpec((None, 8, d), lambda j: (_mod_row(j, nlt, tpb), 0, 0)),
            pl.BlockSpec((d, d), lambda j: (0, 0)),
            pl.BlockSpec((8, GW), lambda j: (0, 0)),
            pl.BlockSpec((8, SC_CH), lambda j: (0, 0)),
        ],
        out_specs=pl.BlockSpec((tm, d), lambda j: (j, 0)),
        out_shape=jax.ShapeDtypeStruct((n_rows, d), F32),
        compiler_params=_cparams(("parallel",)),
        name="outproj",
    )(xs, of, ob, p, yb, p, p, p, p, p, p, p, mod, w, gg, conv_c)


def _ffn_kernel(x_ref, g_ref, mod_ref, wg_ref, wu_ref, wd_ref, o_ref, h_ref, acc_ref, *, tf):
    x = x_ref[...]
    h_ref[...] = _norm_mod(x, g_ref[...], mod_ref[3:4, :], mod_ref[4:5, :]).astype(BF16)
    acc_ref[...] = jnp.zeros_like(acc_ref)

    def body(c, carry):
        off = pl.multiple_of(c * tf, tf)
        hb = h_ref[...]
        gate = _dot(hb, wg_ref[:, pl.ds(off, tf)])
        up = _dot(hb, wu_ref[:, pl.ds(off, tf)])
        a = (_silu(gate) * up).astype(BF16)
        acc_ref[...] += _dot(a, wd_ref[pl.ds(off, tf), :])
        return carry

    lax.fori_loop(0, wg_ref.shape[1] // tf, body, 0)
    o_ref[...] = x + mod_ref[5:6, :] * acc_ref[...]


def _ffn(xs, g, mod, wg, wu, wd, n_rows, geom):
    d = xs.shape[1]
    ff = wg.shape[1]
    tm = geom["tm_mm"]
    nlt, tpb = geom["nl"] // tm, geom["l"] // tm
    return pl.pallas_call(
        functools.partial(_ffn_kernel, tf=256),
        grid=(n_rows // tm,),
        in_specs=[
            pl.BlockSpec((tm, d), lambda j: (j, 0)),
            pl.BlockSpec((1, d), lambda j: (0, 0)),
            pl.BlockSpec((None, 8, d), lambda j: (_mod_row(j, nlt, tpb), 0, 0)),
            pl.BlockSpec((d, ff), lambda j: (0, 0)),
            pl.BlockSpec((d, ff), lambda j: (0, 0)),
            pl.BlockSpec((ff, d), lambda j: (0, 0)),
        ],
        out_specs=pl.BlockSpec((tm, d), lambda j: (j, 0)),
        out_shape=jax.ShapeDtypeStruct((n_rows, d), F32),
        scratch_shapes=[pltpu.VMEM((tm, d), BF16), pltpu.VMEM((tm, d), F32)],
        compiler_params=_cparams(("parallel",)),
        name="ffn",
    )(xs, g, mod, wg, wu, wd)


def _moe_kernel(x_ref, g_ref, mod_ref, r_ref, wgu_ref, wd_ref, o_ref, h_ref, comb_ref, acc_ref):
    e = pl.program_id(1)
    ffe = wd_ref.shape[0]

    @pl.when(e == 0)
    def _():
        h = _norm_mod(x_ref[...], g_ref[...], mod_ref[3:4, :], mod_ref[4:5, :])
        h_ref[...] = h.astype(BF16)
        acc_ref[...] = jnp.zeros_like(acc_ref)
        logits = jnp.dot(h, r_ref[...], precision=lax.Precision.HIGHEST, preferred_element_type=F32)
        lane = lax.broadcasted_iota(jnp.int32, logits.shape, 1).astype(F32)
        valid = lane < N_EXPERTS
        logits = jnp.where(valid, logits, -jnp.inf)
        ex = jnp.exp(logits - jnp.max(logits, axis=-1, keepdims=True))
        probs = jnp.where(valid, ex / jnp.sum(ex, axis=-1, keepdims=True), -1.0)
        p1 = jnp.max(probs, axis=-1, keepdims=True)
        i1 = jnp.min(jnp.where(probs == p1, lane, float(LANE)), axis=-1, keepdims=True)
        rest = jnp.where(lane == i1, -1.0, probs)
        p2 = jnp.max(rest, axis=-1, keepdims=True)
        i2 = jnp.min(jnp.where(rest == p2, lane, float(LANE)), axis=-1, keepdims=True)
        den = p1 + p2
        comb_ref[...] = jnp.where(lane == i1, p1 / den, 0.0) + jnp.where(lane == i2, p2 / den, 0.0)

    hb = h_ref[...]
    gu = _dot(hb, wgu_ref[...])
    a = (_silu(gu[:, :ffe]) * gu[:, ffe:]).astype(BF16)
    y = _dot(a, wd_ref[...])
    comb = comb_ref[...]
    lane = lax.broadcasted_iota(jnp.int32, comb.shape, 1)
    ce = jnp.sum(jnp.where(lane == e, comb, 0.0), axis=-1, keepdims=True)
    acc_ref[...] += ce * y

    @pl.when(e == pl.num_programs(1) - 1)
    def _():
        o_ref[...] = x_ref[...] + mod_ref[5:6, :] * acc_ref[...]


def _moe(xs, g, mod, router, wgu, wd, n_rows, geom):
    d = xs.shape[1]
    ne, ffe = wd.shape[0], wd.shape[1]
    tm = geom["tm_mm"]
    nlt, tpb = geom["nl"] // tm, geom["l"] // tm
    return pl.pallas_call(
        _moe_kernel,
        grid=(n_rows // tm, ne),
        in_specs=[
            pl.BlockSpec((tm, d), lambda j, e: (j, 0)),
            pl.BlockSpec((1, d), lambda j, e: (0, 0)),
            pl.BlockSpec((None, 8, d), lambda j, e: (_mod_row(j, nlt, tpb), 0, 0)),
            pl.BlockSpec((d, LANE), lambda j, e: (0, 0)),
            pl.BlockSpec((None, d, 2 * ffe), lambda j, e: (e, 0, 0)),
            pl.BlockSpec((None, ffe, d), lambda j, e: (e, 0, 0)),
        ],
        out_specs=pl.BlockSpec((tm, d), lambda j, e: (j, 0)),
        out_shape=jax.ShapeDtypeStruct((n_rows, d), F32),
        scratch_shapes=[pltpu.VMEM((tm, d), BF16), pltpu.VMEM((tm, LANE), F32), pltpu.VMEM((tm, d), F32)],
        compiler_params=_cparams(("parallel", "arbitrary")),
        name="moe",
    )(xs, g, mod, router, wgu, wd)


def _final_norm_kernel(x_ref, g_ref, o_ref):
    x = x_ref[...]
    ms = jnp.mean(x * x, axis=-1, keepdims=True)
    o_ref[...] = x * lax.rsqrt(ms + EPS) * g_ref[...]


def _final_norm(xs, g, n_rows, geom):
    d = xs.shape[1]
    tm = geom["tm_mm"]
    return pl.pallas_call(
        _final_norm_kernel,
        grid=(n_rows // tm,),
        in_specs=[pl.BlockSpec((tm, d), lambda j: (j, 0)), pl.BlockSpec((1, d), lambda j: (0, 0))],
        out_specs=pl.BlockSpec((tm, d), lambda j: (j, 0)),
        out_shape=jax.ShapeDtypeStruct((n_rows, d), F32),
        compiler_params=_cparams(("parallel",)),
        name="final_norm",
    )(xs, g)


def _rope_tables(l, tm):
    quarter = HEAD_DIM // 4
    inv_freq = ROPE_THETA ** (-jnp.arange(quarter, dtype=F32) / quarter)
    t = jnp.arange(l, dtype=jnp.int32)
    ang_r = (t // GRID_W).astype(F32)[:, None] * inv_freq[None, :]
    ang_c = (t % GRID_W).astype(F32)[:, None] * inv_freq[None, :]
    cos = jnp.concatenate([jnp.cos(ang_r)] * 2 + [jnp.cos(ang_c)] * 2, axis=1)
    sin = jnp.concatenate([-jnp.sin(ang_r), jnp.sin(ang_r), -jnp.sin(ang_c), jnp.sin(ang_c)], axis=1)
    cos = jnp.tile(cos, (1, LANE // HEAD_DIM))
    sin = jnp.tile(sin, (1, LANE // HEAD_DIM))
    cos = jnp.concatenate([jnp.ones((tm, LANE), F32), cos], axis=0)
    sin = jnp.concatenate([jnp.zeros((tm, LANE), F32), sin], axis=0)
    return cos, sin


def _pad_rows(a, rows=8):
    return jnp.pad(a, ((0, rows - a.shape[0]), (0, 0)))


def _permute_w_in(w):
    o = 0
    parts = {}
    for name, size in (("qkv", A_QKV), ("z", A_Z), ("a", 2 * GDN_HEADS), ("b", 2 * GDN_HEADS),
                       ("q", ATT_Q), ("k", ATT_KV), ("v", ATT_KV), ("cb", SC_CH), ("cc", SC_CH), ("ch", SC_CH)):
        parts[name] = w[:, o:o + size]
        o += size
    pad = jnp.zeros((w.shape[0], P_COLS - C_AB - 4 * GDN_HEADS), w.dtype)
    cols = [parts[k] for k in ("qkv", "z", "q", "k", "v", "cb", "cc", "ch", "a", "b")] + [pad]
    return jnp.concatenate(cols, axis=1).astype(BF16)


def kernel(x, c, ctx, c_ctx, w_mod, b_mod, norm1, norm2, w_in, conv_a, a_log, dt_bias, gdn_norm,
           q_norm, k_norm, conv_c, w_out, ffn_w_gate, ffn_w_up, ffn_w_down, router,
           moe_w_gate, moe_w_up, moe_w_down, norm_f):
    b, l, d = x.shape
    lc = ctx.shape[1]
    depth = w_mod.shape[0]
    nl, nc = b * l, b * lc
    tm_mm = math.gcd(TM_MM, math.gcd(l, nc))
    tm_cv = math.gcd(TM_CV, math.gcd(l, lc))
    assert l % GRID_W == 0 and l % TK == 0 and lc % TK == 0 and nl % lc == 0
    tq = math.gcd(TQ, math.gcd(l, lc))
    geom = dict(b=b, l=l, lc=lc, nl=nl, nc=nc, tm_mm=tm_mm, tm_cv=tm_cv, tq=tq)
    n = nl + nc

    mod = _modulation(c, c_ctx, w_mod, b_mod)
    cos_t, sin_t = _rope_tables(l, tm_cv)
    xs = jnp.concatenate([x.reshape(nl, d), ctx.reshape(nc, d)], axis=0)

    for li in range(depth):
        last = li == depth - 1
        rows_out = nl if last else n
        w_in_p = _permute_w_in(w_in[li])
        gparams = _pad_rows(jnp.stack([jnp.pad(a_log[li].reshape(-1), (0, LANE - 2 * GDN_HEADS)),
                                       jnp.pad(dt_bias[li].reshape(-1), (0, LANE - 2 * GDN_HEADS))]))
        p = _inproj(xs, norm1[li][None, :], mod[li], w_in_p, geom)
        gqkv, gates, qt, kh, vt = _prep(
            p, cos_t, sin_t, _pad_rows(conv_a[li]), gparams,
            _pad_rows(jnp.tile(q_norm[li], ATT_HEADS)[None, :]),
            _pad_rows(jnp.tile(k_norm[li], ATT_KV_HEADS)[None, :]), geom)
        of, ob = _gdn(gqkv, gates, geom)
        yb = _attention(qt, kh, vt, not last, geom)
        xs = _outproj(xs, of, ob, p, yb, mod[li], w_out[li].astype(BF16),
                      _pad_rows(jnp.tile(gdn_norm[li], GDN_HEADS)[None, :]), _pad_rows(conv_c[li]),
                      rows_out, geom)
        i = li // 2
        if li % 2 == 0:
            xs = _ffn(xs, norm2[li][None, :], mod[li], ffn_w_gate[i].astype(BF16),
                      ffn_w_up[i].astype(BF16), ffn_w_down[i].astype(BF16), rows_out, geom)
        else:
            wgu = jnp.concatenate([moe_w_gate[i], moe_w_up[i]], axis=-1).astype(BF16)
            xs = _moe(xs, norm2[li][None, :], mod[li],
                      jnp.pad(router[i], ((0, 0), (0, LANE - N_EXPERTS))), wgu,
                      moe_w_down[i].astype(BF16), rows_out, geom)
    out = _final_norm(xs, norm_f[None, :], nl, geom)
    return out.reshape(b, l, d)
```

```python
import functools
import math

import jax
import jax.numpy as jnp
import numpy as np
from jax import lax
from jax.experimental import pallas as pl
from jax.experimental.pallas import tpu as pltpu

F32 = jnp.float32
BF16 = jnp.bfloat16
EPS = 1e-6

GRID_W = 64
HEAD_DIM = 64
GDN_HEADS = 4
GDN_DK = 64
GDN_DV = 64
GDN_CHUNK = 64
ATT_HEADS = 8
ATT_KV_HEADS = 2
ATT_GROUPS = ATT_HEADS // ATT_KV_HEADS
ROPE_THETA = 10000.0
SC_CH = 256
N_EXPERTS = 8
TOP_K = 2
A_QKV = GDN_HEADS * (2 * GDN_DK + GDN_DV)
A_Z = GDN_HEADS * GDN_DV
GW = GDN_HEADS * GDN_DK
ATT_Q = ATT_HEADS * HEAD_DIM
ATT_KV = ATT_KV_HEADS * HEAD_DIM

C_QKV = 0
C_Z = 768
C_Q = 1024
C_K = 1536
C_V = 1664
C_CB = 1792
C_CC = 2048
C_CH = 2304
C_AB = 2560
P_COLS = 2688

VMEM_LIMIT = 56 * 1024 * 1024
LANE = 128

TM_MM = 512
TM_CV = 256
TQ = 256
TK = 256
GDN_CHUNKS_PER_STEP = 4


def _cparams(sem):
    return pltpu.CompilerParams(dimension_semantics=sem, vmem_limit_bytes=VMEM_LIMIT)


def _sigmoid(x):
    return 1.0 / (1.0 + jnp.exp(-x))


def _silu(x):
    return x * _sigmoid(x)


def _dot(a, b):
    return jnp.dot(a, b, preferred_element_type=F32)


def _group_sum(x2, gmat):
    hi = x2.astype(BF16)
    lo = (x2 - hi.astype(F32)).astype(BF16)
    return _dot(hi, gmat) + _dot(lo, gmat)


def _group_ones(width):
    r = lax.broadcasted_iota(jnp.int32, (width, width), 0) // 64
    c = lax.broadcasted_iota(jnp.int32, (width, width), 1) // 64
    return jnp.where(r == c, 1.0, 0.0).astype(BF16)


def _mod_kernel(s_ref, w_ref, b_ref, o_ref):
    s = _silu(s_ref[...])
    o_ref[...] = jnp.dot(s, w_ref[...], precision=lax.Precision.HIGHEST,
                         preferred_element_type=F32) + b_ref[...]


def _modulation(c, c_ctx, w_mod, b_mod):
    depth, d, d6 = w_mod.shape
    rows = c.shape[0] + 1
    rpad = -(-rows // 8) * 8
    s = jnp.concatenate([c_ctx[None, :], c, jnp.zeros((rpad - rows, d), F32)], axis=0)
    tn = 1536
    out = pl.pallas_call(
        _mod_kernel,
        grid=(depth, d6 // tn),
        in_specs=[
            pl.BlockSpec((rpad, d), lambda l, j: (0, 0)),
            pl.BlockSpec((None, d, tn), lambda l, j: (l, 0, j)),
            pl.BlockSpec((None, 1, tn), lambda l, j: (l, 0, j)),
        ],
        out_specs=pl.BlockSpec((None, rpad, tn), lambda l, j: (l, 0, j)),
        out_shape=jax.ShapeDtypeStruct((depth, rpad, d6), F32),
        compiler_params=_cparams(("parallel", "parallel")),
        name="modulation",
    )(s, w_mod, b_mod.reshape(depth, 1, d6))
    m = out[:, :rows].reshape(depth, rows, 6, d)
    return jnp.pad(m, ((0, 0), (0, 0), (0, 2), (0, 0)))


def _norm_mod(x, g, shift, scale):
    ms = jnp.mean(x * x, axis=-1, keepdims=True)
    y = x * lax.rsqrt(ms + EPS) * g
    return y * (1.0 + scale) + shift


def _inproj_kernel(x_ref, g_ref, mod_ref, w_ref, o_ref):
    h = _norm_mod(x_ref[...], g_ref[...], mod_ref[0:1, :], mod_ref[1:2, :]).astype(BF16)
    ncol = o_ref.shape[1]
    for c0 in range(0, ncol, 256):
        c1 = min(c0 + 256, ncol)
        o_ref[:, c0:c1] = _dot(h, w_ref[:, c0:c1])


def _mod_row(j, n_lat_tiles, tiles_per_batch):
    return jnp.where(j < n_lat_tiles, 1 + j // tiles_per_batch, 0)


def _inproj(xs, g, mod, w, geom):
    n, d = xs.shape
    tm = geom["tm_mm"]
    nlt, tpb = geom["nl"] // tm, geom["l"] // tm
    return pl.pallas_call(
        _inproj_kernel,
        grid=(n // tm,),
        in_specs=[
            pl.BlockSpec((tm, d), lambda j: (j, 0)),
            pl.BlockSpec((1, d), lambda j: (0, 0)),
            pl.BlockSpec((None, 8, d), lambda j: (_mod_row(j, nlt, tpb), 0, 0)),
            pl.BlockSpec((d, P_COLS), lambda j: (0, 0)),
        ],
        out_specs=pl.BlockSpec((tm, P_COLS), lambda j: (j, 0)),
        out_shape=jax.ShapeDtypeStruct((n, P_COLS), F32),
        compiler_params=_cparams(("parallel",)),
        name="inproj",
    )(xs, g, mod, w)


def _seq_flags(j, tm, geom):
    row0 = j * tm
    is_lat = row0 < geom["nl"]
    pos = jnp.where(is_lat, row0 % geom["l"], (row0 - geom["nl"]) % geom["lc"])
    slen = jnp.where(is_lat, geom["l"], geom["lc"])
    return pos == 0, pos + tm == slen


def _conv3(u, prev_row, next_row, w_ref):
    tm = u.shape[0]
    rid = lax.broadcasted_iota(jnp.int32, u.shape, 0)
    up = jnp.where(rid == 0, prev_row, pltpu.roll(u, 1, 0))
    un = jnp.where(rid == tm - 1, next_row, pltpu.roll(u, tm - 1, 0))
    return w_ref[0:1, :] * up + w_ref[1:2, :] * u + w_ref[2:3, :] * un


def _softplus(x):
    return jnp.maximum(x, 0.0) + jnp.log(1.0 + jnp.exp(-jnp.abs(x)))


def _prep_kernel(qkv_ref, qkvp_ref, qkvn_ref, ab_ref, q_ref, k_ref, v_ref, cos_ref, sin_ref,
                 cw_ref, gp_ref, qg_ref, kg_ref,
                 gqkv_ref, gate_ref, qt_ref, kh_ref, vt_ref, *, geom):
    j = pl.program_id(0)
    tm = qkv_ref.shape[0]
    first, last = _seq_flags(j, tm, geom)
    u = qkv_ref[...]
    hp = jnp.where(first, 0.0, qkvp_ref[7:8, :])
    hn = jnp.where(last, 0.0, qkvn_ref[0:1, :])
    s = _silu(_conv3(u, hp, hn, cw_ref))
    g256 = _group_ones(GW)
    qg = s[:, 0:GW]
    kg = s[:, GW:2 * GW]
    qg = qg * lax.rsqrt(_group_sum(qg * qg, g256) + EPS) * (GDN_DK ** -0.5)
    kg = kg * lax.rsqrt(_group_sum(kg * kg, g256) + EPS)
    gqkv_ref[:, 0:GW] = qg
    gqkv_ref[:, GW:2 * GW] = kg
    gqkv_ref[:, 2 * GW:3 * GW] = s[:, 2 * GW:3 * GW]
    ab = ab_ref[...]
    lane = lax.broadcasted_iota(jnp.int32, ab.shape, 1)
    log_a = -jnp.exp(gp_ref[0:1, :]) * _softplus(ab + gp_ref[1:2, :])
    gate_ref[...] = jnp.where(lane < 2 * GDN_HEADS, log_a, _sigmoid(ab))
    cos = cos_ref[...]
    sin = sin_ref[...]
    lane128 = lax.broadcasted_iota(jnp.int32, cos.shape, 1)
    first_half = (lane128 % 32) < 16

    def rope(xb):
        r_lo = pltpu.roll(xb, 16, 1)
        r_hi = pltpu.roll(xb, 112, 1)
        return xb * cos + jnp.where(first_half, r_hi, r_lo) * sin

    q = q_ref[...]
    q = q * lax.rsqrt(_group_sum(q * q, _group_ones(ATT_Q)) * (1.0 / HEAD_DIM) + EPS) * qg_ref[0:1, :]
    qr = jnp.concatenate([rope(q[:, c:c + LANE]) for c in range(0, ATT_Q, LANE)], axis=1)
    qt_ref[...] = (qr * (HEAD_DIM ** -0.5 * math.log2(math.e))).T.astype(BF16)
    k = k_ref[...]
    k = k * lax.rsqrt(_group_sum(k * k, _group_ones(ATT_KV)) * (1.0 / HEAD_DIM) + EPS) * kg_ref[0:1, :]
    kr = rope(k).astype(BF16)
    for h in range(ATT_KV_HEADS):
        kh_ref[h] = kr[:, h * HEAD_DIM:(h + 1) * HEAD_DIM]
    vt_ref[...] = v_ref[...].T.astype(BF16)


def _prep(p, cos_t, sin_t, conv_a, gparams, qg, kg, geom):
    n = p.shape[0]
    tm = geom["tm_cv"]
    nlt, tpb = geom["nl"] // tm, geom["l"] // tm
    hb = tm // 8
    nblk8 = n // 8

    def rope_idx(j):
        return jnp.where(j < nlt, 1 + j % tpb, 0)

    col = lambda w, c: pl.BlockSpec((tm, w), lambda j: (j, c // w))
    return pl.pallas_call(
        functools.partial(_prep_kernel, geom=geom),
        grid=(n // tm,),
        in_specs=[
            col(A_QKV, C_QKV),
            pl.BlockSpec((8, A_QKV), lambda j: (jnp.maximum(j * hb - 1, 0), 0)),
            pl.BlockSpec((8, A_QKV), lambda j: (jnp.minimum((j + 1) * hb, nblk8 - 1), 0)),
            col(LANE, C_AB),
            col(ATT_Q, C_Q),
            col(ATT_KV, C_K),
            col(ATT_KV, C_V),
            pl.BlockSpec((tm, LANE), lambda j: (rope_idx(j), 0)),
            pl.BlockSpec((tm, LANE), lambda j: (rope_idx(j), 0)),
            pl.BlockSpec((8, A_QKV), lambda j: (0, 0)),
            pl.BlockSpec((8, LANE), lambda j: (0, 0)),
            pl.BlockSpec((8, ATT_Q), lambda j: (0, 0)),
            pl.BlockSpec((8, ATT_KV), lambda j: (0, 0)),
        ],
        out_specs=[
            pl.BlockSpec((tm, A_QKV), lambda j: (j, 0)),
            pl.BlockSpec((tm, LANE), lambda j: (j, 0)),
            pl.BlockSpec((ATT_Q, tm), lambda j: (0, j)),
            pl.BlockSpec((ATT_KV_HEADS, tm, HEAD_DIM), lambda j: (0, j, 0)),
            pl.BlockSpec((ATT_KV, tm), lambda j: (0, j)),
        ],
        out_shape=[
            jax.ShapeDtypeStruct((n, A_QKV), F32),
            jax.ShapeDtypeStruct((n, LANE), F32),
            jax.ShapeDtypeStruct((ATT_Q, n), BF16),
            jax.ShapeDtypeStruct((ATT_KV_HEADS, n, HEAD_DIM), BF16),
            jax.ShapeDtypeStruct((ATT_KV, n), BF16),
        ],
        compiler_params=_cparams(("parallel",)),
        name="mixer_prep",
    )(p, p, p, p, p, p, p, cos_t, sin_t, conv_a, gparams, qg, kg)


def _head_of_lane(shape):
    return lax.broadcasted_iota(jnp.int32, shape, 1) // 64


def _block_diag(xb):
    hl = _head_of_lane(xb.shape)
    zero = jnp.zeros_like(xb)
    return jnp.concatenate([jnp.where(hl == h, xb, zero) for h in range(GDN_HEADS)], axis=0)


def _dot_exact3(a_bf, x):
    h1 = x.astype(BF16)
    r1 = x - h1.astype(F32)
    h2 = r1.astype(BF16)
    h3 = (r1 - h2.astype(F32)).astype(BF16)
    return _dot(a_bf, h1) + _dot(a_bf, h2) + _dot(a_bf, h3)


def _gdn_local(insts):
    cs = GDN_CHUNK
    hl = _head_of_lane((cs, GW))
    row = lax.broadcasted_iota(jnp.int32, (cs, GW), 0)
    col = lax.broadcasted_iota(jnp.int32, (cs, GW), 1) % 64
    ti = lax.broadcasted_iota(jnp.int32, (cs, cs), 0)
    tj = lax.broadcasted_iota(jnp.int32, (cs, cs), 1)
    same16 = (row // 16) == (col // 16)
    same32 = (row // 32) == (col // 32)
    eye = jnp.where(row == col, 1.0, 0.0)
    nt = (((1,), (1,)), ((), ()))

    def widen(gates, base):
        out = jnp.zeros((cs, GW), F32)
        for h in range(GDN_HEADS):
            out = jnp.where(hl == h, gates[:, base + h:base + h + 1], out)
        return out

    for it in insts:
        rev = it["rev"]
        d = 1 if rev else 0
        it["la_w"] = widen(it["gates"], GDN_HEADS * d)
        it["beta_w"] = widen(it["gates"], 2 * GDN_HEADS + GDN_HEADS * d)
        it["tri"] = jnp.where((tj >= ti) if rev else (tj <= ti), 1.0, 0.0).astype(BF16)
        it["incl"] = (row <= col) if rev else (row >= col)
        it["strict"] = (row < col) if rev else (row > col)
        it["kb"] = it["k"].astype(BF16)
        it["kbd"] = _block_diag(it["kb"])
    for it in insts:
        it["kk_w"] = lax.dot_general(it["kb"], it["kbd"], nt, preferred_element_type=F32)
    for it in insts:
        it["qk_w"] = lax.dot_general(it["q"].astype(BF16), it["kbd"], nt, preferred_element_type=F32)
    for it in insts:
        it["gcc"] = _dot_exact3(it["tri"], it["la_w"])
    for it in insts:
        rev, la_w, incl = it["rev"], it["la_w"], it["incl"]
        gcr = jnp.sum(jnp.where((row >= col) if rev else (row <= col), la_w, 0.0), axis=0, keepdims=True)
        it["decay"] = jnp.where(incl, jnp.exp(jnp.where(incl, it["gcc"] - gcr, 0.0)), 0.0)
        it["lower"] = jnp.where(it["strict"], it["beta_w"] * it["kk_w"] * it["decay"], 0.0)
        it["pw"] = jnp.where(same16, it["lower"], 0.0)
        it["t"] = eye - it["pw"]
        it["pw_bd"] = _block_diag(it["pw"].astype(BF16))
    for _ in range(3):
        for it in insts:
            it["pw"] = _dot(it["pw"].astype(BF16), it["pw_bd"])
        for it in insts:
            it["pw_bd"] = _block_diag(it["pw"].astype(BF16))
        for it in insts:
            it["t"] = it["t"] + _dot(it["t"].astype(BF16), it["pw_bd"])
    for level in range(2):
        for it in insts:
            off = (jnp.where(same32 & jnp.logical_not(same16), it["lower"], 0.0) if level == 0
                   else jnp.where(same32, 0.0, it["lower"]))
            it["tb"] = it["t"].astype(BF16)
            it["tc"] = _dot(it["tb"], _block_diag(off.astype(BF16)))
        for it in insts:
            it["t"] = it["t"] - _dot(it["tc"].astype(BF16), _block_diag(it["tb"]))
    for it in insts:
        it["tb"] = it["t"].astype(BF16)
        it["egc"] = jnp.exp(it["gcc"])
        it["u"] = _dot(it["tb"], _block_diag((it["v"] * it["beta_w"]).astype(BF16)))
    for it in insts:
        it["w"] = _dot(it["tb"], _block_diag((it["k"] * (it["beta_w"] * it["egc"])).astype(BF16)))
    for it in insts:
        gcc = it["gcc"]
        gl = gcc[0:1, :] if it["rev"] else gcc[cs - 1:cs, :]
        it["qkm"] = jnp.where(it["incl"], it["qk_w"] * it["decay"], 0.0).astype(BF16)
        it["q_dec"] = (it["q"] * it["egc"]).astype(BF16)
        it["k_dec"] = (it["k"] * jnp.exp(gl - gcc)).astype(BF16)
        it["a_last"] = jnp.exp(gl)
        it["wb"] = it["w"].astype(BF16)


def _gdn_recurrence(group):
    r2 = lax.broadcasted_iota(jnp.int32, (GW, GW), 0) // 64
    c2 = lax.broadcasted_iota(jnp.int32, (GW, GW), 1) // 64
    tn = (((0,), (0,)), ((), ()))
    for it in group:
        it["s"] = it["s_ref"][...]
        it["sb"] = it["s"].astype(BF16)
    for it in group:
        it["ws"] = _dot(it["wb"], it["sb"])
    for it in group:
        it["qs"] = _dot(it["q_dec"], it["sb"])
    for it in group:
        it["vb"] = (it["u"] - it["ws"]).astype(BF16)
    for it in group:
        it["o"] = it["qs"] + _dot(it["qkm"], _block_diag(it["vb"]))
    for it in group:
        upd = lax.dot_general(it["k_dec"], it["vb"], tn, preferred_element_type=F32)
        it["s_ref"][...] = it["s"] * it["a_last"] + jnp.where(r2 == c2, upd, 0.0)
    for it in group:
        it["o_ref"][it["rows"], :] = it["o"]


def _gdn_kernel(xf_ref, gf_ref, xb_ref, gb_ref, of_ref, ob_ref, sf_ref, sb_ref):
    @pl.when(pl.program_id(1) == 0)
    def _():
        sf_ref[...] = jnp.zeros_like(sf_ref)
        sb_ref[...] = jnp.zeros_like(sb_ref)

    n_chunks = xf_ref.shape[0] // GDN_CHUNK
    per_dir = []
    for x_ref, g_ref, o_ref, s_ref, rev in ((xf_ref, gf_ref, of_ref, sf_ref, False),
                                            (xb_ref, gb_ref, ob_ref, sb_ref, True)):
        seq = []
        for ci in (reversed(range(n_chunks)) if rev else range(n_chunks)):
            rows = slice(ci * GDN_CHUNK, (ci + 1) * GDN_CHUNK)
            seq.append(dict(rev=rev, rows=rows, o_ref=o_ref, s_ref=s_ref, gates=g_ref[rows, :],
                            q=x_ref[rows, 0:GW], k=x_ref[rows, GW:2 * GW], v=x_ref[rows, 2 * GW:3 * GW]))
        per_dir.append(seq)
    _gdn_local([it for pair in zip(*per_dir) for it in pair])
    for pair in zip(*per_dir):
        _gdn_recurrence(list(pair))


def _gdn(gqkv, gates, geom):
    n = gqkv.shape[0]
    cs = geom["gdn_rows"]
    nl_c, nc_c = geom["l"] // cs, geom["lc"] // cs
    lat_blocks = geom["nl"] // cs
    steps = nl_c + nc_c

    def fwd(b, s):
        return jnp.where(s < nc_c, lat_blocks + b * nc_c + s, b * nl_c + s - nc_c)

    def bwd(b, s):
        return jnp.where(s < nc_c, lat_blocks + b * nc_c + (nc_c - 1 - s), b * nl_c + (nl_c - 1 - (s - nc_c)))

    return pl.pallas_call(
        _gdn_kernel,
        grid=(geom["b"], steps),
        in_specs=[
            pl.BlockSpec((cs, A_QKV), lambda b, s: (fwd(b, s), 0)),
            pl.BlockSpec((cs, LANE), lambda b, s: (fwd(b, s), 0)),
            pl.BlockSpec((cs, A_QKV), lambda b, s: (bwd(b, s), 0)),
            pl.BlockSpec((cs, LANE), lambda b, s: (bwd(b, s), 0)),
        ],
        out_specs=[
            pl.BlockSpec((cs, GW), lambda b, s: (fwd(b, s), 0)),
            pl.BlockSpec((cs, GW), lambda b, s: (bwd(b, s), 0)),
        ],
        out_shape=[jax.ShapeDtypeStruct((n, GW), F32), jax.ShapeDtypeStruct((n, GW), F32)],
        scratch_shapes=[pltpu.VMEM((GW, GW), F32), pltpu.VMEM((GW, GW), F32)],
        compiler_params=_cparams(("parallel", "arbitrary")),
        name="gdn_scan",
    )(gqkv, gates, gqkv, gates)


def _attn_kernel(qt_ref, kc_ref, kl_ref, vc_ref, vl_ref, o_ref, acc_ref, *, n_lat_q):
    tq = qt_ref.shape[1]
    acc_ref[...] = jnp.zeros(acc_ref.shape, F32)

    rows = [slice(g * HEAD_DIM, (g + 1) * HEAD_DIM) for g in range(ATT_GROUPS)]

    def scores(k_ref, off):
        kc = k_ref[pl.ds(off, TK), :]
        return [_dot(kc, qt_ref[r, :]) for r in rows]

    def update(v_ref, off, sts, stats):
        vc = v_ref[:, pl.ds(off, TK)]
        new_stats, ps, alphas = [], [], []
        for g, st in enumerate(sts):
            m_old, l_old = stats[2 * g], stats[2 * g + 1]
            m_new = jnp.maximum(m_old, jnp.max(st, axis=0, keepdims=True))
            p = jnp.exp2(st - m_new)
            alpha = jnp.exp2(m_old - m_new)
            new_stats += [m_new, alpha * l_old + jnp.sum(p, axis=0, keepdims=True)]
            ps.append(p.astype(BF16))
            alphas.append(alpha)
        pvs = [_dot(vc, p) for p in ps]
        for r, alpha, pv in zip(rows, alphas, pvs):
            acc_ref[r, :] = alpha * acc_ref[r, :] + pv
        return tuple(new_stats)

    def segment(k_ref, v_ref, enabled, stats):
        n_chunks = k_ref.shape[0] // TK
        per_iter = 2 if n_chunks % 2 == 0 else 1

        def body(i, st):
            offs = [pl.multiple_of((i * per_iter + c) * TK, TK) for c in range(per_iter)]
            all_scores = [scores(k_ref, off) for off in offs]
            for off, sts in zip(offs, all_scores):
                st = update(v_ref, off, sts, st)
            return st
        return lax.fori_loop(0, jnp.where(enabled, n_chunks // per_iter, 0), body, stats)

    stats = (jnp.full((1, tq), -jnp.inf, F32), jnp.zeros((1, tq), F32)) * ATT_GROUPS
    stats = segment(kc_ref, vc_ref, True, stats)
    stats = segment(kl_ref, vl_ref, pl.program_id(2) < n_lat_q, stats)

    for g in range(ATT_GROUPS):
        r = slice(g * HEAD_DIM, (g + 1) * HEAD_DIM)
        acc_ref[r, :] = acc_ref[r, :] / stats[2 * g + 1]
    o_ref[...] = acc_ref[...].T.astype(o_ref.dtype)


def _attention(qt, kh, vt, with_ctx_queries, geom):
    b, l, lc, nl = geom["b"], geom["l"], geom["lc"], geom["nl"]
    tq = geom["tq"]
    nq_l, nq_c = l // tq, lc // tq
    nq = nq_l + (nq_c if with_ctx_queries else 0)
    n_rows = nl + (geom["nc"] if with_ctx_queries else 0)
    gw = ATT_GROUPS * HEAD_DIM

    def qblock(bb, i):
        return jnp.where(i < nq_l, bb * nq_l + i, nl // tq + bb * nq_c + (i - nq_l))

    return pl.pallas_call(
        functools.partial(_attn_kernel, n_lat_q=nq_l),
        grid=(b, ATT_KV_HEADS, nq),
        in_specs=[
            pl.BlockSpec((gw, tq), lambda bb, h, i: (h, qblock(bb, i))),
            pl.BlockSpec((None, lc, HEAD_DIM), lambda bb, h, i: (h, nl // lc + bb, 0)),
            pl.BlockSpec((None, l, HEAD_DIM), lambda bb, h, i: (h, bb, 0)),
            pl.BlockSpec((HEAD_DIM, lc), lambda bb, h, i: (h, nl // lc + bb)),
            pl.BlockSpec((HEAD_DIM, l), lambda bb, h, i: (h, bb)),
        ],
        out_specs=pl.BlockSpec((tq, gw), lambda bb, h, i: (qblock(bb, i), h)),
        out_shape=jax.ShapeDtypeStruct((n_rows, ATT_Q), BF16),
        scratch_shapes=[pltpu.VMEM((gw, tq), F32)],
        compiler_params=_cparams(("parallel", "parallel", "arbitrary")),
        name="attention",
    )(qt, kh, kh, vt, vt)


def _outproj_kernel(x_ref, of_ref, ob_ref, z_ref, yb_ref, cb_ref, cc_ref, ch_ref,
                    ccp_ref, chp_ref, ccn_ref, chn_ref, mod_ref, w_ref, gg_ref, cw_ref,
                    o_ref, *, geom):
    j = pl.program_id(0)
    tm = x_ref.shape[0]
    first, last = _seq_flags(j, tm, geom)
    o = of_ref[...] + ob_ref[...]
    ms = _group_sum(o * o, _group_ones(GW)) * (1.0 / GDN_DV)
    ya = o * lax.rsqrt(ms + EPS) * gg_ref[0:1, :] * _silu(z_ref[...])
    u = cc_ref[...] * ch_ref[...]
    hp = jnp.where(first, 0.0, ccp_ref[7:8, :] * chp_ref[7:8, :])
    hn = jnp.where(last, 0.0, ccn_ref[0:1, :] * chn_ref[0:1, :])
    yc = cb_ref[...] * _conv3(u, hp, hn, cw_ref)
    acc = _dot(ya.astype(BF16), w_ref[0:A_Z, :])
    acc += _dot(yb_ref[...], w_ref[A_Z:A_Z + ATT_Q, :])
    acc += _dot(yc.astype(BF16), w_ref[A_Z + ATT_Q:, :])
    o_ref[...] = x_ref[...] + mod_ref[2:3, :] * acc


def _outproj(xs, of, ob, p, yb, mod, w, gg, conv_c, n_rows, geom):
    d = xs.shape[1]
    tm = geom["tm_cv"]
    nlt, tpb = geom["nl"] // tm, geom["l"] // tm
    hb = tm // 8
    nblk8 = xs.shape[0] // 8
    col = lambda w_, c: pl.BlockSpec((tm, w_), lambda j: (j, c // w_))
    prev = lambda c: pl.BlockSpec((8, SC_CH), lambda j: (jnp.maximum(j * hb - 1, 0), c // SC_CH))
    nxt = lambda c: pl.BlockSpec((8, SC_CH), lambda j: (jnp.minimum((j + 1) * hb, nblk8 - 1), c // SC_CH))
    return pl.pallas_call(
        functools.partial(_outproj_kernel, geom=geom),
        grid=(n_rows // tm,),
        in_specs=[
            pl.BlockSpec((tm, d), lambda j: (j, 0)),
            pl.BlockSpec((tm, GW), lambda j: (j, 0)),
            pl.BlockSpec((tm, GW), lambda j: (j, 0)),
            col(A_Z, C_Z),
            pl.BlockSpec((tm, ATT_Q), lambda j: (j, 0)),
            col(SC_CH, C_CB), col(SC_CH, C_CC), col(SC_CH, C_CH),
            prev(C_CC), prev(C_CH), nxt(C_CC), nxt(C_CH),
            pl.BlockSpec((None, 8, d), lambda j: (_mod_row(j, nlt, tpb), 0, 0)),
            pl.BlockSpec((d, d), lambda j: (0, 0)),
            pl.BlockSpec((8, GW), lambda j: (0, 0)),
            pl.BlockSpec((8, SC_CH), lambda j: (0, 0)),
        ],
        out_specs=pl.BlockSpec((tm, d), lambda j: (j, 0)),
        out_shape=jax.ShapeDtypeStruct((n_rows, d), F32),
        compiler_params=_cparams(("parallel",)),
        name="outproj",
    )(xs, of, ob, p, yb, p, p, p, p, p, p, p, mod, w, gg, conv_c)


def _ffn_kernel(x_ref, g_ref, mod_ref, wg_ref, wu_ref, wd_ref, o_ref, h_ref, acc_ref, *, tf):
    x = x_ref[...]
    h_ref[...] = _norm_mod(x, g_ref[...], mod_ref[3:4, :], mod_ref[4:5, :]).astype(BF16)
    acc_ref[...] = jnp.zeros_like(acc_ref)

    def body(c, carry):
        off = pl.multiple_of(c * tf, tf)
        hb = h_ref[...]
        gate = _dot(hb, wg_ref[:, pl.ds(off, tf)])
        up = _dot(hb, wu_ref[:, pl.ds(off, tf)])
        a = (_silu(gate) * up).astype(BF16)
        acc_ref[...] += _dot(a, wd_ref[pl.ds(off, tf), :])
        return carry

    lax.fori_loop(0, wg_ref.shape[1] // tf, body, 0)
    o_ref[...] = x + mod_ref[5:6, :] * acc_ref[...]


def _ffn(xs, g, mod, wg, wu, wd, n_rows, geom):
    d = xs.shape[1]
    ff = wg.shape[1]
    tm = geom["tm_mm"]
    nlt, tpb = geom["nl"] // tm, geom["l"] // tm
    return pl.pallas_call(
        functools.partial(_ffn_kernel, tf=256),
        grid=(n_rows // tm,),
        in_specs=[
            pl.BlockSpec((tm, d), lambda j: (j, 0)),
            pl.BlockSpec((1, d), lambda j: (0, 0)),
            pl.BlockSpec((None, 8, d), lambda j: (_mod_row(j, nlt, tpb), 0, 0)),
            pl.BlockSpec((d, ff), lambda j: (0, 0)),
            pl.BlockSpec((d, ff), lambda j: (0, 0)),
            pl.BlockSpec((ff, d), lambda j: (0, 0)),
        ],
        out_specs=pl.BlockSpec((tm, d), lambda j: (j, 0)),
        out_shape=jax.ShapeDtypeStruct((n_rows, d), F32),
        scratch_shapes=[pltpu.VMEM((tm, d), BF16), pltpu.VMEM((tm, d), F32)],
        compiler_params=_cparams(("parallel",)),
        name="ffn",
    )(xs, g, mod, wg, wu, wd)


def _moe_kernel(x_ref, g_ref, mod_ref, r_ref, wgu_ref, wd_ref, o_ref, h_ref, comb_ref, acc_ref):
    e = pl.program_id(1)
    ffe = wd_ref.shape[0]

    @pl.when(e == 0)
    def _():
        h = _norm_mod(x_ref[...], g_ref[...], mod_ref[3:4, :], mod_ref[4:5, :])
        h_ref[...] = h.astype(BF16)
        acc_ref[...] = jnp.zeros_like(acc_ref)
        logits = jnp.dot(h, r_ref[...], precision=lax.Precision.HIGHEST, preferred_element_type=F32)
        lane = lax.broadcasted_iota(jnp.int32, logits.shape, 1).astype(F32)
        valid = lane < N_EXPERTS
        logits = jnp.where(valid, logits, -jnp.inf)
        ex = jnp.exp(logits - jnp.max(logits, axis=-1, keepdims=True))
        probs = jnp.where(valid, ex / jnp.sum(ex, axis=-1, keepdims=True), -1.0)
        p1 = jnp.max(probs, axis=-1, keepdims=True)
        i1 = jnp.min(jnp.where(probs == p1, lane, float(LANE)), axis=-1, keepdims=True)
        rest = jnp.where(lane == i1, -1.0, probs)
        p2 = jnp.max(rest, axis=-1, keepdims=True)
        i2 = jnp.min(jnp.where(rest == p2, lane, float(LANE)), axis=-1, keepdims=True)
        den = p1 + p2
        comb_ref[...] = jnp.where(lane == i1, p1 / den, 0.0) + jnp.where(lane == i2, p2 / den, 0.0)

    hb = h_ref[...]
    gu = _dot(hb, wgu_ref[...])
    a = (_silu(gu[:, :ffe]) * gu[:, ffe:]).astype(BF16)
    y = _dot(a, wd_ref[...])
    comb = comb_ref[...]
    lane = lax.broadcasted_iota(jnp.int32, comb.shape, 1)
    ce = jnp.sum(jnp.where(lane == e, comb, 0.0), axis=-1, keepdims=True)
    acc_ref[...] += ce * y

    @pl.when(e == pl.num_programs(1) - 1)
    def _():
        o_ref[...] = x_ref[...] + mod_ref[5:6, :] * acc_ref[...]


def _moe(xs, g, mod, router, wgu, wd, n_rows, geom):
    d = xs.shape[1]
    ne, ffe = wd.shape[0], wd.shape[1]
    tm = geom["tm_mm"]
    nlt, tpb = geom["nl"] // tm, geom["l"] // tm
    return pl.pallas_call(
        _moe_kernel,
        grid=(n_rows // tm, ne),
        in_specs=[
            pl.BlockSpec((tm, d), lambda j, e: (j, 0)),
            pl.BlockSpec((1, d), lambda j, e: (0, 0)),
            pl.BlockSpec((None, 8, d), lambda j, e: (_mod_row(j, nlt, tpb), 0, 0)),
            pl.BlockSpec((d, LANE), lambda j, e: (0, 0)),
            pl.BlockSpec((None, d, 2 * ffe), lambda j, e: (e, 0, 0)),
            pl.BlockSpec((None, ffe, d), lambda j, e: (e, 0, 0)),
        ],
        out_specs=pl.BlockSpec((tm, d), lambda j, e: (j, 0)),
        out_shape=jax.ShapeDtypeStruct((n_rows, d), F32),
        scratch_shapes=[pltpu.VMEM((tm, d), BF16), pltpu.VMEM((tm, LANE), F32), pltpu.VMEM((tm, d), F32)],
        compiler_params=_cparams(("parallel", "arbitrary")),
        name="moe",
    )(xs, g, mod, router, wgu, wd)


def _final_norm_kernel(x_ref, g_ref, o_ref):
    x = x_ref[...]
    ms = jnp.mean(x * x, axis=-1, keepdims=True)
    o_ref[...] = x * lax.rsqrt(ms + EPS) * g_ref[...]


def _final_norm(xs, g, n_rows, geom):
    d = xs.shape[1]
    tm = geom["tm_mm"]
    return pl.pallas_call(
        _final_norm_kernel,
        grid=(n_rows // tm,),
        in_specs=[pl.BlockSpec((tm, d), lambda j: (j, 0)), pl.BlockSpec((1, d), lambda j: (0, 0))],
        out_specs=pl.BlockSpec((tm, d), lambda j: (j, 0)),
        out_shape=jax.ShapeDtypeStruct((n_rows, d), F32),
        compiler_params=_cparams(("parallel",)),
        name="final_norm",
    )(xs, g)


def _rope_tables(l, tm):
    quarter = HEAD_DIM // 4
    inv_freq = ROPE_THETA ** (-jnp.arange(quarter, dtype=F32) / quarter)
    t = jnp.arange(l, dtype=jnp.int32)
    ang_r = (t // GRID_W).astype(F32)[:, None] * inv_freq[None, :]
    ang_c = (t % GRID_W).astype(F32)[:, None] * inv_freq[None, :]
    cos = jnp.concatenate([jnp.cos(ang_r)] * 2 + [jnp.cos(ang_c)] * 2, axis=1)
    sin = jnp.concatenate([-jnp.sin(ang_r), jnp.sin(ang_r), -jnp.sin(ang_c), jnp.sin(ang_c)], axis=1)
    cos = jnp.tile(cos, (1, LANE // HEAD_DIM))
    sin = jnp.tile(sin, (1, LANE // HEAD_DIM))
    cos = jnp.concatenate([jnp.ones((tm, LANE), F32), cos], axis=0)
    sin = jnp.concatenate([jnp.zeros((tm, LANE), F32), sin], axis=0)
    return cos, sin


def _pad_rows(a, rows=8):
    return jnp.pad(a, ((0, rows - a.shape[0]), (0, 0)))


def _permute_w_in(w):
    o = 0
    parts = {}
    for name, size in (("qkv", A_QKV), ("z", A_Z), ("a", 2 * GDN_HEADS), ("b", 2 * GDN_HEADS),
                       ("q", ATT_Q), ("k", ATT_KV), ("v", ATT_KV), ("cb", SC_CH), ("cc", SC_CH), ("ch", SC_CH)):
        parts[name] = w[:, o:o + size]
        o += size
    pad = jnp.zeros((w.shape[0], P_COLS - C_AB - 4 * GDN_HEADS), w.dtype)
    cols = [parts[k] for k in ("qkv", "z", "q", "k", "v", "cb", "cc", "ch", "a", "b")] + [pad]
    return jnp.concatenate(cols, axis=1).astype(BF16)


def kernel(x, c, ctx, c_ctx, w_mod, b_mod, norm1, norm2, w_in, conv_a, a_log, dt_bias, gdn_norm,
           q_norm, k_norm, conv_c, w_out, ffn_w_gate, ffn_w_up, ffn_w_down, router,
           moe_w_gate, moe_w_up, moe_w_down, norm_f):
    b, l, d = x.shape
    lc = ctx.shape[1]
    depth = w_mod.shape[0]
    nl, nc = b * l, b * lc
    tm_mm = math.gcd(TM_MM, math.gcd(l, nc))
    tm_cv = math.gcd(TM_CV, math.gcd(l, lc))
    assert l % GRID_W == 0 and l % TK == 0 and lc % TK == 0 and nl % lc == 0
    tq = math.gcd(TQ, math.gcd(l, lc))
    gdn_rows = math.gcd(GDN_CHUNKS_PER_STEP * GDN_CHUNK, math.gcd(l, lc))
    geom = dict(b=b, l=l, lc=lc, nl=nl, nc=nc, tm_mm=tm_mm, tm_cv=tm_cv, tq=tq, gdn_rows=gdn_rows)
    n = nl + nc

    mod = _modulation(c, c_ctx, w_mod, b_mod)
    cos_t, sin_t = _rope_tables(l, tm_cv)
    xs = jnp.concatenate([x.reshape(nl, d), ctx.reshape(nc, d)], axis=0)

    for li in range(depth):
        last = li == depth - 1
        rows_out = nl if last else n
        w_in_p = _permute_w_in(w_in[li])
        gparams = _pad_rows(jnp.stack([jnp.pad(a_log[li].reshape(-1), (0, LANE - 2 * GDN_HEADS)),
                                       jnp.pad(dt_bias[li].reshape(-1), (0, LANE - 2 * GDN_HEADS))]))
        p = _inproj(xs, norm1[li][None, :], mod[li], w_in_p, geom)
        gqkv, gates, qt, kh, vt = _prep(
            p, cos_t, sin_t, _pad_rows(conv_a[li]), gparams,
            _pad_rows(jnp.tile(q_norm[li], ATT_HEADS)[None, :]),
            _pad_rows(jnp.tile(k_norm[li], ATT_KV_HEADS)[None, :]), geom)
        of, ob = _gdn(gqkv, gates, geom)
        yb = _attention(qt, kh, vt, not last, geom)
        xs = _outproj(xs, of, ob, p, yb, mod[li], w_out[li].astype(BF16),
                      _pad_rows(jnp.tile(gdn_norm[li], GDN_HEADS)[None, :]), _pad_rows(conv_c[li]),
                      rows_out, geom)
        i = li // 2
        if li % 2 == 0:
            xs = _ffn(xs, norm2[li][None, :], mod[li], ffn_w_gate[i].astype(BF16),
                      ffn_w_up[i].astype(BF16), ffn_w_down[i].astype(BF16), rows_out, geom)
        else:
            wgu = jnp.concatenate([moe_w_gate[i], moe_w_up[i]], axis=-1).astype(BF16)
            xs = _moe(xs, norm2[li][None, :], mod[li],
                      jnp.pad(router[i], ((0, 0), (0, LANE - N_EXPERTS))), wgu,
                      moe_w_down[i].astype(BF16), rows_out, geom)
    out = _final_norm(xs, norm_f[None, :], nl, geom)
    return out.reshape(b, l, d)
```

```python
import functools
import math

import jax
import jax.numpy as jnp
from jax import lax
from jax.experimental import pallas as pl
from jax.experimental.pallas import tpu as pltpu

F32 = jnp.float32
BF16 = jnp.bfloat16
EPS = 1e-6

GRID_W = 64
HEAD_DIM = 64
GDN_HEADS = 4
GDN_DK = 64
GDN_DV = 64
GDN_CHUNK = 64
ATT_HEADS = 8
ATT_KV_HEADS = 2
ATT_GROUPS = ATT_HEADS // ATT_KV_HEADS
ROPE_THETA = 10000.0
SC_CH = 256
N_EXPERTS = 8
TOP_K = 2
A_QKV = GDN_HEADS * (2 * GDN_DK + GDN_DV)
A_Z = GDN_HEADS * GDN_DV
GW = GDN_HEADS * GDN_DK
ATT_Q = ATT_HEADS * HEAD_DIM
ATT_KV = ATT_KV_HEADS * HEAD_DIM

C_QKV = 0
C_Z = 768
C_Q = 1024
C_K = 1536
C_V = 1664
C_CB = 1792
C_CC = 2048
C_CH = 2304
C_AB = 2560
P_COLS = 2688

VMEM_LIMIT = 56 * 1024 * 1024
LANE = 128

TM_MM = 512
TM_CV = 256
TQ = 256
TK = 256
V_ROWS = HEAD_DIM + 16
ATT_CHUNKS_PER_ITER = 4
GDN_CHUNKS_PER_STEP = 4
GDN_BATCH_PER_STEP = 2
FFN_SPLIT = 2


def _cparams(sem):
    return pltpu.CompilerParams(dimension_semantics=sem, vmem_limit_bytes=VMEM_LIMIT)


def _sigmoid(x):
    return 1.0 / (1.0 + jnp.exp(-x))


def _silu(x):
    return x * _sigmoid(x)


def _dot(a, b):
    return jnp.dot(a, b, preferred_element_type=F32)


def _group_sum(x2, gmat):
    hi = x2.astype(BF16)
    lo = (x2 - hi.astype(F32)).astype(BF16)
    return _dot(hi, gmat) + _dot(lo, gmat)


def _group_ones(width):
    r = lax.broadcasted_iota(jnp.int32, (width, width), 0) // 64
    c = lax.broadcasted_iota(jnp.int32, (width, width), 1) // 64
    return jnp.where(r == c, 1.0, 0.0).astype(BF16)


def _mod_kernel(s_ref, w_ref, b_ref, o_ref):
    s = _silu(s_ref[...])
    o_ref[...] = jnp.dot(s, w_ref[...], precision=lax.Precision.HIGHEST,
                         preferred_element_type=F32) + b_ref[...]


def _modulation(c, c_ctx, w_mod, b_mod):
    depth, d, d6 = w_mod.shape
    rows = c.shape[0] + 1
    rpad = -(-rows // 8) * 8
    s = jnp.concatenate([c_ctx[None, :], c, jnp.zeros((rpad - rows, d), F32)], axis=0)
    tn = 1536
    out = pl.pallas_call(
        _mod_kernel,
        grid=(depth, d6 // tn),
        in_specs=[
            pl.BlockSpec((rpad, d), lambda l, j: (0, 0)),
            pl.BlockSpec((None, d, tn), lambda l, j: (l, 0, j)),
            pl.BlockSpec((None, 1, tn), lambda l, j: (l, 0, j)),
        ],
        out_specs=pl.BlockSpec((None, rpad, tn), lambda l, j: (l, 0, j)),
        out_shape=jax.ShapeDtypeStruct((depth, rpad, d6), F32),
        compiler_params=_cparams(("parallel", "parallel")),
        name="modulation",
    )(s, w_mod, b_mod.reshape(depth, 1, d6))
    m = out[:, :rows].reshape(depth, rows, 6, d)
    return jnp.pad(m, ((0, 0), (0, 0), (0, 2), (0, 0)))


def _norm_mod(x, g, shift, scale):
    ms = jnp.mean(x * x, axis=-1, keepdims=True)
    y = x * lax.rsqrt(ms + EPS) * g
    return y * (1.0 + scale) + shift


def _inproj_kernel(x_ref, g_ref, mod_ref, w_ref, o_ref):
    h = _norm_mod(x_ref[...], g_ref[...], mod_ref[0:1, :], mod_ref[1:2, :]).astype(BF16)
    ncol = o_ref.shape[1]
    for c0 in range(0, ncol, 256):
        c1 = min(c0 + 256, ncol)
        o_ref[:, c0:c1] = _dot(h, w_ref[:, c0:c1])


def _mod_row(j, n_lat_tiles, tiles_per_batch):
    return jnp.where(j < n_lat_tiles, 1 + j // tiles_per_batch, 0)


def _inproj(xs, g, mod, w, geom):
    n, d = xs.shape
    tm = geom["tm_mm"]
    nlt, tpb = geom["nl"] // tm, geom["l"] // tm
    return pl.pallas_call(
        _inproj_kernel,
        grid=(n // tm,),
        in_specs=[
            pl.BlockSpec((tm, d), lambda j: (j, 0)),
            pl.BlockSpec((1, d), lambda j: (0, 0)),
            pl.BlockSpec((None, 8, d), lambda j: (_mod_row(j, nlt, tpb), 0, 0)),
            pl.BlockSpec((d, P_COLS), lambda j: (0, 0)),
        ],
        out_specs=pl.BlockSpec((tm, P_COLS), lambda j: (j, 0)),
        out_shape=jax.ShapeDtypeStruct((n, P_COLS), F32),
        compiler_params=_cparams(("parallel",)),
        name="inproj",
    )(xs, g, mod, w)


def _seq_flags(j, tm, geom):
    row0 = j * tm
    is_lat = row0 < geom["nl"]
    pos = jnp.where(is_lat, row0 % geom["l"], (row0 - geom["nl"]) % geom["lc"])
    slen = jnp.where(is_lat, geom["l"], geom["lc"])
    return pos == 0, pos + tm == slen


def _conv3(u, prev_row, next_row, w_ref):
    tm = u.shape[0]
    rid = lax.broadcasted_iota(jnp.int32, u.shape, 0)
    up = jnp.where(rid == 0, prev_row, pltpu.roll(u, 1, 0))
    un = jnp.where(rid == tm - 1, next_row, pltpu.roll(u, tm - 1, 0))
    return w_ref[0:1, :] * up + w_ref[1:2, :] * u + w_ref[2:3, :] * un


def _softplus(x):
    return jnp.maximum(x, 0.0) + jnp.log(1.0 + jnp.exp(-jnp.abs(x)))


def _gdn_tile_perm(j, geom):
    cs, nb = geom["gdn_rows"], geom["gdn_nb"]
    nlt, tpb, tpc = geom["nl"] // cs, geom["l"] // cs, geom["lc"] // cs
    jc = j - nlt
    lat = ((j // tpb // nb) * tpb + j % tpb) * nb + (j // tpb) % nb
    ctx = nlt + ((jc // tpc // nb) * tpc + jc % tpc) * nb + (jc // tpc) % nb
    return jnp.where(j < nlt, lat, ctx)


def _prep_kernel(qkv_ref, qkvp_ref, qkvn_ref, ab_ref, q_ref, k_ref, v_ref, cos_ref, sin_ref,
                 cw_ref, gp_ref, qg_ref, kg_ref,
                 gqkv_ref, gate_ref, qt_ref, kh_ref, vt_ref, *, geom):
    j = pl.program_id(0)
    tm = qkv_ref.shape[0]
    first, last = _seq_flags(j, tm, geom)
    u = qkv_ref[...]
    hp = jnp.where(first, 0.0, qkvp_ref[7:8, :])
    hn = jnp.where(last, 0.0, qkvn_ref[0:1, :])
    s = _silu(_conv3(u, hp, hn, cw_ref))
    g256 = _group_ones(GW)
    qg = s[:, 0:GW]
    kg = s[:, GW:2 * GW]
    qg = qg * lax.rsqrt(_group_sum(qg * qg, g256) + EPS) * (GDN_DK ** -0.5)
    kg = kg * lax.rsqrt(_group_sum(kg * kg, g256) + EPS)
    gqkv_ref[:, 0:GW] = qg
    gqkv_ref[:, GW:2 * GW] = kg
    gqkv_ref[:, 2 * GW:3 * GW] = s[:, 2 * GW:3 * GW]
    ab = ab_ref[...]
    lane = lax.broadcasted_iota(jnp.int32, ab.shape, 1)
    log_a = -jnp.exp(gp_ref[0:1, :]) * _softplus(ab + gp_ref[1:2, :])
    gate_ref[...] = jnp.where(lane < 2 * GDN_HEADS, log_a, _sigmoid(ab))
    cos = cos_ref[...]
    sin = sin_ref[...]
    lane128 = lax.broadcasted_iota(jnp.int32, cos.shape, 1)
    first_half = (lane128 % 32) < 16

    def rope(xb):
        r_lo = pltpu.roll(xb, 16, 1)
        r_hi = pltpu.roll(xb, 112, 1)
        return xb * cos + jnp.where(first_half, r_hi, r_lo) * sin

    q = q_ref[...]
    q = q * lax.rsqrt(_group_sum(q * q, _group_ones(ATT_Q)) * (1.0 / HEAD_DIM) + EPS) * qg_ref[0:1, :]
    qr = jnp.concatenate([rope(q[:, c:c + LANE]) for c in range(0, ATT_Q, LANE)], axis=1)
    qt_ref[...] = (qr * (HEAD_DIM ** -0.5 * math.log2(math.e))).T.astype(BF16)
    k = k_ref[...]
    k = k * lax.rsqrt(_group_sum(k * k, _group_ones(ATT_KV)) * (1.0 / HEAD_DIM) + EPS) * kg_ref[0:1, :]
    kr = rope(k).astype(BF16)
    for h in range(ATT_KV_HEADS):
        kh_ref[h] = kr[:, h * HEAD_DIM:(h + 1) * HEAD_DIM]
    vt = v_ref[...].T
    pad_row = lax.broadcasted_iota(jnp.int32, (V_ROWS - HEAD_DIM, tm), 0)
    ones_rows = jnp.where(pad_row == 0, 1.0, 0.0)
    vt_ref[...] = jnp.concatenate(
        [part for h in range(ATT_KV_HEADS) for part in (vt[h * HEAD_DIM:(h + 1) * HEAD_DIM, :], ones_rows)],
        axis=0).astype(BF16)


def _prep(p, cos_t, sin_t, conv_a, gparams, qg, kg, geom):
    n = p.shape[0]
    tm = geom["tm_cv"]
    nlt, tpb = geom["nl"] // tm, geom["l"] // tm
    hb = tm // 8
    nblk8 = n // 8

    def rope_idx(j):
        return jnp.where(j < nlt, 1 + j % tpb, 0)

    col = lambda w, c: pl.BlockSpec((tm, w), lambda j: (j, c // w))
    return pl.pallas_call(
        functools.partial(_prep_kernel, geom=geom),
        grid=(n // tm,),
        in_specs=[
            col(A_QKV, C_QKV),
            pl.BlockSpec((8, A_QKV), lambda j: (jnp.maximum(j * hb - 1, 0), 0)),
            pl.BlockSpec((8, A_QKV), lambda j: (jnp.minimum((j + 1) * hb, nblk8 - 1), 0)),
            col(LANE, C_AB),
            col(ATT_Q, C_Q),
            col(ATT_KV, C_K),
            col(ATT_KV, C_V),
            pl.BlockSpec((tm, LANE), lambda j: (rope_idx(j), 0)),
            pl.BlockSpec((tm, LANE), lambda j: (rope_idx(j), 0)),
            pl.BlockSpec((8, A_QKV), lambda j: (0, 0)),
            pl.BlockSpec((8, LANE), lambda j: (0, 0)),
            pl.BlockSpec((8, ATT_Q), lambda j: (0, 0)),
            pl.BlockSpec((8, ATT_KV), lambda j: (0, 0)),
        ],
        out_specs=[
            pl.BlockSpec((tm, A_QKV), lambda j: (_gdn_tile_perm(j, geom), 0)),
            pl.BlockSpec((tm, LANE), lambda j: (_gdn_tile_perm(j, geom), 0)),
            pl.BlockSpec((ATT_Q, tm), lambda j: (0, j)),
            pl.BlockSpec((ATT_KV_HEADS, tm, HEAD_DIM), lambda j: (0, j, 0)),
            pl.BlockSpec((ATT_KV_HEADS * V_ROWS, tm), lambda j: (0, j)),
        ],
        out_shape=[
            jax.ShapeDtypeStruct((n, A_QKV), F32),
            jax.ShapeDtypeStruct((n, LANE), F32),
            jax.ShapeDtypeStruct((ATT_Q, n), BF16),
            jax.ShapeDtypeStruct((ATT_KV_HEADS, n, HEAD_DIM), BF16),
            jax.ShapeDtypeStruct((ATT_KV_HEADS * V_ROWS, n), BF16),
        ],
        compiler_params=_cparams(("parallel",)),
        name="mixer_prep",
    )(p, p, p, p, p, p, p, cos_t, sin_t, conv_a, gparams, qg, kg)


def _head_of_lane(shape):
    return lax.broadcasted_iota(jnp.int32, shape, 1) // 64


def _block_diag(xb):
    hl = _head_of_lane(xb.shape)
    zero = jnp.zeros_like(xb)
    return jnp.concatenate([jnp.where(hl == h, xb, zero) for h in range(GDN_HEADS)], axis=0)


def _dot_exact3(a_bf, x):
    h1 = x.astype(BF16)
    r1 = x - h1.astype(F32)
    h2 = r1.astype(BF16)
    h3 = (r1 - h2.astype(F32)).astype(BF16)
    return _dot(a_bf, h1) + _dot(a_bf, h2) + _dot(a_bf, h3)


def _gdn_local(insts):
    cs = GDN_CHUNK
    hl = _head_of_lane((cs, GW))
    row = lax.broadcasted_iota(jnp.int32, (cs, GW), 0)
    col = lax.broadcasted_iota(jnp.int32, (cs, GW), 1) % 64
    ti = lax.broadcasted_iota(jnp.int32, (cs, cs), 0)
    tj = lax.broadcasted_iota(jnp.int32, (cs, cs), 1)
    same16 = (row // 16) == (col // 16)
    same32 = (row // 32) == (col // 32)
    eye = jnp.where(row == col, 1.0, 0.0)
    nt = (((1,), (1,)), ((), ()))

    def widen(gates, base):
        out = jnp.zeros((cs, GW), F32)
        for h in range(GDN_HEADS):
            out = jnp.where(hl == h, gates[:, base + h:base + h + 1], out)
        return out

    for it in insts:
        rev = it["rev"]
        d = 1 if rev else 0
        it["la_w"] = widen(it["gates"], GDN_HEADS * d)
        it["beta_w"] = widen(it["gates"], 2 * GDN_HEADS + GDN_HEADS * d)
        it["tri"] = jnp.where((tj >= ti) if rev else (tj <= ti), 1.0, 0.0).astype(BF16)
        it["incl"] = (row <= col) if rev else (row >= col)
        it["strict"] = (row < col) if rev else (row > col)
        it["kb"] = it["k"].astype(BF16)
        it["kbd"] = _block_diag(it["kb"])
    for it in insts:
        kq = lax.dot_general(jnp.concatenate([it["kb"], it["q"].astype(BF16)], axis=0), it["kbd"], nt,
                             preferred_element_type=F32)
        it["kk_w"], it["qk_w"] = kq[0:cs, :], kq[cs:2 * cs, :]
    for it in insts:
        it["gcc"] = _dot_exact3(it["tri"], it["la_w"])
    for it in insts:
        rev, la_w, incl = it["rev"], it["la_w"], it["incl"]
        gcr = jnp.sum(jnp.where((row >= col) if rev else (row <= col), la_w, 0.0), axis=0, keepdims=True)
        it["decay"] = jnp.where(incl, jnp.exp(jnp.where(incl, it["gcc"] - gcr, 0.0)), 0.0)
        it["lower"] = jnp.where(it["strict"], it["beta_w"] * it["kk_w"] * it["decay"], 0.0)
        pw = jnp.where(same16, it["lower"], 0.0)
        it["t"] = eye - pw
        it["pwb"] = pw.astype(BF16)
    for it in insts:
        it["pwb"] = _dot(it["pwb"], _block_diag(it["pwb"])).astype(BF16)
    for _ in range(2):
        for it in insts:
            both = _dot(jnp.concatenate([it["pwb"], it["t"].astype(BF16)], axis=0), _block_diag(it["pwb"]))
            it["pwb"] = both[0:cs, :].astype(BF16)
            it["t"] = it["t"] + both[cs:2 * cs, :]
    for it in insts:
        it["t"] = it["t"] + _dot(it["t"].astype(BF16), _block_diag(it["pwb"]))
    for level in range(2):
        for it in insts:
            off = (jnp.where(same32 & jnp.logical_not(same16), it["lower"], 0.0) if level == 0
                   else jnp.where(same32, 0.0, it["lower"]))
            it["tb"] = it["t"].astype(BF16)
            it["tc"] = _dot(it["tb"], _block_diag(off.astype(BF16)))
        for it in insts:
            it["t"] = it["t"] - _dot(it["tc"].astype(BF16), _block_diag(it["tb"]))
    for it in insts:
        it["tb"] = it["t"].astype(BF16)
        it["egc"] = jnp.exp(it["gcc"])
        it["u"] = _dot(it["tb"], _block_diag((it["v"] * it["beta_w"]).astype(BF16)))
    for it in insts:
        it["w"] = _dot(it["tb"], _block_diag((it["k"] * (it["beta_w"] * it["egc"])).astype(BF16)))
    for it in insts:
        gcc = it["gcc"]
        gl = gcc[0:1, :] if it["rev"] else gcc[cs - 1:cs, :]
        it["qkm"] = jnp.where(it["incl"], it["qk_w"] * it["decay"], 0.0).astype(BF16)
        it["q_dec"] = (it["q"] * it["egc"]).astype(BF16)
        it["k_dec"] = (it["k"] * jnp.exp(gl - gcc)).astype(BF16)
        it["a_last"] = jnp.exp(gl)
        it["wb"] = it["w"].astype(BF16)


def _gdn_recurrence(group):
    r2 = lax.broadcasted_iota(jnp.int32, (GW, GW), 0) // 64
    c2 = lax.broadcasted_iota(jnp.int32, (GW, GW), 1) // 64
    tn = (((0,), (0,)), ((), ()))
    for it in group:
        it["s"] = it["s_ref"][...]
        it["sb"] = it["s"].astype(BF16)
    for it in group:
        it["wq_s"] = _dot(jnp.concatenate([it["wb"], it["q_dec"]], axis=0), it["sb"])
    for it in group:
        it["vb"] = (it["u"] - it["wq_s"][0:GDN_CHUNK, :]).astype(BF16)
    for it in group:
        it["o"] = it["wq_s"][GDN_CHUNK:2 * GDN_CHUNK, :] + _dot(it["qkm"], _block_diag(it["vb"]))
    for it in group:
        upd = lax.dot_general(it["k_dec"], it["vb"], tn, preferred_element_type=F32)
        it["s_ref"][...] = it["s"] * it["a_last"] + jnp.where(r2 == c2, upd, 0.0)
    for it in group:
        it["o_ref"][it["rows"], :] = it["o"]


def _gdn_kernel(xf_ref, gf_ref, xb_ref, gb_ref, of_ref, ob_ref, *s_refs, seq_rows):
    @pl.when(pl.program_id(1) == 0)
    def _():
        for s_ref in s_refs:
            s_ref[...] = jnp.zeros_like(s_ref)

    n_chunks = seq_rows // GDN_CHUNK
    scans = []
    for slot in range(xf_ref.shape[0] // seq_rows):
        for x_ref, g_ref, o_ref, rev in ((xf_ref, gf_ref, of_ref, False), (xb_ref, gb_ref, ob_ref, True)):
            seq = []
            for ci in (reversed(range(n_chunks)) if rev else range(n_chunks)):
                r0 = slot * seq_rows + ci * GDN_CHUNK
                rows = slice(r0, r0 + GDN_CHUNK)
                seq.append(dict(rev=rev, rows=rows, o_ref=o_ref, s_ref=s_refs[len(scans)], gates=g_ref[rows, :],
                                q=x_ref[rows, 0:GW], k=x_ref[rows, GW:2 * GW], v=x_ref[rows, 2 * GW:3 * GW]))
            scans.append(seq)
    _gdn_local([it for group in zip(*scans) for it in group])
    for group in zip(*scans):
        _gdn_recurrence(list(group))


def _gdn(gqkv, gates, geom):
    n = gqkv.shape[0]
    cs, nb = geom["gdn_rows"], geom["gdn_nb"]
    nl_c, nc_c = geom["l"] // cs, geom["lc"] // cs
    lat_blocks = geom["nl"] // (cs * nb)
    steps = nl_c + nc_c

    def fwd(g, s):
        return jnp.where(s < nc_c, lat_blocks + g * nc_c + s, g * nl_c + s - nc_c)

    def bwd(g, s):
        return jnp.where(s < nc_c, lat_blocks + g * nc_c + (nc_c - 1 - s), g * nl_c + (nl_c - 1 - (s - nc_c)))

    return pl.pallas_call(
        functools.partial(_gdn_kernel, seq_rows=cs),
        grid=(geom["b"] // nb, steps),
        in_specs=[
            pl.BlockSpec((nb * cs, A_QKV), lambda g, s: (fwd(g, s), 0)),
            pl.BlockSpec((nb * cs, LANE), lambda g, s: (fwd(g, s), 0)),
            pl.BlockSpec((nb * cs, A_QKV), lambda g, s: (bwd(g, s), 0)),
            pl.BlockSpec((nb * cs, LANE), lambda g, s: (bwd(g, s), 0)),
        ],
        out_specs=[
            pl.BlockSpec((nb * cs, GW), lambda g, s: (fwd(g, s), 0)),
            pl.BlockSpec((nb * cs, GW), lambda g, s: (bwd(g, s), 0)),
        ],
        out_shape=[jax.ShapeDtypeStruct((n, GW), F32), jax.ShapeDtypeStruct((n, GW), F32)],
        scratch_shapes=[pltpu.VMEM((GW, GW), F32)] * (2 * nb),
        compiler_params=_cparams(("parallel", "arbitrary")),
        name="gdn_scan",
    )(gqkv, gates, gqkv, gates)


def _attn_kernel(*refs, lat_queries):
    if lat_queries:
        qt_ref, kc_ref, kl_ref, vc_ref, vl_ref, o_ref, acc_ref, st_ref = refs
    else:
        qt_ref, kc_ref, vc_ref, o_ref, acc_ref = refs
    tq = qt_ref.shape[1]
    acc_ref[...] = jnp.zeros(acc_ref.shape, F32)
    qrows = [slice(g * HEAD_DIM, (g + 1) * HEAD_DIM) for g in range(ATT_GROUPS)]
    arows = [slice(g * V_ROWS, (g + 1) * V_ROWS) for g in range(ATT_GROUPS)]

    def scores(k_ref, off):
        kc = k_ref[pl.ds(off, TK), :]
        return [_dot(kc, qt_ref[r, :]) for r in qrows]

    def update(v_ref, off, sts, ms):
        vc = v_ref[:, pl.ds(off, TK)]
        new_ms, ps, alphas = [], [], []
        for st, m_old in zip(sts, ms):
            m_new = jnp.maximum(m_old, jnp.max(st, axis=0, keepdims=True))
            ps.append(jnp.exp2(st - m_new).astype(BF16))
            alphas.append(jnp.exp2(m_old - m_new))
            new_ms.append(m_new)
        pvs = [_dot(vc, p) for p in ps]
        for r, alpha, pv in zip(arows, alphas, pvs):
            acc_ref[r, :] = alpha * acc_ref[r, :] + pv
        return tuple(new_ms)

    ms = (jnp.full((1, tq), -jnp.inf, F32),) * ATT_GROUPS
    n_ctx = kc_ref.shape[0] // TK
    cur = scores(kc_ref, 0)
    for c in range(1, n_ctx):
        nxt = scores(kc_ref, c * TK)
        ms = update(vc_ref, (c - 1) * TK, cur, ms)
        cur = nxt
    if lat_queries:
        n_lat = kl_ref.shape[0] // TK
        nxt = scores(kl_ref, 0)
        ms = update(vc_ref, (n_ctx - 1) * TK, cur, ms)
        for g in range(ATT_GROUPS):
            st_ref[g] = nxt[g]
        per_iter = math.gcd(ATT_CHUNKS_PER_ITER, n_lat)

        def body(i, ms):
            cur = [st_ref[g] for g in range(ATT_GROUPS)]
            for c in range(per_iter):
                off = pl.multiple_of((i * per_iter + c) * TK, TK)
                off_next = pl.multiple_of(jnp.minimum(off + TK, (n_lat - 1) * TK), TK)
                nxt = scores(kl_ref, off_next)
                ms = update(vl_ref, off, cur, ms)
                cur = nxt
            for g in range(ATT_GROUPS):
                st_ref[g] = cur[g]
            return ms

        ms = lax.fori_loop(0, n_lat // per_iter, body, ms)
    else:
        ms = update(vc_ref, (n_ctx - 1) * TK, cur, ms)

    out = [acc_ref[r, :][0:HEAD_DIM, :] / acc_ref[r, :][HEAD_DIM:HEAD_DIM + 1, :] for r in arows]
    o_ref[...] = jnp.concatenate(out, axis=0).T.astype(o_ref.dtype)


def _attention(qt, kh, vt, lat_queries, geom):
    b, l, lc, nl = geom["b"], geom["l"], geom["lc"], geom["nl"]
    tq = geom["tq"]
    nq = (l if lat_queries else lc) // tq
    qbase = 0 if lat_queries else nl // tq
    gw = ATT_GROUPS * HEAD_DIM
    k_ctx = pl.BlockSpec((None, lc, HEAD_DIM), lambda bb, h, i: (h, nl // lc + bb, 0))
    v_ctx = pl.BlockSpec((V_ROWS, lc), lambda bb, h, i: (h, nl // lc + bb))
    k_lat = pl.BlockSpec((None, l, HEAD_DIM), lambda bb, h, i: (h, bb, 0))
    v_lat = pl.BlockSpec((V_ROWS, l), lambda bb, h, i: (h, bb))
    scratch = [pltpu.VMEM((ATT_GROUPS * V_ROWS, tq), F32)]
    if lat_queries:
        kv_specs, kv_args = [k_ctx, k_lat, v_ctx, v_lat], (kh, kh, vt, vt)
        scratch.append(pltpu.VMEM((ATT_GROUPS, TK, tq), F32))
    else:
        kv_specs, kv_args = [k_ctx, v_ctx], (kh, vt)
    return pl.pallas_call(
        functools.partial(_attn_kernel, lat_queries=lat_queries),
        grid=(b, ATT_KV_HEADS, nq),
        in_specs=[pl.BlockSpec((gw, tq), lambda bb, h, i: (h, qbase + bb * nq + i))] + kv_specs,
        out_specs=pl.BlockSpec((tq, gw), lambda bb, h, i: (bb * nq + i, h)),
        out_shape=jax.ShapeDtypeStruct((b * nq * tq, ATT_Q), BF16),
        scratch_shapes=scratch,
        compiler_params=_cparams(("parallel", "parallel", "arbitrary")),
        name="attention_lat" if lat_queries else "attention_ctx",
    )(qt, *kv_args)


def _outproj_kernel(x_ref, of_ref, ob_ref, z_ref, ybl_ref, ybc_ref, cb_ref, cc_ref, ch_ref,
                    ccp_ref, chp_ref, ccn_ref, chn_ref, mod_ref, w_ref, gg_ref, cw_ref,
                    o_ref, *, geom):
    j = pl.program_id(0)
    tm = x_ref.shape[0]
    first, last = _seq_flags(j, tm, geom)
    o = of_ref[...] + ob_ref[...]
    ms = _group_sum(o * o, _group_ones(GW)) * (1.0 / GDN_DV)
    ya = o * lax.rsqrt(ms + EPS) * gg_ref[0:1, :] * _silu(z_ref[...])
    yb = jnp.where(j * tm < geom["nl"], ybl_ref[...], ybc_ref[...])
    u = cc_ref[...] * ch_ref[...]
    hp = jnp.where(first, 0.0, ccp_ref[7:8, :] * chp_ref[7:8, :])
    hn = jnp.where(last, 0.0, ccn_ref[0:1, :] * chn_ref[0:1, :])
    yc = cb_ref[...] * _conv3(u, hp, hn, cw_ref)
    acc = _dot(ya.astype(BF16), w_ref[0:A_Z, :])
    acc += _dot(yb, w_ref[A_Z:A_Z + ATT_Q, :])
    acc += _dot(yc.astype(BF16), w_ref[A_Z + ATT_Q:, :])
    o_ref[...] = x_ref[...] + mod_ref[2:3, :] * acc


def _outproj(xs, of, ob, p, yb_lat, yb_ctx, mod, w, gg, conv_c, n_rows, geom):
    d = xs.shape[1]
    tm = geom["tm_cv"]
    nlt, tpb = geom["nl"] // tm, geom["l"] // tm
    nct = yb_ctx.shape[0] // tm
    hb = tm // 8
    nblk8 = xs.shape[0] // 8
    col = lambda w_, c: pl.BlockSpec((tm, w_), lambda j: (j, c // w_))
    prev = lambda c: pl.BlockSpec((8, SC_CH), lambda j: (jnp.maximum(j * hb - 1, 0), c // SC_CH))
    nxt = lambda c: pl.BlockSpec((8, SC_CH), lambda j: (jnp.minimum((j + 1) * hb, nblk8 - 1), c // SC_CH))
    gdn_o = pl.BlockSpec((tm, GW), lambda j: (_gdn_tile_perm(j, geom), 0))
    return pl.pallas_call(
        functools.partial(_outproj_kernel, geom=geom),
        grid=(n_rows // tm,),
        in_specs=[
            pl.BlockSpec((tm, d), lambda j: (j, 0)),
            gdn_o, gdn_o,
            col(A_Z, C_Z),
            pl.BlockSpec((tm, ATT_Q), lambda j: (jnp.minimum(j, nlt - 1), 0)),
            pl.BlockSpec((tm, ATT_Q), lambda j: (jnp.clip(j - nlt, 0, nct - 1), 0)),
            col(SC_CH, C_CB), col(SC_CH, C_CC), col(SC_CH, C_CH),
            prev(C_CC), prev(C_CH), nxt(C_CC), nxt(C_CH),
            pl.BlockSpec((None, 8, d), lambda j: (_mod_row(j, nlt, tpb), 0, 0)),
            pl.BlockSpec((d, d), lambda j: (0, 0)),
            pl.BlockSpec((8, GW), lambda j: (0, 0)),
            pl.BlockSpec((8, SC_CH), lambda j: (0, 0)),
        ],
        out_specs=pl.BlockSpec((tm, d), lambda j: (j, 0)),
        out_shape=jax.ShapeDtypeStruct((n_rows, d), F32),
        compiler_params=_cparams(("parallel",)),
        name="outproj",
    )(xs, of, ob, p, yb_lat, yb_ctx, p, p, p, p, p, p, p, mod, w, gg, conv_c)


def _experts_kernel(*refs, routed, final_norm):
    refs = list(refs)
    x_ref, g_ref, mod_ref = refs[0:3]
    del refs[0:3]
    r_ref = refs.pop(0) if routed else None
    gf_ref = refs.pop(0) if final_norm else None
    wgu_ref, wd_ref, o_ref, h_ref, acc_ref = refs[0:5]
    comb_ref = refs[5] if routed else None
    e = pl.program_id(1)
    ffe = wd_ref.shape[0]

    @pl.when(e == 0)
    def _():
        h = _norm_mod(x_ref[...], g_ref[...], mod_ref[3:4, :], mod_ref[4:5, :])
        h_ref[...] = h.astype(BF16)
        acc_ref[...] = jnp.zeros_like(acc_ref)
        if routed:
            logits = jnp.dot(h, r_ref[...], precision=lax.Precision.HIGHEST, preferred_element_type=F32)
            lane = lax.broadcasted_iota(jnp.int32, logits.shape, 1).astype(F32)
            valid = lane < N_EXPERTS
            logits = jnp.where(valid, logits, -jnp.inf)
            ex = jnp.exp(logits - jnp.max(logits, axis=-1, keepdims=True))
            probs = jnp.where(valid, ex / jnp.sum(ex, axis=-1, keepdims=True), -1.0)
            p1 = jnp.max(probs, axis=-1, keepdims=True)
            i1 = jnp.min(jnp.where(probs == p1, lane, float(LANE)), axis=-1, keepdims=True)
            rest = jnp.where(lane == i1, -1.0, probs)
            p2 = jnp.max(rest, axis=-1, keepdims=True)
            i2 = jnp.min(jnp.where(rest == p2, lane, float(LANE)), axis=-1, keepdims=True)
            den = p1 + p2
            comb_ref[...] = jnp.where(lane == i1, p1 / den, 0.0) + jnp.where(lane == i2, p2 / den, 0.0)

    gu = _dot(h_ref[...], wgu_ref[...])
    a = (_silu(gu[:, :ffe]) * gu[:, ffe:]).astype(BF16)
    y = _dot(a, wd_ref[...])
    if routed:
        comb = comb_ref[...]
        lane = lax.broadcasted_iota(jnp.int32, comb.shape, 1)
        y = jnp.sum(jnp.where(lane == e, comb, 0.0), axis=-1, keepdims=True) * y
    acc_ref[...] += y

    @pl.when(e == pl.num_programs(1) - 1)
    def _():
        y = x_ref[...] + mod_ref[5:6, :] * acc_ref[...]
        if final_norm:
            y = y * lax.rsqrt(jnp.mean(y * y, axis=-1, keepdims=True) + EPS) * gf_ref[...]
        o_ref[...] = y


def _experts(xs, g, mod, router, wgu, wd, n_rows, geom, final_g=None):
    d = xs.shape[1]
    ne, ffe = wd.shape[0], wd.shape[1]
    tm = geom["tm_mm"]
    nlt, tpb = geom["nl"] // tm, geom["l"] // tm
    routed = router is not None
    in_specs = [
        pl.BlockSpec((tm, d), lambda j, e: (j, 0)),
        pl.BlockSpec((1, d), lambda j, e: (0, 0)),
        pl.BlockSpec((None, 8, d), lambda j, e: (_mod_row(j, nlt, tpb), 0, 0)),
    ]
    scratch = [pltpu.VMEM((tm, d), BF16), pltpu.VMEM((tm, d), F32)]
    args = [xs, g, mod]
    if routed:
        in_specs.append(pl.BlockSpec((d, LANE), lambda j, e: (0, 0)))
        scratch.append(pltpu.VMEM((tm, LANE), F32))
        args.append(router)
    if final_g is not None:
        in_specs.append(pl.BlockSpec((1, d), lambda j, e: (0, 0)))
        args.append(final_g)
    in_specs += [
        pl.BlockSpec((None, d, 2 * ffe), lambda j, e: (e, 0, 0)),
        pl.BlockSpec((None, ffe, d), lambda j, e: (e, 0, 0)),
    ]
    return pl.pallas_call(
        functools.partial(_experts_kernel, routed=routed, final_norm=final_g is not None),
        grid=(n_rows // tm, ne),
        in_specs=in_specs,
        out_specs=pl.BlockSpec((tm, d), lambda j, e: (j, 0)),
        out_shape=jax.ShapeDtypeStruct((n_rows, d), F32),
        scratch_shapes=scratch,
        compiler_params=_cparams(("parallel", "arbitrary")),
        name="moe" if routed else "ffn",
    )(*args, wgu, wd)


def _rope_tables(l, tm):
    quarter = HEAD_DIM // 4
    inv_freq = ROPE_THETA ** (-jnp.arange(quarter, dtype=F32) / quarter)
    t = jnp.arange(l, dtype=jnp.int32)
    ang_r = (t // GRID_W).astype(F32)[:, None] * inv_freq[None, :]
    ang_c = (t % GRID_W).astype(F32)[:, None] * inv_freq[None, :]
    cos = jnp.concatenate([jnp.cos(ang_r)] * 2 + [jnp.cos(ang_c)] * 2, axis=1)
    sin = jnp.concatenate([-jnp.sin(ang_r), jnp.sin(ang_r), -jnp.sin(ang_c), jnp.sin(ang_c)], axis=1)
    cos = jnp.tile(cos, (1, LANE // HEAD_DIM))
    sin = jnp.tile(sin, (1, LANE // HEAD_DIM))
    cos = jnp.concatenate([jnp.ones((tm, LANE), F32), cos], axis=0)
    sin = jnp.concatenate([jnp.zeros((tm, LANE), F32), sin], axis=0)
    return cos, sin


def _pad_rows(a, rows=8):
    return jnp.pad(a, ((0, rows - a.shape[0]), (0, 0)))


def _permute_w_in(w):
    o = 0
    parts = {}
    for name, size in (("qkv", A_QKV), ("z", A_Z), ("a", 2 * GDN_HEADS), ("b", 2 * GDN_HEADS),
                       ("q", ATT_Q), ("k", ATT_KV), ("v", ATT_KV), ("cb", SC_CH), ("cc", SC_CH), ("ch", SC_CH)):
        parts[name] = w[:, o:o + size]
        o += size
    pad = jnp.zeros((w.shape[0], P_COLS - C_AB - 4 * GDN_HEADS), w.dtype)
    cols = [parts[k] for k in ("qkv", "z", "q", "k", "v", "cb", "cc", "ch", "a", "b")] + [pad]
    return jnp.concatenate(cols, axis=1).astype(BF16)


def _split_ffn(wg, wu, wd):
    d, ff = wg.shape
    ffe = ff // FFN_SPLIT
    wgu = jnp.concatenate([wg.reshape(d, FFN_SPLIT, ffe), wu.reshape(d, FFN_SPLIT, ffe)], axis=-1)
    return jnp.swapaxes(wgu, 0, 1).astype(BF16), wd.reshape(FFN_SPLIT, ffe, d).astype(BF16)


def kernel(x, c, ctx, c_ctx, w_mod, b_mod, norm1, norm2, w_in, conv_a, a_log, dt_bias, gdn_norm,
           q_norm, k_norm, conv_c, w_out, ffn_w_gate, ffn_w_up, ffn_w_down, router,
           moe_w_gate, moe_w_up, moe_w_down, norm_f):
    b, l, d = x.shape
    lc = ctx.shape[1]
    depth = w_mod.shape[0]
    nl, nc = b * l, b * lc
    tm_mm = math.gcd(TM_MM, math.gcd(l, nc))
    tm_cv = math.gcd(TM_CV, math.gcd(l, lc))
    tq = math.gcd(TQ, math.gcd(l, lc))
    gdn_rows = math.gcd(GDN_CHUNKS_PER_STEP * GDN_CHUNK, math.gcd(l, lc))
    gdn_nb = GDN_BATCH_PER_STEP if b % GDN_BATCH_PER_STEP == 0 else 1
    assert l % GRID_W == 0 and l % TK == 0 and lc % TK == 0 and nl % lc == 0 and gdn_rows == tm_cv
    geom = dict(b=b, l=l, lc=lc, nl=nl, nc=nc, tm_mm=tm_mm, tm_cv=tm_cv, tq=tq,
                gdn_rows=gdn_rows, gdn_nb=gdn_nb)
    n = nl + nc

    mod = _modulation(c, c_ctx, w_mod, b_mod)
    cos_t, sin_t = _rope_tables(l, tm_cv)
    xs = jnp.concatenate([x.reshape(nl, d), ctx.reshape(nc, d)], axis=0)

    for li in range(depth):
        last = li == depth - 1
        rows_out = nl if last else n
        w_in_p = _permute_w_in(w_in[li])
        gparams = _pad_rows(jnp.stack([jnp.pad(a_log[li].reshape(-1), (0, LANE - 2 * GDN_HEADS)),
                                       jnp.pad(dt_bias[li].reshape(-1), (0, LANE - 2 * GDN_HEADS))]))
        p = _inproj(xs, norm1[li][None, :], mod[li], w_in_p, geom)
        gqkv, gates, qt, kh, vt = _prep(
            p, cos_t, sin_t, _pad_rows(conv_a[li]), gparams,
            _pad_rows(jnp.tile(q_norm[li], ATT_HEADS)[None, :]),
            _pad_rows(jnp.tile(k_norm[li], ATT_KV_HEADS)[None, :]), geom)
        of, ob = _gdn(gqkv, gates, geom)
        yb_lat = _attention(qt, kh, vt, True, geom)
        yb_ctx = yb_lat if last else _attention(qt, kh, vt, False, geom)
        xs = _outproj(xs, of, ob, p, yb_lat, yb_ctx, mod[li], w_out[li].astype(BF16),
                      _pad_rows(jnp.tile(gdn_norm[li], GDN_HEADS)[None, :]), _pad_rows(conv_c[li]),
                      rows_out, geom)
        i = li // 2
        final_g = norm_f[None, :] if last else None
        if li % 2 == 0:
            wgu, wd = _split_ffn(ffn_w_gate[i], ffn_w_up[i], ffn_w_down[i])
            xs = _experts(xs, norm2[li][None, :], mod[li], None, wgu, wd, rows_out, geom, final_g)
        else:
            wgu = jnp.concatenate([moe_w_gate[i], moe_w_up[i]], axis=-1).astype(BF16)
            xs = _experts(xs, norm2[li][None, :], mod[li],
                          jnp.pad(router[i], ((0, 0), (0, LANE - N_EXPERTS))), wgu,
                          moe_w_down[i].astype(BF16), rows_out, geom, final_g)
    return xs.reshape(b, l, d)
```

```python
import functools
import math

import jax
import jax.numpy as jnp
from jax import lax
from jax.experimental import pallas as pl
from jax.experimental.pallas import tpu as pltpu

F32 = jnp.float32
BF16 = jnp.bfloat16
EPS = 1e-6

GRID_W = 64
HEAD_DIM = 64
GDN_HEADS = 4
GDN_DK = 64
GDN_DV = 64
GDN_CHUNK = 64
ATT_HEADS = 8
ATT_KV_HEADS = 2
ATT_GROUPS = ATT_HEADS // ATT_KV_HEADS
ROPE_THETA = 10000.0
SC_CH = 256
N_EXPERTS = 8
TOP_K = 2
A_QKV = GDN_HEADS * (2 * GDN_DK + GDN_DV)
A_Z = GDN_HEADS * GDN_DV
GW = GDN_HEADS * GDN_DK
ATT_Q = ATT_HEADS * HEAD_DIM
ATT_KV = ATT_KV_HEADS * HEAD_DIM

C_QKV = 0
C_Z = 768
C_Q = 1024
C_K = 1536
C_V = 1664
C_CB = 1792
C_CC = 2048
C_CH = 2304
C_AB = 2560
P_COLS = 2688

VMEM_LIMIT = 56 * 1024 * 1024
LANE = 128

TM_MM = 512
TM_CV = 256
TQ = 256
TK = 256
V_ROWS = HEAD_DIM + 16
ATT_CHUNKS_PER_ITER = 4
GDN_CHUNKS_PER_STEP = 4
GDN_BATCH_PER_STEP = 2
FFN_SPLIT = 2


def _cparams(sem):
    return pltpu.CompilerParams(dimension_semantics=sem, vmem_limit_bytes=VMEM_LIMIT)


def _sigmoid(x):
    return 1.0 / (1.0 + jnp.exp(-x))


def _silu(x):
    return x * _sigmoid(x)


def _dot(a, b):
    return jnp.dot(a, b, preferred_element_type=F32)


def _dot_split3(a, b):
    ah = a.astype(BF16)
    al = (a - ah.astype(F32)).astype(BF16)
    bh = b.astype(BF16)
    bl = (b - bh.astype(F32)).astype(BF16)
    return _dot(ah, bh) + _dot(al, bh) + _dot(ah, bl)


def _group_sum(x2, gmat):
    hi = x2.astype(BF16)
    lo = (x2 - hi.astype(F32)).astype(BF16)
    return _dot(hi, gmat) + _dot(lo, gmat)


def _group_ones(width):
    r = lax.broadcasted_iota(jnp.int32, (width, width), 0) // 64
    c = lax.broadcasted_iota(jnp.int32, (width, width), 1) // 64
    return jnp.where(r == c, 1.0, 0.0).astype(BF16)


def _mod_kernel(s_ref, w_ref, b_ref, o_ref):
    s = _silu(s_ref[...])
    o_ref[...] = jnp.dot(s, w_ref[...], precision=lax.Precision.HIGHEST,
                         preferred_element_type=F32) + b_ref[...]


def _modulation(c, c_ctx, w_mod, b_mod):
    depth, d, d6 = w_mod.shape
    rows = c.shape[0] + 1
    rpad = -(-rows // 8) * 8
    s = jnp.concatenate([c_ctx[None, :], c, jnp.zeros((rpad - rows, d), F32)], axis=0)
    tn = 1536
    out = pl.pallas_call(
        _mod_kernel,
        grid=(depth, d6 // tn),
        in_specs=[
            pl.BlockSpec((rpad, d), lambda l, j: (0, 0)),
            pl.BlockSpec((None, d, tn), lambda l, j: (l, 0, j)),
            pl.BlockSpec((None, 1, tn), lambda l, j: (l, 0, j)),
        ],
        out_specs=pl.BlockSpec((None, rpad, tn), lambda l, j: (l, 0, j)),
        out_shape=jax.ShapeDtypeStruct((depth, rpad, d6), F32),
        compiler_params=_cparams(("parallel", "parallel")),
        name="modulation",
    )(s, w_mod, b_mod.reshape(depth, 1, d6))
    m = out[:, :rows].reshape(depth, rows, 6, d)
    return jnp.pad(m, ((0, 0), (0, 0), (0, 2), (0, 0)))


def _norm_mod(x, g, shift, scale):
    ms = jnp.mean(x * x, axis=-1, keepdims=True)
    y = x * lax.rsqrt(ms + EPS) * g
    return y * (1.0 + scale) + shift


def _inproj_kernel(x_ref, g_ref, mod_ref, w_ref, o_ref):
    h = _norm_mod(x_ref[...], g_ref[...], mod_ref[0:1, :], mod_ref[1:2, :]).astype(BF16)
    ncol = o_ref.shape[1]
    for c0 in range(0, ncol, 256):
        c1 = min(c0 + 256, ncol)
        o_ref[:, c0:c1] = _dot(h, w_ref[:, c0:c1])


def _mod_row(j, n_lat_tiles, tiles_per_batch):
    return jnp.where(j < n_lat_tiles, 1 + j // tiles_per_batch, 0)


def _inproj(xs, g, mod, w, geom):
    n, d = xs.shape
    tm = geom["tm_mm"]
    nlt, tpb = geom["nl"] // tm, geom["l"] // tm
    return pl.pallas_call(
        _inproj_kernel,
        grid=(n // tm,),
        in_specs=[
            pl.BlockSpec((tm, d), lambda j: (j, 0)),
            pl.BlockSpec((1, d), lambda j: (0, 0)),
            pl.BlockSpec((None, 8, d), lambda j: (_mod_row(j, nlt, tpb), 0, 0)),
            pl.BlockSpec((d, P_COLS), lambda j: (0, 0)),
        ],
        out_specs=pl.BlockSpec((tm, P_COLS), lambda j: (j, 0)),
        out_shape=jax.ShapeDtypeStruct((n, P_COLS), F32),
        compiler_params=_cparams(("parallel",)),
        name="inproj",
    )(xs, g, mod, w)


def _seq_flags(j, tm, geom):
    row0 = j * tm
    is_lat = row0 < geom["nl"]
    pos = jnp.where(is_lat, row0 % geom["l"], (row0 - geom["nl"]) % geom["lc"])
    slen = jnp.where(is_lat, geom["l"], geom["lc"])
    return pos == 0, pos + tm == slen


def _conv3(u, prev_row, next_row, w_ref):
    tm = u.shape[0]
    rid = lax.broadcasted_iota(jnp.int32, u.shape, 0)
    up = jnp.where(rid == 0, prev_row, pltpu.roll(u, 1, 0))
    un = jnp.where(rid == tm - 1, next_row, pltpu.roll(u, tm - 1, 0))
    return w_ref[0:1, :] * up + w_ref[1:2, :] * u + w_ref[2:3, :] * un


def _softplus(x):
    return jnp.maximum(x, 0.0) + jnp.log(1.0 + jnp.exp(-jnp.abs(x)))


def _gdn_tile_perm(j, geom):
    cs, nb = geom["gdn_rows"], geom["gdn_nb"]
    nlt, tpb, tpc = geom["nl"] // cs, geom["l"] // cs, geom["lc"] // cs
    jc = j - nlt
    lat = ((j // tpb // nb) * tpb + j % tpb) * nb + (j // tpb) % nb
    ctx = nlt + ((jc // tpc // nb) * tpc + jc % tpc) * nb + (jc // tpc) % nb
    return jnp.where(j < nlt, lat, ctx)


def _prep_kernel(qkv_ref, qkvp_ref, qkvn_ref, ab_ref, q_ref, k_ref, v_ref, cos_ref, sin_ref,
                 cw_ref, gp_ref, qg_ref, kg_ref,
                 gqkv_ref, gate_ref, qt_ref, kh_ref, vt_ref, *, geom):
    j = pl.program_id(0)
    tm = qkv_ref.shape[0]
    first, last = _seq_flags(j, tm, geom)
    u = qkv_ref[...]
    hp = jnp.where(first, 0.0, qkvp_ref[7:8, :])
    hn = jnp.where(last, 0.0, qkvn_ref[0:1, :])
    s = _silu(_conv3(u, hp, hn, cw_ref))
    g256 = _group_ones(GW)
    qg = s[:, 0:GW]
    kg = s[:, GW:2 * GW]
    qg = qg * lax.rsqrt(_group_sum(qg * qg, g256) + EPS) * (GDN_DK ** -0.5)
    kg = kg * lax.rsqrt(_group_sum(kg * kg, g256) + EPS)
    gqkv_ref[:, 0:GW] = qg
    gqkv_ref[:, GW:2 * GW] = kg
    gqkv_ref[:, 2 * GW:3 * GW] = s[:, 2 * GW:3 * GW]
    ab = ab_ref[...]
    lane = lax.broadcasted_iota(jnp.int32, ab.shape, 1)
    log_a = -jnp.exp(gp_ref[0:1, :]) * _softplus(ab + gp_ref[1:2, :])
    gate_ref[...] = jnp.where(lane < 2 * GDN_HEADS, log_a, _sigmoid(ab))
    cos = cos_ref[...]
    sin = sin_ref[...]
    lane128 = lax.broadcasted_iota(jnp.int32, cos.shape, 1)
    first_half = (lane128 % 32) < 16

    def rope(xb):
        r_lo = pltpu.roll(xb, 16, 1)
        r_hi = pltpu.roll(xb, 112, 1)
        return xb * cos + jnp.where(first_half, r_hi, r_lo) * sin

    q = q_ref[...]
    q = q * lax.rsqrt(_group_sum(q * q, _group_ones(ATT_Q)) * (1.0 / HEAD_DIM) + EPS) * qg_ref[0:1, :]
    qr = jnp.concatenate([rope(q[:, c:c + LANE]) for c in range(0, ATT_Q, LANE)], axis=1)
    qt_ref[...] = (qr * (HEAD_DIM ** -0.5 * math.log2(math.e))).T.astype(BF16)
    k = k_ref[...]
    k = k * lax.rsqrt(_group_sum(k * k, _group_ones(ATT_KV)) * (1.0 / HEAD_DIM) + EPS) * kg_ref[0:1, :]
    kr = rope(k).astype(BF16)
    for h in range(ATT_KV_HEADS):
        kh_ref[h] = kr[:, h * HEAD_DIM:(h + 1) * HEAD_DIM]
    vt = v_ref[...].T
    pad_row = lax.broadcasted_iota(jnp.int32, (V_ROWS - HEAD_DIM, tm), 0)
    ones_rows = jnp.where(pad_row == 0, 1.0, 0.0)
    vt_ref[...] = jnp.concatenate(
        [part for h in range(ATT_KV_HEADS) for part in (vt[h * HEAD_DIM:(h + 1) * HEAD_DIM, :], ones_rows)],
        axis=0).astype(BF16)


def _prep(p, cos_t, sin_t, conv_a, gparams, qg, kg, geom):
    n = p.shape[0]
    tm = geom["tm_cv"]
    nlt, tpb = geom["nl"] // tm, geom["l"] // tm
    hb = tm // 8
    nblk8 = n // 8

    def rope_idx(j):
        return jnp.where(j < nlt, 1 + j % tpb, 0)

    col = lambda w, c: pl.BlockSpec((tm, w), lambda j: (j, c // w))
    return pl.pallas_call(
        functools.partial(_prep_kernel, geom=geom),
        grid=(n // tm,),
        in_specs=[
            col(A_QKV, C_QKV),
            pl.BlockSpec((8, A_QKV), lambda j: (jnp.maximum(j * hb - 1, 0), 0)),
            pl.BlockSpec((8, A_QKV), lambda j: (jnp.minimum((j + 1) * hb, nblk8 - 1), 0)),
            col(LANE, C_AB),
            col(ATT_Q, C_Q),
            col(ATT_KV, C_K),
            col(ATT_KV, C_V),
            pl.BlockSpec((tm, LANE), lambda j: (rope_idx(j), 0)),
            pl.BlockSpec((tm, LANE), lambda j: (rope_idx(j), 0)),
            pl.BlockSpec((8, A_QKV), lambda j: (0, 0)),
            pl.BlockSpec((8, LANE), lambda j: (0, 0)),
            pl.BlockSpec((8, ATT_Q), lambda j: (0, 0)),
            pl.BlockSpec((8, ATT_KV), lambda j: (0, 0)),
        ],
        out_specs=[
            pl.BlockSpec((tm, A_QKV), lambda j: (_gdn_tile_perm(j, geom), 0)),
            pl.BlockSpec((tm, LANE), lambda j: (_gdn_tile_perm(j, geom), 0)),
            pl.BlockSpec((ATT_Q, tm), lambda j: (0, j)),
            pl.BlockSpec((ATT_KV_HEADS, tm, HEAD_DIM), lambda j: (0, j, 0)),
            pl.BlockSpec((ATT_KV_HEADS * V_ROWS, tm), lambda j: (0, j)),
        ],
        out_shape=[
            jax.ShapeDtypeStruct((n, A_QKV), F32),
            jax.ShapeDtypeStruct((n, LANE), F32),
            jax.ShapeDtypeStruct((ATT_Q, n), BF16),
            jax.ShapeDtypeStruct((ATT_KV_HEADS, n, HEAD_DIM), BF16),
            jax.ShapeDtypeStruct((ATT_KV_HEADS * V_ROWS, n), BF16),
        ],
        compiler_params=_cparams(("parallel",)),
        name="mixer_prep",
    )(p, p, p, p, p, p, p, cos_t, sin_t, conv_a, gparams, qg, kg)


def _head_of_lane(shape):
    return lax.broadcasted_iota(jnp.int32, shape, 1) // 64


def _block_diag(xb):
    hl = _head_of_lane(xb.shape)
    zero = jnp.zeros_like(xb)
    return jnp.concatenate([jnp.where(hl == h, xb, zero) for h in range(GDN_HEADS)], axis=0)


def _dot_exact3(a_bf, x):
    h1 = x.astype(BF16)
    r1 = x - h1.astype(F32)
    h2 = r1.astype(BF16)
    h3 = (r1 - h2.astype(F32)).astype(BF16)
    return _dot(a_bf, h1) + _dot(a_bf, h2) + _dot(a_bf, h3)


def _gdn_local(insts):
    cs = GDN_CHUNK
    hl = _head_of_lane((cs, GW))
    row = lax.broadcasted_iota(jnp.int32, (cs, GW), 0)
    col = lax.broadcasted_iota(jnp.int32, (cs, GW), 1) % 64
    ti = lax.broadcasted_iota(jnp.int32, (cs, cs), 0)
    tj = lax.broadcasted_iota(jnp.int32, (cs, cs), 1)
    same16 = (row // 16) == (col // 16)
    same32 = (row // 32) == (col // 32)
    eye = jnp.where(row == col, 1.0, 0.0)
    nt = (((1,), (1,)), ((), ()))

    def widen(gates, base):
        out = jnp.zeros((cs, GW), F32)
        for h in range(GDN_HEADS):
            out = jnp.where(hl == h, gates[:, base + h:base + h + 1], out)
        return out

    for it in insts:
        rev = it["rev"]
        d = 1 if rev else 0
        it["la_w"] = widen(it["gates"], GDN_HEADS * d)
        it["beta_w"] = widen(it["gates"], 2 * GDN_HEADS + GDN_HEADS * d)
        it["tri"] = jnp.where((tj >= ti) if rev else (tj <= ti), 1.0, 0.0).astype(BF16)
        it["incl"] = (row <= col) if rev else (row >= col)
        it["strict"] = (row < col) if rev else (row > col)
        it["kb"] = it["k"].astype(BF16)
        it["kbd"] = _block_diag(it["kb"])
    for it in insts:
        kq = lax.dot_general(jnp.concatenate([it["kb"], it["q"].astype(BF16)], axis=0), it["kbd"], nt,
                             preferred_element_type=F32)
        it["kk_w"], it["qk_w"] = kq[0:cs, :], kq[cs:2 * cs, :]
    for it in insts:
        it["gcc"] = _dot_exact3(it["tri"], it["la_w"])
    for it in insts:
        rev, la_w, incl = it["rev"], it["la_w"], it["incl"]
        gcr = jnp.sum(jnp.where((row >= col) if rev else (row <= col), la_w, 0.0), axis=0, keepdims=True)
        it["decay"] = jnp.where(incl, jnp.exp(jnp.where(incl, it["gcc"] - gcr, 0.0)), 0.0)
        it["lower"] = jnp.where(it["strict"], it["beta_w"] * it["kk_w"] * it["decay"], 0.0)
        pw = jnp.where(same16, it["lower"], 0.0)
        it["t"] = eye - pw
        it["pwb"] = pw.astype(BF16)
    for it in insts:
        it["pwb"] = _dot(it["pwb"], _block_diag(it["pwb"])).astype(BF16)
    for _ in range(2):
        for it in insts:
            both = _dot(jnp.concatenate([it["pwb"], it["t"].astype(BF16)], axis=0), _block_diag(it["pwb"]))
            it["pwb"] = both[0:cs, :].astype(BF16)
            it["t"] = it["t"] + both[cs:2 * cs, :]
    for it in insts:
        it["t"] = it["t"] + _dot(it["t"].astype(BF16), _block_diag(it["pwb"]))
    for level in range(2):
        for it in insts:
            off = (jnp.where(same32 & jnp.logical_not(same16), it["lower"], 0.0) if level == 0
                   else jnp.where(same32, 0.0, it["lower"]))
            it["tb"] = it["t"].astype(BF16)
            it["tc"] = _dot(it["tb"], _block_diag(off.astype(BF16)))
        for it in insts:
            it["t"] = it["t"] - _dot(it["tc"].astype(BF16), _block_diag(it["tb"]))
    for it in insts:
        it["tb"] = it["t"].astype(BF16)
        it["egc"] = jnp.exp(it["gcc"])
        it["u"] = _dot(it["tb"], _block_diag((it["v"] * it["beta_w"]).astype(BF16)))
    for it in insts:
        it["w"] = _dot(it["tb"], _block_diag((it["k"] * (it["beta_w"] * it["egc"])).astype(BF16)))
    for it in insts:
        gcc = it["gcc"]
        gl = gcc[0:1, :] if it["rev"] else gcc[cs - 1:cs, :]
        it["qkm"] = jnp.where(it["incl"], it["qk_w"] * it["decay"], 0.0).astype(BF16)
        it["q_dec"] = (it["q"] * it["egc"]).astype(BF16)
        it["k_dec"] = (it["k"] * jnp.exp(gl - gcc)).astype(BF16)
        it["a_last"] = jnp.exp(gl)
        it["wb"] = it["w"].astype(BF16)


def _gdn_recurrence(group):
    r2 = lax.broadcasted_iota(jnp.int32, (GW, GW), 0) // 64
    c2 = lax.broadcasted_iota(jnp.int32, (GW, GW), 1) // 64
    tn = (((0,), (0,)), ((), ()))
    for it in group:
        it["s"] = it["s_ref"][...]
        it["sb"] = it["s"].astype(BF16)
    for it in group:
        it["wq_s"] = _dot(jnp.concatenate([it["wb"], it["q_dec"]], axis=0), it["sb"])
    for it in group:
        it["vb"] = (it["u"] - it["wq_s"][0:GDN_CHUNK, :]).astype(BF16)
    for it in group:
        it["o"] = it["wq_s"][GDN_CHUNK:2 * GDN_CHUNK, :] + _dot(it["qkm"], _block_diag(it["vb"]))
    for it in group:
        upd = lax.dot_general(it["k_dec"], it["vb"], tn, preferred_element_type=F32)
        it["s_ref"][...] = it["s"] * it["a_last"] + jnp.where(r2 == c2, upd, 0.0)
    for it in group:
        it["o_ref"][it["rows"], :] = it["o"]


def _gdn_kernel(xf_ref, gf_ref, xb_ref, gb_ref, of_ref, ob_ref, *s_refs, seq_rows):
    @pl.when(pl.program_id(1) == 0)
    def _():
        for s_ref in s_refs:
            s_ref[...] = jnp.zeros_like(s_ref)

    n_chunks = seq_rows // GDN_CHUNK
    scans = []
    for slot in range(xf_ref.shape[0] // seq_rows):
        for x_ref, g_ref, o_ref, rev in ((xf_ref, gf_ref, of_ref, False), (xb_ref, gb_ref, ob_ref, True)):
            seq = []
            for ci in (reversed(range(n_chunks)) if rev else range(n_chunks)):
                r0 = slot * seq_rows + ci * GDN_CHUNK
                rows = slice(r0, r0 + GDN_CHUNK)
                seq.append(dict(rev=rev, rows=rows, o_ref=o_ref, s_ref=s_refs[len(scans)], gates=g_ref[rows, :],
                                q=x_ref[rows, 0:GW], k=x_ref[rows, GW:2 * GW], v=x_ref[rows, 2 * GW:3 * GW]))
            scans.append(seq)
    _gdn_local([it for group in zip(*scans) for it in group])
    for group in zip(*scans):
        _gdn_recurrence(list(group))


def _gdn(gqkv, gates, geom):
    n = gqkv.shape[0]
    cs, nb = geom["gdn_rows"], geom["gdn_nb"]
    nl_c, nc_c = geom["l"] // cs, geom["lc"] // cs
    lat_blocks = geom["nl"] // (cs * nb)
    steps = nl_c + nc_c

    def fwd(g, s):
        return jnp.where(s < nc_c, lat_blocks + g * nc_c + s, g * nl_c + s - nc_c)

    def bwd(g, s):
        return jnp.where(s < nc_c, lat_blocks + g * nc_c + (nc_c - 1 - s), g * nl_c + (nl_c - 1 - (s - nc_c)))

    return pl.pallas_call(
        functools.partial(_gdn_kernel, seq_rows=cs),
        grid=(geom["b"] // nb, steps),
        in_specs=[
            pl.BlockSpec((nb * cs, A_QKV), lambda g, s: (fwd(g, s), 0)),
            pl.BlockSpec((nb * cs, LANE), lambda g, s: (fwd(g, s), 0)),
            pl.BlockSpec((nb * cs, A_QKV), lambda g, s: (bwd(g, s), 0)),
            pl.BlockSpec((nb * cs, LANE), lambda g, s: (bwd(g, s), 0)),
        ],
        out_specs=[
            pl.BlockSpec((nb * cs, GW), lambda g, s: (fwd(g, s), 0)),
            pl.BlockSpec((nb * cs, GW), lambda g, s: (bwd(g, s), 0)),
        ],
        out_shape=[jax.ShapeDtypeStruct((n, GW), F32), jax.ShapeDtypeStruct((n, GW), F32)],
        scratch_shapes=[pltpu.VMEM((GW, GW), F32)] * (2 * nb),
        compiler_params=_cparams(("parallel", "arbitrary")),
        name="gdn_scan",
    )(gqkv, gates, gqkv, gates)


def _attn_kernel(*refs, lat_queries):
    if lat_queries:
        qt_ref, kc_ref, kl_ref, vc_ref, vl_ref, o_ref, acc_ref, st_ref = refs
    else:
        qt_ref, kc_ref, vc_ref, o_ref, acc_ref = refs
    tq = qt_ref.shape[1]
    acc_ref[...] = jnp.zeros(acc_ref.shape, F32)
    qrows = [slice(g * HEAD_DIM, (g + 1) * HEAD_DIM) for g in range(ATT_GROUPS)]
    arows = [slice(g * V_ROWS, (g + 1) * V_ROWS) for g in range(ATT_GROUPS)]

    def scores(k_ref, off):
        kc = k_ref[pl.ds(off, TK), :]
        return [_dot(kc, qt_ref[r, :]) for r in qrows]

    def update(v_ref, off, sts, ms):
        vc = v_ref[:, pl.ds(off, TK)]
        new_ms, ps, alphas = [], [], []
        for st, m_old in zip(sts, ms):
            m_new = jnp.maximum(m_old, jnp.max(st, axis=0, keepdims=True))
            ps.append(jnp.exp2(st - m_new).astype(BF16))
            alphas.append(jnp.exp2(m_old - m_new))
            new_ms.append(m_new)
        pvs = [_dot(vc, p) for p in ps]
        for r, alpha, pv in zip(arows, alphas, pvs):
            acc_ref[r, :] = alpha * acc_ref[r, :] + pv
        return tuple(new_ms)

    ms = (jnp.full((1, tq), -jnp.inf, F32),) * ATT_GROUPS
    n_ctx = kc_ref.shape[0] // TK
    cur = scores(kc_ref, 0)
    for c in range(1, n_ctx):
        nxt = scores(kc_ref, c * TK)
        ms = update(vc_ref, (c - 1) * TK, cur, ms)
        cur = nxt
    if lat_queries:
        n_lat = kl_ref.shape[0] // TK
        nxt = scores(kl_ref, 0)
        ms = update(vc_ref, (n_ctx - 1) * TK, cur, ms)
        for g in range(ATT_GROUPS):
            st_ref[g] = nxt[g]
        per_iter = math.gcd(ATT_CHUNKS_PER_ITER, n_lat)

        def body(i, ms):
            cur = [st_ref[g] for g in range(ATT_GROUPS)]
            for c in range(per_iter):
                off = pl.multiple_of((i * per_iter + c) * TK, TK)
                off_next = pl.multiple_of(jnp.minimum(off + TK, (n_lat - 1) * TK), TK)
                nxt = scores(kl_ref, off_next)
                ms = update(vl_ref, off, cur, ms)
                cur = nxt
            for g in range(ATT_GROUPS):
                st_ref[g] = cur[g]
            return ms

        ms = lax.fori_loop(0, n_lat // per_iter, body, ms)
    else:
        ms = update(vc_ref, (n_ctx - 1) * TK, cur, ms)

    out = [acc_ref[r, :][0:HEAD_DIM, :] / acc_ref[r, :][HEAD_DIM:HEAD_DIM + 1, :] for r in arows]
    o_ref[...] = jnp.concatenate(out, axis=0).T.astype(o_ref.dtype)


def _attention(qt, kh, vt, lat_queries, geom):
    b, l, lc, nl = geom["b"], geom["l"], geom["lc"], geom["nl"]
    tq = geom["tq"]
    nq = (l if lat_queries else lc) // tq
    qbase = 0 if lat_queries else nl // tq
    gw = ATT_GROUPS * HEAD_DIM
    k_ctx = pl.BlockSpec((None, lc, HEAD_DIM), lambda bb, h, i: (h, nl // lc + bb, 0))
    v_ctx = pl.BlockSpec((V_ROWS, lc), lambda bb, h, i: (h, nl // lc + bb))
    k_lat = pl.BlockSpec((None, l, HEAD_DIM), lambda bb, h, i: (h, bb, 0))
    v_lat = pl.BlockSpec((V_ROWS, l), lambda bb, h, i: (h, bb))
    scratch = [pltpu.VMEM((ATT_GROUPS * V_ROWS, tq), F32)]
    if lat_queries:
        kv_specs, kv_args = [k_ctx, k_lat, v_ctx, v_lat], (kh, kh, vt, vt)
        scratch.append(pltpu.VMEM((ATT_GROUPS, TK, tq), F32))
    else:
        kv_specs, kv_args = [k_ctx, v_ctx], (kh, vt)
    return pl.pallas_call(
        functools.partial(_attn_kernel, lat_queries=lat_queries),
        grid=(b, ATT_KV_HEADS, nq),
        in_specs=[pl.BlockSpec((gw, tq), lambda bb, h, i: (h, qbase + bb * nq + i))] + kv_specs,
        out_specs=pl.BlockSpec((tq, gw), lambda bb, h, i: (bb * nq + i, h)),
        out_shape=jax.ShapeDtypeStruct((b * nq * tq, ATT_Q), BF16),
        scratch_shapes=scratch,
        compiler_params=_cparams(("parallel", "parallel", "arbitrary")),
        name="attention_lat" if lat_queries else "attention_ctx",
    )(qt, *kv_args)


def _outproj_kernel(x_ref, of_ref, ob_ref, z_ref, ybl_ref, ybc_ref, cb_ref, cc_ref, ch_ref,
                    ccp_ref, chp_ref, ccn_ref, chn_ref, mod_ref, w_ref, gg_ref, cw_ref,
                    o_ref, *, geom):
    j = pl.program_id(0)
    tm = x_ref.shape[0]
    first, last = _seq_flags(j, tm, geom)
    o = of_ref[...] + ob_ref[...]
    ms = _group_sum(o * o, _group_ones(GW)) * (1.0 / GDN_DV)
    ya = o * lax.rsqrt(ms + EPS) * gg_ref[0:1, :] * _silu(z_ref[...])
    yb = jnp.where(j * tm < geom["nl"], ybl_ref[...], ybc_ref[...])
    u = cc_ref[...] * ch_ref[...]
    hp = jnp.where(first, 0.0, ccp_ref[7:8, :] * chp_ref[7:8, :])
    hn = jnp.where(last, 0.0, ccn_ref[0:1, :] * chn_ref[0:1, :])
    yc = cb_ref[...] * _conv3(u, hp, hn, cw_ref)
    acc = _dot(ya.astype(BF16), w_ref[0:A_Z, :])
    acc += _dot(yb, w_ref[A_Z:A_Z + ATT_Q, :])
    acc += _dot(yc.astype(BF16), w_ref[A_Z + ATT_Q:, :])
    o_ref[...] = x_ref[...] + mod_ref[2:3, :] * acc


def _outproj(xs, of, ob, p, yb_lat, yb_ctx, mod, w, gg, conv_c, n_rows, geom):
    d = xs.shape[1]
    tm = geom["tm_cv"]
    nlt, tpb = geom["nl"] // tm, geom["l"] // tm
    nct = yb_ctx.shape[0] // tm
    hb = tm // 8
    nblk8 = xs.shape[0] // 8
    col = lambda w_, c: pl.BlockSpec((tm, w_), lambda j: (j, c // w_))
    prev = lambda c: pl.BlockSpec((8, SC_CH), lambda j: (jnp.maximum(j * hb - 1, 0), c // SC_CH))
    nxt = lambda c: pl.BlockSpec((8, SC_CH), lambda j: (jnp.minimum((j + 1) * hb, nblk8 - 1), c // SC_CH))
    gdn_o = pl.BlockSpec((tm, GW), lambda j: (_gdn_tile_perm(j, geom), 0))
    return pl.pallas_call(
        functools.partial(_outproj_kernel, geom=geom),
        grid=(n_rows // tm,),
        in_specs=[
            pl.BlockSpec((tm, d), lambda j: (j, 0)),
            gdn_o, gdn_o,
            col(A_Z, C_Z),
            pl.BlockSpec((tm, ATT_Q), lambda j: (jnp.minimum(j, nlt - 1), 0)),
            pl.BlockSpec((tm, ATT_Q), lambda j: (jnp.clip(j - nlt, 0, nct - 1), 0)),
            col(SC_CH, C_CB), col(SC_CH, C_CC), col(SC_CH, C_CH),
            prev(C_CC), prev(C_CH), nxt(C_CC), nxt(C_CH),
            pl.BlockSpec((None, 8, d), lambda j: (_mod_row(j, nlt, tpb), 0, 0)),
            pl.BlockSpec((d, d), lambda j: (0, 0)),
            pl.BlockSpec((8, GW), lambda j: (0, 0)),
            pl.BlockSpec((8, SC_CH), lambda j: (0, 0)),
        ],
        out_specs=pl.BlockSpec((tm, d), lambda j: (j, 0)),
        out_shape=jax.ShapeDtypeStruct((n_rows, d), F32),
        compiler_params=_cparams(("parallel",)),
        name="outproj",
    )(xs, of, ob, p, yb_lat, yb_ctx, p, p, p, p, p, p, p, mod, w, gg, conv_c)


def _experts_kernel(*refs, routed, final_norm):
    refs = list(refs)
    x_ref, g_ref, mod_ref = refs[0:3]
    del refs[0:3]
    r_ref = refs.pop(0) if routed else None
    gf_ref = refs.pop(0) if final_norm else None
    wgu_ref, wd_ref, o_ref, h_ref, acc_ref, a_ref = refs[0:6]
    comb_ref = refs[6] if routed else None
    e = pl.program_id(1)
    ne = pl.num_programs(1) - 1
    ffe = wd_ref.shape[0]

    def gate_up():
        gu = _dot(h_ref[...], wgu_ref[...])
        a_ref[...] = (_silu(gu[:, :ffe]) * gu[:, ffe:]).astype(BF16)

    def down(a_prev):
        y = _dot(a_prev, wd_ref[...])
        if routed:
            comb = comb_ref[...]
            lane = lax.broadcasted_iota(jnp.int32, comb.shape, 1)
            y = jnp.sum(jnp.where(lane == e - 1, comb, 0.0), axis=-1, keepdims=True) * y
        acc_ref[...] += y

    @pl.when(e == 0)
    def _():
        h = _norm_mod(x_ref[...], g_ref[...], mod_ref[3:4, :], mod_ref[4:5, :])
        h_ref[...] = h.astype(BF16)
        acc_ref[...] = jnp.zeros_like(acc_ref)
        if routed:
            logits = _dot_split3(h, r_ref[...])
            lane = lax.broadcasted_iota(jnp.int32, logits.shape, 1).astype(F32)
            valid = lane < N_EXPERTS
            logits = jnp.where(valid, logits, -jnp.inf)
            ex = jnp.exp(logits - jnp.max(logits, axis=-1, keepdims=True))
            probs = jnp.where(valid, ex / jnp.sum(ex, axis=-1, keepdims=True), -1.0)
            p1 = jnp.max(probs, axis=-1, keepdims=True)
            i1 = jnp.min(jnp.where(probs == p1, lane, float(LANE)), axis=-1, keepdims=True)
            rest = jnp.where(lane == i1, -1.0, probs)
            p2 = jnp.max(rest, axis=-1, keepdims=True)
            i2 = jnp.min(jnp.where(rest == p2, lane, float(LANE)), axis=-1, keepdims=True)
            den = p1 + p2
            comb_ref[...] = jnp.where(lane == i1, p1 / den, 0.0) + jnp.where(lane == i2, p2 / den, 0.0)
        gate_up()

    @pl.when(jnp.logical_and(e > 0, e < ne))
    def _():
        a_prev = a_ref[...]
        down(a_prev)
        gate_up()

    @pl.when(e == ne)
    def _():
        down(a_ref[...])
        y = x_ref[...] + mod_ref[5:6, :] * acc_ref[...]
        if final_norm:
            y = y * lax.rsqrt(jnp.mean(y * y, axis=-1, keepdims=True) + EPS) * gf_ref[...]
        o_ref[...] = y


def _experts(xs, g, mod, router, wgu, wd, n_rows, geom, final_g=None):
    d = xs.shape[1]
    ne, ffe = wd.shape[0], wd.shape[1]
    tm = geom["tm_mm"]
    nlt, tpb = geom["nl"] // tm, geom["l"] // tm
    routed = router is not None
    in_specs = [
        pl.BlockSpec((tm, d), lambda j, e: (j, 0)),
        pl.BlockSpec((1, d), lambda j, e: (0, 0)),
        pl.BlockSpec((None, 8, d), lambda j, e: (_mod_row(j, nlt, tpb), 0, 0)),
    ]
    scratch = [pltpu.VMEM((tm, d), BF16), pltpu.VMEM((tm, d), F32), pltpu.VMEM((tm, ffe), BF16)]
    args = [xs, g, mod]
    if routed:
        in_specs.append(pl.BlockSpec((d, LANE), lambda j, e: (0, 0)))
        scratch.append(pltpu.VMEM((tm, LANE), F32))
        args.append(router)
    if final_g is not None:
        in_specs.append(pl.BlockSpec((1, d), lambda j, e: (0, 0)))
        args.append(final_g)
    in_specs += [
        pl.BlockSpec((None, d, 2 * ffe), lambda j, e: (jnp.minimum(e, ne - 1), 0, 0)),
        pl.BlockSpec((None, ffe, d), lambda j, e: (jnp.maximum(e - 1, 0), 0, 0)),
    ]
    return pl.pallas_call(
        functools.partial(_experts_kernel, routed=routed, final_norm=final_g is not None),
        grid=(n_rows // tm, ne + 1),
        in_specs=in_specs,
        out_specs=pl.BlockSpec((tm, d), lambda j, e: (j, 0)),
        out_shape=jax.ShapeDtypeStruct((n_rows, d), F32),
        scratch_shapes=scratch,
        compiler_params=_cparams(("parallel", "arbitrary")),
        name="moe" if routed else "ffn",
    )(*args, wgu, wd)


def _rope_tables(l, tm):
    quarter = HEAD_DIM // 4
    inv_freq = ROPE_THETA ** (-jnp.arange(quarter, dtype=F32) / quarter)
    t = jnp.arange(l, dtype=jnp.int32)
    ang_r = (t // GRID_W).astype(F32)[:, None] * inv_freq[None, :]
    ang_c = (t % GRID_W).astype(F32)[:, None] * inv_freq[None, :]
    cos = jnp.concatenate([jnp.cos(ang_r)] * 2 + [jnp.cos(ang_c)] * 2, axis=1)
    sin = jnp.concatenate([-jnp.sin(ang_r), jnp.sin(ang_r), -jnp.sin(ang_c), jnp.sin(ang_c)], axis=1)
    cos = jnp.tile(cos, (1, LANE // HEAD_DIM))
    sin = jnp.tile(sin, (1, LANE // HEAD_DIM))
    cos = jnp.concatenate([jnp.ones((tm, LANE), F32), cos], axis=0)
    sin = jnp.concatenate([jnp.zeros((tm, LANE), F32), sin], axis=0)
    return cos, sin


def _pad_rows(a, rows=8):
    return jnp.pad(a, ((0, rows - a.shape[0]), (0, 0)))


def _permute_w_in(w):
    o = 0
    parts = {}
    for name, size in (("qkv", A_QKV), ("z", A_Z), ("a", 2 * GDN_HEADS), ("b", 2 * GDN_HEADS),
                       ("q", ATT_Q), ("k", ATT_KV), ("v", ATT_KV), ("cb", SC_CH), ("cc", SC_CH), ("ch", SC_CH)):
        parts[name] = w[:, o:o + size]
        o += size
    pad = jnp.zeros((w.shape[0], P_COLS - C_AB - 4 * GDN_HEADS), w.dtype)
    cols = [parts[k] for k in ("qkv", "z", "q", "k", "v", "cb", "cc", "ch", "a", "b")] + [pad]
    return jnp.concatenate(cols, axis=1).astype(BF16)


def _split_ffn(wg, wu, wd):
    d, ff = wg.shape
    ffe = ff // FFN_SPLIT
    wgu = jnp.concatenate([wg.reshape(d, FFN_SPLIT, ffe), wu.reshape(d, FFN_SPLIT, ffe)], axis=-1)
    return jnp.swapaxes(wgu, 0, 1).astype(BF16), wd.reshape(FFN_SPLIT, ffe, d).astype(BF16)


def kernel(x, c, ctx, c_ctx, w_mod, b_mod, norm1, norm2, w_in, conv_a, a_log, dt_bias, gdn_norm,
           q_norm, k_norm, conv_c, w_out, ffn_w_gate, ffn_w_up, ffn_w_down, router,
           moe_w_gate, moe_w_up, moe_w_down, norm_f):
    b, l, d = x.shape
    lc = ctx.shape[1]
    depth = w_mod.shape[0]
    nl, nc = b * l, b * lc
    tm_mm = math.gcd(TM_MM, math.gcd(l, nc))
    tm_cv = math.gcd(TM_CV, math.gcd(l, lc))
    tq = math.gcd(TQ, math.gcd(l, lc))
    gdn_rows = math.gcd(GDN_CHUNKS_PER_STEP * GDN_CHUNK, math.gcd(l, lc))
    gdn_nb = GDN_BATCH_PER_STEP if b % GDN_BATCH_PER_STEP == 0 else 1
    assert l % GRID_W == 0 and l % TK == 0 and lc % TK == 0 and nl % lc == 0 and gdn_rows == tm_cv
    geom = dict(b=b, l=l, lc=lc, nl=nl, nc=nc, tm_mm=tm_mm, tm_cv=tm_cv, tq=tq,
                gdn_rows=gdn_rows, gdn_nb=gdn_nb)
    n = nl + nc

    mod = _modulation(c, c_ctx, w_mod, b_mod)
    cos_t, sin_t = _rope_tables(l, tm_cv)
    xs = jnp.concatenate([x.reshape(nl, d), ctx.reshape(nc, d)], axis=0)

    for li in range(depth):
        last = li == depth - 1
        rows_out = nl if last else n
        w_in_p = _permute_w_in(w_in[li])
        gparams = _pad_rows(jnp.stack([jnp.pad(a_log[li].reshape(-1), (0, LANE - 2 * GDN_HEADS)),
                                       jnp.pad(dt_bias[li].reshape(-1), (0, LANE - 2 * GDN_HEADS))]))
        p = _inproj(xs, norm1[li][None, :], mod[li], w_in_p, geom)
        gqkv, gates, qt, kh, vt = _prep(
            p, cos_t, sin_t, _pad_rows(conv_a[li]), gparams,
            _pad_rows(jnp.tile(q_norm[li], ATT_HEADS)[None, :]),
            _pad_rows(jnp.tile(k_norm[li], ATT_KV_HEADS)[None, :]), geom)
        of, ob = _gdn(gqkv, gates, geom)
        yb_lat = _attention(qt, kh, vt, True, geom)
        yb_ctx = yb_lat if last else _attention(qt, kh, vt, False, geom)
        xs = _outproj(xs, of, ob, p, yb_lat, yb_ctx, mod[li], w_out[li].astype(BF16),
                      _pad_rows(jnp.tile(gdn_norm[li], GDN_HEADS)[None, :]), _pad_rows(conv_c[li]),
                      rows_out, geom)
        i = li // 2
        final_g = norm_f[None, :] if last else None
        if li % 2 == 0:
            wgu, wd = _split_ffn(ffn_w_gate[i], ffn_w_up[i], ffn_w_down[i])
            xs = _experts(xs, norm2[li][None, :], mod[li], None, wgu, wd, rows_out, geom, final_g)
        else:
            wgu = jnp.concatenate([moe_w_gate[i], moe_w_up[i]], axis=-1).astype(BF16)
            xs = _experts(xs, norm2[li][None, :], mod[li],
                          jnp.pad(router[i], ((0, 0), (0, LANE - N_EXPERTS))), wgu,
                          moe_w_down[i].astype(BF16), rows_out, geom, final_g)
    return xs.reshape(b, l, d)
```

```python
import functools
import math

import jax
import jax.numpy as jnp
from jax import lax
from jax.experimental import pallas as pl
from jax.experimental.pallas import tpu as pltpu

F32 = jnp.float32
BF16 = jnp.bfloat16
EPS = 1e-6

GRID_W = 64
HEAD_DIM = 64
GDN_HEADS = 4
GDN_DK = 64
GDN_DV = 64
GDN_CHUNK = 64
ATT_HEADS = 8
ATT_KV_HEADS = 2
ATT_GROUPS = ATT_HEADS // ATT_KV_HEADS
ROPE_THETA = 10000.0
SC_CH = 256
N_EXPERTS = 8
TOP_K = 2
A_QKV = GDN_HEADS * (2 * GDN_DK + GDN_DV)
A_Z = GDN_HEADS * GDN_DV
GW = GDN_HEADS * GDN_DK
ATT_Q = ATT_HEADS * HEAD_DIM
ATT_KV = ATT_KV_HEADS * HEAD_DIM

C_QKV = 0
C_Z = 768
C_Q = 1024
C_K = 1536
C_V = 1664
C_CB = 1792
C_CC = 2048
C_CH = 2304
C_AB = 2560
P_COLS = 2688

VMEM_LIMIT = 56 * 1024 * 1024
LANE = 128

TM_MM = 512
TM_CV = 256
TQ = 256
TK = 256
V_ROWS = HEAD_DIM + 16
ATT_CHUNKS_PER_ITER = 4
GDN_CHUNKS_PER_STEP = 4
GDN_BATCH_PER_STEP = 2
FFN_SPLIT = 2


def _cparams(sem):
    return pltpu.CompilerParams(dimension_semantics=sem, vmem_limit_bytes=VMEM_LIMIT)


def _sigmoid(x):
    return 1.0 / (1.0 + jnp.exp(-x))


def _silu(x):
    return x * _sigmoid(x)


def _dot(a, b):
    return jnp.dot(a, b, preferred_element_type=F32)


def _dot_split3(a, b):
    ah = a.astype(BF16)
    al = (a - ah.astype(F32)).astype(BF16)
    bh = b.astype(BF16)
    bl = (b - bh.astype(F32)).astype(BF16)
    return _dot(ah, bh) + _dot(al, bh) + _dot(ah, bl)


def _group_sum(x2, gmat):
    hi = x2.astype(BF16)
    lo = (x2 - hi.astype(F32)).astype(BF16)
    return _dot(hi, gmat) + _dot(lo, gmat)


def _group_ones(width):
    r = lax.broadcasted_iota(jnp.int32, (width, width), 0) // 64
    c = lax.broadcasted_iota(jnp.int32, (width, width), 1) // 64
    return jnp.where(r == c, 1.0, 0.0).astype(BF16)


def _mod_kernel(s_ref, w_ref, b_ref, o_ref):
    s = _silu(s_ref[...])
    o_ref[...] = jnp.dot(s, w_ref[...], precision=lax.Precision.HIGHEST,
                         preferred_element_type=F32) + b_ref[...]


def _modulation(c, c_ctx, w_mod, b_mod):
    depth, d, d6 = w_mod.shape
    rows = c.shape[0] + 1
    rpad = -(-rows // 8) * 8
    s = jnp.concatenate([c_ctx[None, :], c, jnp.zeros((rpad - rows, d), F32)], axis=0)
    tn = 1536
    out = pl.pallas_call(
        _mod_kernel,
        grid=(depth, d6 // tn),
        in_specs=[
            pl.BlockSpec((rpad, d), lambda l, j: (0, 0)),
            pl.BlockSpec((None, d, tn), lambda l, j: (l, 0, j)),
            pl.BlockSpec((None, 1, tn), lambda l, j: (l, 0, j)),
        ],
        out_specs=pl.BlockSpec((None, rpad, tn), lambda l, j: (l, 0, j)),
        out_shape=jax.ShapeDtypeStruct((depth, rpad, d6), F32),
        compiler_params=_cparams(("parallel", "parallel")),
        name="modulation",
    )(s, w_mod, b_mod.reshape(depth, 1, d6))
    m = out[:, :rows].reshape(depth, rows, 6, d)
    return jnp.pad(m, ((0, 0), (0, 0), (0, 2), (0, 0)))


def _norm_mod(x, g, shift, scale):
    ms = jnp.mean(x * x, axis=-1, keepdims=True)
    y = x * lax.rsqrt(ms + EPS) * g
    return y * (1.0 + scale) + shift


def _inproj_kernel(x_ref, g_ref, mod_ref, w_ref, o_ref):
    h = _norm_mod(x_ref[...], g_ref[...], mod_ref[0:1, :], mod_ref[1:2, :]).astype(BF16)
    ncol = o_ref.shape[1]
    for c0 in range(0, ncol, 256):
        c1 = min(c0 + 256, ncol)
        o_ref[:, c0:c1] = _dot(h, w_ref[:, c0:c1])


def _mod_row(j, n_lat_tiles, tiles_per_batch):
    return jnp.where(j < n_lat_tiles, 1 + j // tiles_per_batch, 0)


def _inproj(xs, g, mod, w, geom):
    n, d = xs.shape
    tm = geom["tm_mm"]
    nlt, tpb = geom["nl"] // tm, geom["l"] // tm
    return pl.pallas_call(
        _inproj_kernel,
        grid=(n // tm,),
        in_specs=[
            pl.BlockSpec((tm, d), lambda j: (j, 0)),
            pl.BlockSpec((1, d), lambda j: (0, 0)),
            pl.BlockSpec((None, 8, d), lambda j: (_mod_row(j, nlt, tpb), 0, 0)),
            pl.BlockSpec((d, P_COLS), lambda j: (0, 0)),
        ],
        out_specs=pl.BlockSpec((tm, P_COLS), lambda j: (j, 0)),
        out_shape=jax.ShapeDtypeStruct((n, P_COLS), F32),
        compiler_params=_cparams(("parallel",)),
        name="inproj",
    )(xs, g, mod, w)


def _seq_flags(j, tm, geom):
    row0 = j * tm
    is_lat = row0 < geom["nl"]
    pos = jnp.where(is_lat, row0 % geom["l"], (row0 - geom["nl"]) % geom["lc"])
    slen = jnp.where(is_lat, geom["l"], geom["lc"])
    return pos == 0, pos + tm == slen


def _conv3(u, prev_row, next_row, w_ref):
    tm = u.shape[0]
    rid = lax.broadcasted_iota(jnp.int32, u.shape, 0)
    up = jnp.where(rid == 0, prev_row, pltpu.roll(u, 1, 0))
    un = jnp.where(rid == tm - 1, next_row, pltpu.roll(u, tm - 1, 0))
    return w_ref[0:1, :] * up + w_ref[1:2, :] * u + w_ref[2:3, :] * un


def _softplus(x):
    return jnp.maximum(x, 0.0) + jnp.log(1.0 + jnp.exp(-jnp.abs(x)))


def _gdn_tile_perm(j, geom):
    cs, nb = geom["gdn_rows"], geom["gdn_nb"]
    nlt, tpb, tpc = geom["nl"] // cs, geom["l"] // cs, geom["lc"] // cs
    jc = j - nlt
    lat = ((j // tpb // nb) * tpb + j % tpb) * nb + (j // tpb) % nb
    ctx = nlt + ((jc // tpc // nb) * tpc + jc % tpc) * nb + (jc // tpc) % nb
    return jnp.where(j < nlt, lat, ctx)


def _prep_kernel(qkv_ref, qkvp_ref, qkvn_ref, ab_ref, q_ref, k_ref, v_ref, cos_ref, sin_ref,
                 cw_ref, gp_ref, qg_ref, kg_ref,
                 gqkv_ref, gate_ref, qt_ref, kh_ref, vt_ref, *, geom):
    j = pl.program_id(0)
    tm = qkv_ref.shape[0]
    first, last = _seq_flags(j, tm, geom)
    u = qkv_ref[...]
    hp = jnp.where(first, 0.0, qkvp_ref[7:8, :])
    hn = jnp.where(last, 0.0, qkvn_ref[0:1, :])
    s = _silu(_conv3(u, hp, hn, cw_ref))
    g256 = _group_ones(GW)
    qg = s[:, 0:GW]
    kg = s[:, GW:2 * GW]
    qg = qg * lax.rsqrt(_group_sum(qg * qg, g256) + EPS) * (GDN_DK ** -0.5)
    kg = kg * lax.rsqrt(_group_sum(kg * kg, g256) + EPS)
    gqkv_ref[:, 0:GW] = qg
    gqkv_ref[:, GW:2 * GW] = kg
    gqkv_ref[:, 2 * GW:3 * GW] = s[:, 2 * GW:3 * GW]
    ab = ab_ref[...]
    lane = lax.broadcasted_iota(jnp.int32, ab.shape, 1)
    log_a = -jnp.exp(gp_ref[0:1, :]) * _softplus(ab + gp_ref[1:2, :])
    gate_ref[...] = jnp.where(lane < 2 * GDN_HEADS, log_a, _sigmoid(ab))
    cos = cos_ref[...]
    sin = sin_ref[...]
    lane128 = lax.broadcasted_iota(jnp.int32, cos.shape, 1)
    first_half = (lane128 % 32) < 16

    def rope(xb):
        r_lo = pltpu.roll(xb, 16, 1)
        r_hi = pltpu.roll(xb, 112, 1)
        return xb * cos + jnp.where(first_half, r_hi, r_lo) * sin

    q = q_ref[...]
    q = q * lax.rsqrt(_group_sum(q * q, _group_ones(ATT_Q)) * (1.0 / HEAD_DIM) + EPS) * qg_ref[0:1, :]
    qr = jnp.concatenate([rope(q[:, c:c + LANE]) for c in range(0, ATT_Q, LANE)], axis=1)
    qt_ref[...] = (qr * (HEAD_DIM ** -0.5 * math.log2(math.e))).T.astype(BF16)
    k = k_ref[...]
    k = k * lax.rsqrt(_group_sum(k * k, _group_ones(ATT_KV)) * (1.0 / HEAD_DIM) + EPS) * kg_ref[0:1, :]
    kr = rope(k).astype(BF16)
    for h in range(ATT_KV_HEADS):
        kh_ref[h] = kr[:, h * HEAD_DIM:(h + 1) * HEAD_DIM]
    vt = v_ref[...].T
    pad_row = lax.broadcasted_iota(jnp.int32, (V_ROWS - HEAD_DIM, tm), 0)
    ones_rows = jnp.where(pad_row == 0, 1.0, 0.0)
    vt_ref[...] = jnp.concatenate(
        [part for h in range(ATT_KV_HEADS) for part in (vt[h * HEAD_DIM:(h + 1) * HEAD_DIM, :], ones_rows)],
        axis=0).astype(BF16)


def _prep(p, cos_t, sin_t, conv_a, gparams, qg, kg, geom):
    n = p.shape[0]
    tm = geom["tm_cv"]
    nlt, tpb = geom["nl"] // tm, geom["l"] // tm
    hb = tm // 8
    nblk8 = n // 8

    def rope_idx(j):
        return jnp.where(j < nlt, 1 + j % tpb, 0)

    col = lambda w, c: pl.BlockSpec((tm, w), lambda j: (j, c // w))
    return pl.pallas_call(
        functools.partial(_prep_kernel, geom=geom),
        grid=(n // tm,),
        in_specs=[
            col(A_QKV, C_QKV),
            pl.BlockSpec((8, A_QKV), lambda j: (jnp.maximum(j * hb - 1, 0), 0)),
            pl.BlockSpec((8, A_QKV), lambda j: (jnp.minimum((j + 1) * hb, nblk8 - 1), 0)),
            col(LANE, C_AB),
            col(ATT_Q, C_Q),
            col(ATT_KV, C_K),
            col(ATT_KV, C_V),
            pl.BlockSpec((tm, LANE), lambda j: (rope_idx(j), 0)),
            pl.BlockSpec((tm, LANE), lambda j: (rope_idx(j), 0)),
            pl.BlockSpec((8, A_QKV), lambda j: (0, 0)),
            pl.BlockSpec((8, LANE), lambda j: (0, 0)),
            pl.BlockSpec((8, ATT_Q), lambda j: (0, 0)),
            pl.BlockSpec((8, ATT_KV), lambda j: (0, 0)),
        ],
        out_specs=[
            pl.BlockSpec((tm, A_QKV), lambda j: (_gdn_tile_perm(j, geom), 0)),
            pl.BlockSpec((tm, LANE), lambda j: (_gdn_tile_perm(j, geom), 0)),
            pl.BlockSpec((ATT_Q, tm), lambda j: (0, j)),
            pl.BlockSpec((ATT_KV_HEADS, tm, HEAD_DIM), lambda j: (0, j, 0)),
            pl.BlockSpec((ATT_KV_HEADS * V_ROWS, tm), lambda j: (0, j)),
        ],
        out_shape=[
            jax.ShapeDtypeStruct((n, A_QKV), F32),
            jax.ShapeDtypeStruct((n, LANE), F32),
            jax.ShapeDtypeStruct((ATT_Q, n), BF16),
            jax.ShapeDtypeStruct((ATT_KV_HEADS, n, HEAD_DIM), BF16),
            jax.ShapeDtypeStruct((ATT_KV_HEADS * V_ROWS, n), BF16),
        ],
        compiler_params=_cparams(("parallel",)),
        name="mixer_prep",
    )(p, p, p, p, p, p, p, cos_t, sin_t, conv_a, gparams, qg, kg)


def _head_of_lane(shape):
    return lax.broadcasted_iota(jnp.int32, shape, 1) // 64


def _block_diag(xb):
    hl = _head_of_lane(xb.shape)
    zero = jnp.zeros_like(xb)
    return jnp.concatenate([jnp.where(hl == h, xb, zero) for h in range(GDN_HEADS)], axis=0)


def _dot_exact3(a_bf, x):
    h1 = x.astype(BF16)
    r1 = x - h1.astype(F32)
    h2 = r1.astype(BF16)
    h3 = (r1 - h2.astype(F32)).astype(BF16)
    return _dot(a_bf, h1) + _dot(a_bf, h2) + _dot(a_bf, h3)


def _gdn_local(insts):
    cs = GDN_CHUNK
    hl = _head_of_lane((cs, GW))
    row = lax.broadcasted_iota(jnp.int32, (cs, GW), 0)
    col = lax.broadcasted_iota(jnp.int32, (cs, GW), 1) % 64
    ti = lax.broadcasted_iota(jnp.int32, (cs, cs), 0)
    tj = lax.broadcasted_iota(jnp.int32, (cs, cs), 1)
    same16 = (row // 16) == (col // 16)
    same32 = (row // 32) == (col // 32)
    eye = jnp.where(row == col, 1.0, 0.0)
    nt = (((1,), (1,)), ((), ()))

    def widen(gates, base):
        out = jnp.zeros((cs, GW), F32)
        for h in range(GDN_HEADS):
            out = jnp.where(hl == h, gates[:, base + h:base + h + 1], out)
        return out

    for it in insts:
        rev = it["rev"]
        d = 1 if rev else 0
        it["la_w"] = widen(it["gates"], GDN_HEADS * d)
        it["beta_w"] = widen(it["gates"], 2 * GDN_HEADS + GDN_HEADS * d)
        it["tri"] = jnp.where((tj >= ti) if rev else (tj <= ti), 1.0, 0.0).astype(BF16)
        it["incl"] = (row <= col) if rev else (row >= col)
        it["strict"] = (row < col) if rev else (row > col)
        it["kb"] = it["k"].astype(BF16)
        it["kbd"] = _block_diag(it["kb"])
    for it in insts:
        kq = lax.dot_general(jnp.concatenate([it["kb"], it["q"].astype(BF16)], axis=0), it["kbd"], nt,
                             preferred_element_type=F32)
        it["kk_w"], it["qk_w"] = kq[0:cs, :], kq[cs:2 * cs, :]
    for it in insts:
        it["gcc"] = _dot_exact3(it["tri"], it["la_w"])
    for it in insts:
        rev, la_w, incl = it["rev"], it["la_w"], it["incl"]
        gcr = jnp.sum(jnp.where((row >= col) if rev else (row <= col), la_w, 0.0), axis=0, keepdims=True)
        it["decay"] = jnp.where(incl, jnp.exp(jnp.where(incl, it["gcc"] - gcr, 0.0)), 0.0)
        it["lower"] = jnp.where(it["strict"], it["beta_w"] * it["kk_w"] * it["decay"], 0.0)
        pw = jnp.where(same16, it["lower"], 0.0)
        it["t"] = eye - pw
        it["pwb"] = pw.astype(BF16)
    for it in insts:
        it["pwb"] = _dot(it["pwb"], _block_diag(it["pwb"])).astype(BF16)
    for _ in range(2):
        for it in insts:
            both = _dot(jnp.concatenate([it["pwb"], it["t"].astype(BF16)], axis=0), _block_diag(it["pwb"]))
            it["pwb"] = both[0:cs, :].astype(BF16)
            it["t"] = it["t"] + both[cs:2 * cs, :]
    for it in insts:
        it["t"] = it["t"] + _dot(it["t"].astype(BF16), _block_diag(it["pwb"]))
    for level in range(2):
        for it in insts:
            off = (jnp.where(same32 & jnp.logical_not(same16), it["lower"], 0.0) if level == 0
                   else jnp.where(same32, 0.0, it["lower"]))
            it["tb"] = it["t"].astype(BF16)
            it["tc"] = _dot(it["tb"], _block_diag(off.astype(BF16)))
        for it in insts:
            it["t"] = it["t"] - _dot(it["tc"].astype(BF16), _block_diag(it["tb"]))
    for it in insts:
        it["tb"] = it["t"].astype(BF16)
        it["egc"] = jnp.exp(it["gcc"])
        it["u"] = _dot(it["tb"], _block_diag((it["v"] * it["beta_w"]).astype(BF16)))
    for it in insts:
        it["w"] = _dot(it["tb"], _block_diag((it["k"] * (it["beta_w"] * it["egc"])).astype(BF16)))
    for it in insts:
        gcc = it["gcc"]
        gl = gcc[0:1, :] if it["rev"] else gcc[cs - 1:cs, :]
        it["qkm"] = jnp.where(it["incl"], it["qk_w"] * it["decay"], 0.0).astype(BF16)
        it["q_dec"] = (it["q"] * it["egc"]).astype(BF16)
        it["k_dec"] = (it["k"] * jnp.exp(gl - gcc)).astype(BF16)
        it["a_last"] = jnp.exp(gl)
        it["wb"] = it["w"].astype(BF16)


def _gdn_recurrence(group):
    r2 = lax.broadcasted_iota(jnp.int32, (GW, GW), 0) // 64
    c2 = lax.broadcasted_iota(jnp.int32, (GW, GW), 1) // 64
    tn = (((0,), (0,)), ((), ()))
    for it in group:
        it["s"] = it["s_ref"][...]
        it["sb"] = it["s"].astype(BF16)
    for it in group:
        it["wq_s"] = _dot(jnp.concatenate([it["wb"], it["q_dec"]], axis=0), it["sb"])
    for it in group:
        it["vb"] = (it["u"] - it["wq_s"][0:GDN_CHUNK, :]).astype(BF16)
    for it in group:
        it["o"] = it["wq_s"][GDN_CHUNK:2 * GDN_CHUNK, :] + _dot(it["qkm"], _block_diag(it["vb"]))
    for it in group:
        upd = lax.dot_general(it["k_dec"], it["vb"], tn, preferred_element_type=F32)
        it["s_ref"][...] = it["s"] * it["a_last"] + jnp.where(r2 == c2, upd, 0.0)
    for it in group:
        it["o_ref"][it["rows"], :] = it["o"]


def _gdn_kernel(xf_ref, gf_ref, xb_ref, gb_ref, of_ref, ob_ref, *s_refs, seq_rows):
    @pl.when(pl.program_id(1) == 0)
    def _():
        for s_ref in s_refs:
            s_ref[...] = jnp.zeros_like(s_ref)

    n_chunks = seq_rows // GDN_CHUNK
    scans = []
    for slot in range(xf_ref.shape[0] // seq_rows):
        for x_ref, g_ref, o_ref, rev in ((xf_ref, gf_ref, of_ref, False), (xb_ref, gb_ref, ob_ref, True)):
            seq = []
            for ci in (reversed(range(n_chunks)) if rev else range(n_chunks)):
                r0 = slot * seq_rows + ci * GDN_CHUNK
                rows = slice(r0, r0 + GDN_CHUNK)
                seq.append(dict(rev=rev, rows=rows, o_ref=o_ref, s_ref=s_refs[len(scans)], gates=g_ref[rows, :],
                                q=x_ref[rows, 0:GW], k=x_ref[rows, GW:2 * GW], v=x_ref[rows, 2 * GW:3 * GW]))
            scans.append(seq)
    _gdn_local([it for group in zip(*scans) for it in group])
    for group in zip(*scans):
        _gdn_recurrence(list(group))


def _gdn(gqkv, gates, geom):
    n = gqkv.shape[0]
    cs, nb = geom["gdn_rows"], geom["gdn_nb"]
    nl_c, nc_c = geom["l"] // cs, geom["lc"] // cs
    lat_blocks = geom["nl"] // (cs * nb)
    steps = nl_c + nc_c

    def fwd(g, s):
        return jnp.where(s < nc_c, lat_blocks + g * nc_c + s, g * nl_c + s - nc_c)

    def bwd(g, s):
        return jnp.where(s < nc_c, lat_blocks + g * nc_c + (nc_c - 1 - s), g * nl_c + (nl_c - 1 - (s - nc_c)))

    return pl.pallas_call(
        functools.partial(_gdn_kernel, seq_rows=cs),
        grid=(geom["b"] // nb, steps),
        in_specs=[
            pl.BlockSpec((nb * cs, A_QKV), lambda g, s: (fwd(g, s), 0)),
            pl.BlockSpec((nb * cs, LANE), lambda g, s: (fwd(g, s), 0)),
            pl.BlockSpec((nb * cs, A_QKV), lambda g, s: (bwd(g, s), 0)),
            pl.BlockSpec((nb * cs, LANE), lambda g, s: (bwd(g, s), 0)),
        ],
        out_specs=[
            pl.BlockSpec((nb * cs, GW), lambda g, s: (fwd(g, s), 0)),
            pl.BlockSpec((nb * cs, GW), lambda g, s: (bwd(g, s), 0)),
        ],
        out_shape=[jax.ShapeDtypeStruct((n, GW), F32), jax.ShapeDtypeStruct((n, GW), F32)],
        scratch_shapes=[pltpu.VMEM((GW, GW), F32)] * (2 * nb),
        compiler_params=_cparams(("parallel", "arbitrary")),
        name="gdn_scan",
    )(gqkv, gates, gqkv, gates)


def _attn_kernel(*refs, lat_queries):
    if lat_queries:
        qt_ref, kc_ref, kl_ref, vc_ref, vl_ref, o_ref, acc_ref, st_ref = refs
    else:
        qt_ref, kc_ref, vc_ref, o_ref, acc_ref = refs
    tq = qt_ref.shape[1]
    acc_ref[...] = jnp.zeros(acc_ref.shape, F32)
    qrows = [slice(g * HEAD_DIM, (g + 1) * HEAD_DIM) for g in range(ATT_GROUPS)]
    arows = [slice(g * V_ROWS, (g + 1) * V_ROWS) for g in range(ATT_GROUPS)]

    def scores(k_ref, off):
        kc = k_ref[pl.ds(off, TK), :]
        return [_dot(kc, qt_ref[r, :]) for r in qrows]

    def update(v_ref, off, sts, ms):
        vc = v_ref[:, pl.ds(off, TK)]
        new_ms, ps, alphas = [], [], []
        for st, m_old in zip(sts, ms):
            m_new = jnp.maximum(m_old, jnp.max(st, axis=0, keepdims=True))
            ps.append(jnp.exp2(st - m_new).astype(BF16))
            alphas.append(jnp.exp2(m_old - m_new))
            new_ms.append(m_new)
        pvs = [_dot(vc, p) for p in ps]
        for r, alpha, pv in zip(arows, alphas, pvs):
            acc_ref[r, :] = alpha * acc_ref[r, :] + pv
        return tuple(new_ms)

    ms = (jnp.full((1, tq), -jnp.inf, F32),) * ATT_GROUPS
    n_ctx = kc_ref.shape[0] // TK
    cur = scores(kc_ref, 0)
    for c in range(1, n_ctx):
        nxt = scores(kc_ref, c * TK)
        ms = update(vc_ref, (c - 1) * TK, cur, ms)
        cur = nxt
    if lat_queries:
        n_lat = kl_ref.shape[0] // TK
        nxt = scores(kl_ref, 0)
        ms = update(vc_ref, (n_ctx - 1) * TK, cur, ms)
        for g in range(ATT_GROUPS):
            st_ref[g] = nxt[g]
        per_iter = math.gcd(ATT_CHUNKS_PER_ITER, n_lat)

        def body(i, ms):
            cur = [st_ref[g] for g in range(ATT_GROUPS)]
            for c in range(per_iter):
                off = pl.multiple_of((i * per_iter + c) * TK, TK)
                off_next = pl.multiple_of(jnp.minimum(off + TK, (n_lat - 1) * TK), TK)
                nxt = scores(kl_ref, off_next)
                ms = update(vl_ref, off, cur, ms)
                cur = nxt
            for g in range(ATT_GROUPS):
                st_ref[g] = cur[g]
            return ms

        ms = lax.fori_loop(0, n_lat // per_iter, body, ms)
    else:
        ms = update(vc_ref, (n_ctx - 1) * TK, cur, ms)

    out = [acc_ref[r, :][0:HEAD_DIM, :] / acc_ref[r, :][HEAD_DIM:HEAD_DIM + 1, :] for r in arows]
    o_ref[...] = jnp.concatenate(out, axis=0).T.astype(o_ref.dtype)


def _attention(qt, kh, vt, lat_queries, geom):
    b, l, lc, nl = geom["b"], geom["l"], geom["lc"], geom["nl"]
    tq = geom["tq"]
    nq = (l if lat_queries else lc) // tq
    qbase = 0 if lat_queries else nl // tq
    gw = ATT_GROUPS * HEAD_DIM
    k_ctx = pl.BlockSpec((None, lc, HEAD_DIM), lambda bb, h, i: (h, nl // lc + bb, 0))
    v_ctx = pl.BlockSpec((V_ROWS, lc), lambda bb, h, i: (h, nl // lc + bb))
    k_lat = pl.BlockSpec((None, l, HEAD_DIM), lambda bb, h, i: (h, bb, 0))
    v_lat = pl.BlockSpec((V_ROWS, l), lambda bb, h, i: (h, bb))
    scratch = [pltpu.VMEM((ATT_GROUPS * V_ROWS, tq), F32)]
    if lat_queries:
        kv_specs, kv_args = [k_ctx, k_lat, v_ctx, v_lat], (kh, kh, vt, vt)
        scratch.append(pltpu.VMEM((ATT_GROUPS, TK, tq), F32))
    else:
        kv_specs, kv_args = [k_ctx, v_ctx], (kh, vt)
    return pl.pallas_call(
        functools.partial(_attn_kernel, lat_queries=lat_queries),
        grid=(b, ATT_KV_HEADS, nq),
        in_specs=[pl.BlockSpec((gw, tq), lambda bb, h, i: (h, qbase + bb * nq + i))] + kv_specs,
        out_specs=pl.BlockSpec((tq, gw), lambda bb, h, i: (bb * nq + i, h)),
        out_shape=jax.ShapeDtypeStruct((b * nq * tq, ATT_Q), BF16),
        scratch_shapes=scratch,
        compiler_params=_cparams(("parallel", "parallel", "arbitrary")),
        name="attention_lat" if lat_queries else "attention_ctx",
    )(qt, *kv_args)


def _outproj_kernel(x_ref, of_ref, ob_ref, z_ref, ybl_ref, ybc_ref, cb_ref, cc_ref, ch_ref,
                    ccp_ref, chp_ref, ccn_ref, chn_ref, mod_ref, w_ref, gg_ref, cw_ref,
                    o_ref, *, geom):
    j = pl.program_id(0)
    tm = x_ref.shape[0]
    first, last = _seq_flags(j, tm, geom)
    o = of_ref[...] + ob_ref[...]
    ms = _group_sum(o * o, _group_ones(GW)) * (1.0 / GDN_DV)
    ya = o * lax.rsqrt(ms + EPS) * gg_ref[0:1, :] * _silu(z_ref[...])
    yb = jnp.where(j * tm < geom["nl"], ybl_ref[...], ybc_ref[...])
    u = cc_ref[...] * ch_ref[...]
    hp = jnp.where(first, 0.0, ccp_ref[7:8, :] * chp_ref[7:8, :])
    hn = jnp.where(last, 0.0, ccn_ref[0:1, :] * chn_ref[0:1, :])
    yc = cb_ref[...] * _conv3(u, hp, hn, cw_ref)
    acc = _dot(ya.astype(BF16), w_ref[0:A_Z, :])
    acc += _dot(yb, w_ref[A_Z:A_Z + ATT_Q, :])
    acc += _dot(yc.astype(BF16), w_ref[A_Z + ATT_Q:, :])
    o_ref[...] = x_ref[...] + mod_ref[2:3, :] * acc


def _outproj(xs, of, ob, p, yb_lat, yb_ctx, mod, w, gg, conv_c, n_rows, geom):
    d = xs.shape[1]
    tm = geom["tm_cv"]
    nlt, tpb = geom["nl"] // tm, geom["l"] // tm
    nct = yb_ctx.shape[0] // tm
    hb = tm // 8
    nblk8 = xs.shape[0] // 8
    col = lambda w_, c: pl.BlockSpec((tm, w_), lambda j: (j, c // w_))
    prev = lambda c: pl.BlockSpec((8, SC_CH), lambda j: (jnp.maximum(j * hb - 1, 0), c // SC_CH))
    nxt = lambda c: pl.BlockSpec((8, SC_CH), lambda j: (jnp.minimum((j + 1) * hb, nblk8 - 1), c // SC_CH))
    gdn_o = pl.BlockSpec((tm, GW), lambda j: (_gdn_tile_perm(j, geom), 0))
    return pl.pallas_call(
        functools.partial(_outproj_kernel, geom=geom),
        grid=(n_rows // tm,),
        in_specs=[
            pl.BlockSpec((tm, d), lambda j: (j, 0)),
            gdn_o, gdn_o,
            col(A_Z, C_Z),
            pl.BlockSpec((tm, ATT_Q), lambda j: (jnp.minimum(j, nlt - 1), 0)),
            pl.BlockSpec((tm, ATT_Q), lambda j: (jnp.clip(j - nlt, 0, nct - 1), 0)),
            col(SC_CH, C_CB), col(SC_CH, C_CC), col(SC_CH, C_CH),
            prev(C_CC), prev(C_CH), nxt(C_CC), nxt(C_CH),
            pl.BlockSpec((None, 8, d), lambda j: (_mod_row(j, nlt, tpb), 0, 0)),
            pl.BlockSpec((d, d), lambda j: (0, 0)),
            pl.BlockSpec((8, GW), lambda j: (0, 0)),
            pl.BlockSpec((8, SC_CH), lambda j: (0, 0)),
        ],
        out_specs=pl.BlockSpec((tm, d), lambda j: (j, 0)),
        out_shape=jax.ShapeDtypeStruct((n_rows, d), F32),
        compiler_params=_cparams(("parallel",)),
        name="outproj",
    )(xs, of, ob, p, yb_lat, yb_ctx, p, p, p, p, p, p, p, mod, w, gg, conv_c)


def _expert_of_step(j, e, ne):
    return jnp.where(j % 2 == 0, e, ne - 1 - e)


def _experts_kernel(*refs, routed, final_norm):
    refs = list(refs)
    x_ref, g_ref, mod_ref = refs[0:3]
    del refs[0:3]
    r_ref = refs.pop(0) if routed else None
    gf_ref = refs.pop(0) if final_norm else None
    wgu_ref, wd_ref, o_ref, h_ref, acc_ref, a_ref = refs[0:6]
    comb_ref = refs[6] if routed else None
    e = pl.program_id(1)
    ne = pl.num_programs(1) - 1
    ffe = wd_ref.shape[0]

    def gate_up():
        gu = _dot(h_ref[...], wgu_ref[...])
        a_ref[...] = (_silu(gu[:, :ffe]) * gu[:, ffe:]).astype(BF16)

    def down(a_prev):
        y = _dot(a_prev, wd_ref[...])
        if routed:
            comb = comb_ref[...]
            lane = lax.broadcasted_iota(jnp.int32, comb.shape, 1)
            prev_expert = _expert_of_step(pl.program_id(0), e - 1, ne)
            y = jnp.sum(jnp.where(lane == prev_expert, comb, 0.0), axis=-1, keepdims=True) * y
        acc_ref[...] += y

    @pl.when(e == 0)
    def _():
        h = _norm_mod(x_ref[...], g_ref[...], mod_ref[3:4, :], mod_ref[4:5, :])
        h_ref[...] = h.astype(BF16)
        acc_ref[...] = jnp.zeros_like(acc_ref)
        if routed:
            logits = _dot_split3(h, r_ref[...])
            lane = lax.broadcasted_iota(jnp.int32, logits.shape, 1).astype(F32)
            valid = lane < N_EXPERTS
            logits = jnp.where(valid, logits, -jnp.inf)
            ex = jnp.exp(logits - jnp.max(logits, axis=-1, keepdims=True))
            probs = jnp.where(valid, ex / jnp.sum(ex, axis=-1, keepdims=True), -1.0)
            p1 = jnp.max(probs, axis=-1, keepdims=True)
            i1 = jnp.min(jnp.where(probs == p1, lane, float(LANE)), axis=-1, keepdims=True)
            rest = jnp.where(lane == i1, -1.0, probs)
            p2 = jnp.max(rest, axis=-1, keepdims=True)
            i2 = jnp.min(jnp.where(rest == p2, lane, float(LANE)), axis=-1, keepdims=True)
            den = p1 + p2
            comb_ref[...] = jnp.where(lane == i1, p1 / den, 0.0) + jnp.where(lane == i2, p2 / den, 0.0)
        gate_up()

    @pl.when(jnp.logical_and(e > 0, e < ne))
    def _():
        a_prev = a_ref[...]
        down(a_prev)
        gate_up()

    @pl.when(e == ne)
    def _():
        down(a_ref[...])
        y = x_ref[...] + mod_ref[5:6, :] * acc_ref[...]
        if final_norm:
            y = y * lax.rsqrt(jnp.mean(y * y, axis=-1, keepdims=True) + EPS) * gf_ref[...]
        o_ref[...] = y


def _experts(xs, g, mod, router, wgu, wd, n_rows, geom, final_g=None):
    d = xs.shape[1]
    ne, ffe = wd.shape[0], wd.shape[1]
    tm = geom["tm_mm"]
    nlt, tpb = geom["nl"] // tm, geom["l"] // tm
    routed = router is not None
    in_specs = [
        pl.BlockSpec((tm, d), lambda j, e: (j, 0)),
        pl.BlockSpec((1, d), lambda j, e: (0, 0)),
        pl.BlockSpec((None, 8, d), lambda j, e: (_mod_row(j, nlt, tpb), 0, 0)),
    ]
    scratch = [pltpu.VMEM((tm, d), BF16), pltpu.VMEM((tm, d), F32), pltpu.VMEM((tm, ffe), BF16)]
    args = [xs, g, mod]
    if routed:
        in_specs.append(pl.BlockSpec((d, LANE), lambda j, e: (0, 0)))
        scratch.append(pltpu.VMEM((tm, LANE), F32))
        args.append(router)
    if final_g is not None:
        in_specs.append(pl.BlockSpec((1, d), lambda j, e: (0, 0)))
        args.append(final_g)
    in_specs += [
        pl.BlockSpec((None, d, 2 * ffe), lambda j, e: (_expert_of_step(j, jnp.minimum(e, ne - 1), ne), 0, 0)),
        pl.BlockSpec((None, ffe, d), lambda j, e: (_expert_of_step(j, jnp.maximum(e - 1, 0), ne), 0, 0)),
    ]
    return pl.pallas_call(
        functools.partial(_experts_kernel, routed=routed, final_norm=final_g is not None),
        grid=(n_rows // tm, ne + 1),
        in_specs=in_specs,
        out_specs=pl.BlockSpec((tm, d), lambda j, e: (j, 0)),
        out_shape=jax.ShapeDtypeStruct((n_rows, d), F32),
        scratch_shapes=scratch,
        compiler_params=_cparams(("parallel", "arbitrary")),
        name="moe" if routed else "ffn",
    )(*args, wgu, wd)


def _rope_tables(l, tm):
    quarter = HEAD_DIM // 4
    inv_freq = ROPE_THETA ** (-jnp.arange(quarter, dtype=F32) / quarter)
    t = jnp.arange(l, dtype=jnp.int32)
    ang_r = (t // GRID_W).astype(F32)[:, None] * inv_freq[None, :]
    ang_c = (t % GRID_W).astype(F32)[:, None] * inv_freq[None, :]
    cos = jnp.concatenate([jnp.cos(ang_r)] * 2 + [jnp.cos(ang_c)] * 2, axis=1)
    sin = jnp.concatenate([-jnp.sin(ang_r), jnp.sin(ang_r), -jnp.sin(ang_c), jnp.sin(ang_c)], axis=1)
    cos = jnp.tile(cos, (1, LANE // HEAD_DIM))
    sin = jnp.tile(sin, (1, LANE // HEAD_DIM))
    cos = jnp.concatenate([jnp.ones((tm, LANE), F32), cos], axis=0)
    sin = jnp.concatenate([jnp.zeros((tm, LANE), F32), sin], axis=0)
    return cos, sin


def _pad_rows(a, rows=8):
    return jnp.pad(a, ((0, rows - a.shape[0]), (0, 0)))


def _permute_w_in(w):
    o = 0
    parts = {}
    for name, size in (("qkv", A_QKV), ("z", A_Z), ("a", 2 * GDN_HEADS), ("b", 2 * GDN_HEADS),
                       ("q", ATT_Q), ("k", ATT_KV), ("v", ATT_KV), ("cb", SC_CH), ("cc", SC_CH), ("ch", SC_CH)):
        parts[name] = w[:, o:o + size]
        o += size
    pad = jnp.zeros((w.shape[0], P_COLS - C_AB - 4 * GDN_HEADS), w.dtype)
    cols = [parts[k] for k in ("qkv", "z", "q", "k", "v", "cb", "cc", "ch", "a", "b")] + [pad]
    return jnp.concatenate(cols, axis=1).astype(BF16)


def _split_ffn(wg, wu, wd):
    d, ff = wg.shape
    ffe = ff // FFN_SPLIT
    wgu = jnp.concatenate([wg.reshape(d, FFN_SPLIT, ffe), wu.reshape(d, FFN_SPLIT, ffe)], axis=-1)
    return jnp.swapaxes(wgu, 0, 1).astype(BF16), wd.reshape(FFN_SPLIT, ffe, d).astype(BF16)


def kernel(x, c, ctx, c_ctx, w_mod, b_mod, norm1, norm2, w_in, conv_a, a_log, dt_bias, gdn_norm,
           q_norm, k_norm, conv_c, w_out, ffn_w_gate, ffn_w_up, ffn_w_down, router,
           moe_w_gate, moe_w_up, moe_w_down, norm_f):
    b, l, d = x.shape
    lc = ctx.shape[1]
    depth = w_mod.shape[0]
    nl, nc = b * l, b * lc
    tm_mm = math.gcd(TM_MM, math.gcd(l, nc))
    tm_cv = math.gcd(TM_CV, math.gcd(l, lc))
    tq = math.gcd(TQ, math.gcd(l, lc))
    gdn_rows = math.gcd(GDN_CHUNKS_PER_STEP * GDN_CHUNK, math.gcd(l, lc))
    gdn_nb = GDN_BATCH_PER_STEP if b % GDN_BATCH_PER_STEP == 0 else 1
    assert l % GRID_W == 0 and l % TK == 0 and lc % TK == 0 and nl % lc == 0 and gdn_rows == tm_cv
    geom = dict(b=b, l=l, lc=lc, nl=nl, nc=nc, tm_mm=tm_mm, tm_cv=tm_cv, tq=tq,
                gdn_rows=gdn_rows, gdn_nb=gdn_nb)
    n = nl + nc

    mod = _modulation(c, c_ctx, w_mod, b_mod)
    cos_t, sin_t = _rope_tables(l, tm_cv)
    xs = jnp.concatenate([x.reshape(nl, d), ctx.reshape(nc, d)], axis=0)

    for li in range(depth):
        last = li == depth - 1
        rows_out = nl if last else n
        w_in_p = _permute_w_in(w_in[li])
        gparams = _pad_rows(jnp.stack([jnp.pad(a_log[li].reshape(-1), (0, LANE - 2 * GDN_HEADS)),
                                       jnp.pad(dt_bias[li].reshape(-1), (0, LANE - 2 * GDN_HEADS))]))
        p = _inproj(xs, norm1[li][None, :], mod[li], w_in_p, geom)
        gqkv, gates, qt, kh, vt = _prep(
            p, cos_t, sin_t, _pad_rows(conv_a[li]), gparams,
            _pad_rows(jnp.tile(q_norm[li], ATT_HEADS)[None, :]),
            _pad_rows(jnp.tile(k_norm[li], ATT_KV_HEADS)[None, :]), geom)
        of, ob = _gdn(gqkv, gates, geom)
        yb_lat = _attention(qt, kh, vt, True, geom)
        yb_ctx = yb_lat if last else _attention(qt, kh, vt, False, geom)
        xs = _outproj(xs, of, ob, p, yb_lat, yb_ctx, mod[li], w_out[li].astype(BF16),
                      _pad_rows(jnp.tile(gdn_norm[li], GDN_HEADS)[None, :]), _pad_rows(conv_c[li]),
                      rows_out, geom)
        i = li // 2
        final_g = norm_f[None, :] if last else None
        if li % 2 == 0:
            wgu, wd = _split_ffn(ffn_w_gate[i], ffn_w_up[i], ffn_w_down[i])
            xs = _experts(xs, norm2[li][None, :], mod[li], None, wgu, wd, rows_out, geom, final_g)
        else:
            wgu = jnp.concatenate([moe_w_gate[i], moe_w_up[i]], axis=-1).astype(BF16)
            xs = _experts(xs, norm2[li][None, :], mod[li],
                          jnp.pad(router[i], ((0, 0), (0, LANE - N_EXPERTS))), wgu,
                          moe_w_down[i].astype(BF16), rows_out, geom, final_g)
    return xs.reshape(b, l, d)
```

```python
import functools
import math

import jax
import jax.numpy as jnp
from jax import lax
from jax.experimental import pallas as pl
from jax.experimental.pallas import tpu as pltpu

F32 = jnp.float32
BF16 = jnp.bfloat16
EPS = 1e-6

GRID_W = 64
HEAD_DIM = 64
GDN_HEADS = 4
GDN_DK = 64
GDN_DV = 64
GDN_CHUNK = 64
ATT_HEADS = 8
ATT_KV_HEADS = 2
ATT_GROUPS = ATT_HEADS // ATT_KV_HEADS
ROPE_THETA = 10000.0
SC_CH = 256
N_EXPERTS = 8
TOP_K = 2
A_QKV = GDN_HEADS * (2 * GDN_DK + GDN_DV)
A_Z = GDN_HEADS * GDN_DV
GW = GDN_HEADS * GDN_DK
ATT_Q = ATT_HEADS * HEAD_DIM
ATT_KV = ATT_KV_HEADS * HEAD_DIM

C_QKV = 0
C_Z = 768
C_Q = 1024
C_K = 1536
C_V = 1664
C_CB = 1792
C_CC = 2048
C_CH = 2304
C_AB = 2560
P_COLS = 2688

VMEM_LIMIT = 56 * 1024 * 1024
LANE = 128

TM_MM = 512
TM_CV = 256
TQ = 256
TQ_LAT = 512
TK = 256
V_ROWS = HEAD_DIM + 16
ATT_CHUNKS_PER_ITER = 4
GDN_CHUNKS_PER_STEP = 4
GDN_BATCH_PER_STEP = 4
FFN_SPLIT = 2


def _cparams(sem):
    return pltpu.CompilerParams(dimension_semantics=sem, vmem_limit_bytes=VMEM_LIMIT)


def _sigmoid(x):
    return 1.0 / (1.0 + jnp.exp(-x))


def _silu(x):
    return x * _sigmoid(x)


def _dot(a, b):
    return jnp.dot(a, b, preferred_element_type=F32)


def _dot_split3(a, b):
    ah = a.astype(BF16)
    al = (a - ah.astype(F32)).astype(BF16)
    bh = b.astype(BF16)
    bl = (b - bh.astype(F32)).astype(BF16)
    return _dot(ah, bh) + _dot(al, bh) + _dot(ah, bl)


def _group_sum(x2, gmat):
    hi = x2.astype(BF16)
    lo = (x2 - hi.astype(F32)).astype(BF16)
    return _dot(hi, gmat) + _dot(lo, gmat)


def _group_ones(width):
    r = lax.broadcasted_iota(jnp.int32, (width, width), 0) // 64
    c = lax.broadcasted_iota(jnp.int32, (width, width), 1) // 64
    return jnp.where(r == c, 1.0, 0.0).astype(BF16)


def _mod_kernel(s_ref, w_ref, b_ref, o_ref):
    s = _silu(s_ref[...])
    o_ref[...] = jnp.dot(s, w_ref[...], precision=lax.Precision.HIGHEST,
                         preferred_element_type=F32) + b_ref[...]


def _modulation(c, c_ctx, w_mod, b_mod):
    depth, d, d6 = w_mod.shape
    rows = c.shape[0] + 1
    rpad = -(-rows // 8) * 8
    s = jnp.concatenate([c_ctx[None, :], c, jnp.zeros((rpad - rows, d), F32)], axis=0)
    tn = 1536
    out = pl.pallas_call(
        _mod_kernel,
        grid=(depth, d6 // tn),
        in_specs=[
            pl.BlockSpec((rpad, d), lambda l, j: (0, 0)),
            pl.BlockSpec((None, d, tn), lambda l, j: (l, 0, j)),
            pl.BlockSpec((None, 1, tn), lambda l, j: (l, 0, j)),
        ],
        out_specs=pl.BlockSpec((None, rpad, tn), lambda l, j: (l, 0, j)),
        out_shape=jax.ShapeDtypeStruct((depth, rpad, d6), F32),
        compiler_params=_cparams(("parallel", "parallel")),
        name="modulation",
    )(s, w_mod, b_mod.reshape(depth, 1, d6))
    m = out[:, :rows].reshape(depth, rows, 6, d)
    return jnp.pad(m, ((0, 0), (0, 0), (0, 2), (0, 0)))


def _norm_mod(x, g, shift, scale):
    ms = jnp.mean(x * x, axis=-1, keepdims=True)
    y = x * lax.rsqrt(ms + EPS) * g
    return y * (1.0 + scale) + shift


def _inproj_kernel(x_ref, g_ref, mod_ref, w_ref, o_ref):
    h = _norm_mod(x_ref[...], g_ref[...], mod_ref[0:1, :], mod_ref[1:2, :]).astype(BF16)
    ncol = o_ref.shape[1]
    for c0 in range(0, ncol, 256):
        c1 = min(c0 + 256, ncol)
        o_ref[:, c0:c1] = _dot(h, w_ref[:, c0:c1])


def _mod_row(j, n_lat_tiles, tiles_per_batch):
    return jnp.where(j < n_lat_tiles, 1 + j // tiles_per_batch, 0)


def _inproj(xs, g, mod, w, geom):
    n, d = xs.shape
    tm = geom["tm_mm"]
    nlt, tpb = geom["nl"] // tm, geom["l"] // tm
    return pl.pallas_call(
        _inproj_kernel,
        grid=(n // tm,),
        in_specs=[
            pl.BlockSpec((tm, d), lambda j: (j, 0)),
            pl.BlockSpec((1, d), lambda j: (0, 0)),
            pl.BlockSpec((None, 8, d), lambda j: (_mod_row(j, nlt, tpb), 0, 0)),
            pl.BlockSpec((d, P_COLS), lambda j: (0, 0)),
        ],
        out_specs=pl.BlockSpec((tm, P_COLS), lambda j: (j, 0)),
        out_shape=jax.ShapeDtypeStruct((n, P_COLS), F32),
        compiler_params=_cparams(("parallel",)),
        name="inproj",
    )(xs, g, mod, w)


def _seq_flags(j, tm, geom):
    row0 = j * tm
    is_lat = row0 < geom["nl"]
    pos = jnp.where(is_lat, row0 % geom["l"], (row0 - geom["nl"]) % geom["lc"])
    slen = jnp.where(is_lat, geom["l"], geom["lc"])
    return pos == 0, pos + tm == slen


def _conv3(u, prev_row, next_row, w_ref):
    tm = u.shape[0]
    rid = lax.broadcasted_iota(jnp.int32, u.shape, 0)
    up = jnp.where(rid == 0, prev_row, pltpu.roll(u, 1, 0))
    un = jnp.where(rid == tm - 1, next_row, pltpu.roll(u, tm - 1, 0))
    return w_ref[0:1, :] * up + w_ref[1:2, :] * u + w_ref[2:3, :] * un


def _softplus(x):
    return jnp.maximum(x, 0.0) + jnp.log(1.0 + jnp.exp(-jnp.abs(x)))


def _gdn_tile_perm(j, geom):
    cs, nb = geom["gdn_rows"], geom["gdn_nb"]
    nlt, tpb, tpc = geom["nl"] // cs, geom["l"] // cs, geom["lc"] // cs
    jc = j - nlt
    lat = ((j // tpb // nb) * tpb + j % tpb) * nb + (j // tpb) % nb
    ctx = nlt + ((jc // tpc // nb) * tpc + jc % tpc) * nb + (jc // tpc) % nb
    return jnp.where(j < nlt, lat, ctx)


def _prep_kernel(qkv_ref, qkvp_ref, qkvn_ref, ab_ref, q_ref, k_ref, v_ref, cos_ref, sin_ref,
                 cw_ref, gp_ref, qg_ref, kg_ref,
                 gqkv_ref, gate_ref, qt_ref, kh_ref, vt_ref, *, geom):
    j = pl.program_id(0)
    tm = qkv_ref.shape[0]
    first, last = _seq_flags(j, tm, geom)
    u = qkv_ref[...]
    hp = jnp.where(first, 0.0, qkvp_ref[7:8, :])
    hn = jnp.where(last, 0.0, qkvn_ref[0:1, :])
    s = _silu(_conv3(u, hp, hn, cw_ref))
    g256 = _group_ones(GW)
    qg = s[:, 0:GW]
    kg = s[:, GW:2 * GW]
    qg = qg * lax.rsqrt(_group_sum(qg * qg, g256) + EPS) * (GDN_DK ** -0.5)
    kg = kg * lax.rsqrt(_group_sum(kg * kg, g256) + EPS)
    gqkv_ref[:, 0:GW] = qg
    gqkv_ref[:, GW:2 * GW] = kg
    gqkv_ref[:, 2 * GW:3 * GW] = s[:, 2 * GW:3 * GW]
    ab = ab_ref[...]
    lane = lax.broadcasted_iota(jnp.int32, ab.shape, 1)
    log_a = -jnp.exp(gp_ref[0:1, :]) * _softplus(ab + gp_ref[1:2, :])
    gate_ref[...] = jnp.where(lane < 2 * GDN_HEADS, log_a, _sigmoid(ab))
    cos = cos_ref[...]
    sin = sin_ref[...]
    lane128 = lax.broadcasted_iota(jnp.int32, cos.shape, 1)
    first_half = (lane128 % 32) < 16

    def rope(xb):
        r_lo = pltpu.roll(xb, 16, 1)
        r_hi = pltpu.roll(xb, 112, 1)
        return xb * cos + jnp.where(first_half, r_hi, r_lo) * sin

    q = q_ref[...]
    q = q * lax.rsqrt(_group_sum(q * q, _group_ones(ATT_Q)) * (1.0 / HEAD_DIM) + EPS) * qg_ref[0:1, :]
    qr = jnp.concatenate([rope(q[:, c:c + LANE]) for c in range(0, ATT_Q, LANE)], axis=1)
    qt_ref[...] = (qr * (HEAD_DIM ** -0.5 * math.log2(math.e))).T.astype(BF16)
    k = k_ref[...]
    k = k * lax.rsqrt(_group_sum(k * k, _group_ones(ATT_KV)) * (1.0 / HEAD_DIM) + EPS) * kg_ref[0:1, :]
    kr = rope(k).astype(BF16)
    for h in range(ATT_KV_HEADS):
        kh_ref[h] = kr[:, h * HEAD_DIM:(h + 1) * HEAD_DIM]
    vt = v_ref[...].T
    pad_row = lax.broadcasted_iota(jnp.int32, (V_ROWS - HEAD_DIM, tm), 0)
    ones_rows = jnp.where(pad_row == 0, 1.0, 0.0)
    vt_ref[...] = jnp.concatenate(
        [part for h in range(ATT_KV_HEADS) for part in (vt[h * HEAD_DIM:(h + 1) * HEAD_DIM, :], ones_rows)],
        axis=0).astype(BF16)


def _prep(p, cos_t, sin_t, conv_a, gparams, qg, kg, geom):
    n = p.shape[0]
    tm = geom["tm_cv"]
    nlt, tpb = geom["nl"] // tm, geom["l"] // tm
    hb = tm // 8
    nblk8 = n // 8

    def rope_idx(j):
        return jnp.where(j < nlt, 1 + j % tpb, 0)

    col = lambda w, c: pl.BlockSpec((tm, w), lambda j: (j, c // w))
    return pl.pallas_call(
        functools.partial(_prep_kernel, geom=geom),
        grid=(n // tm,),
        in_specs=[
            col(A_QKV, C_QKV),
            pl.BlockSpec((8, A_QKV), lambda j: (jnp.maximum(j * hb - 1, 0), 0)),
            pl.BlockSpec((8, A_QKV), lambda j: (jnp.minimum((j + 1) * hb, nblk8 - 1), 0)),
            col(LANE, C_AB),
            col(ATT_Q, C_Q),
            col(ATT_KV, C_K),
            col(ATT_KV, C_V),
            pl.BlockSpec((tm, LANE), lambda j: (rope_idx(j), 0)),
            pl.BlockSpec((tm, LANE), lambda j: (rope_idx(j), 0)),
            pl.BlockSpec((8, A_QKV), lambda j: (0, 0)),
            pl.BlockSpec((8, LANE), lambda j: (0, 0)),
            pl.BlockSpec((8, ATT_Q), lambda j: (0, 0)),
            pl.BlockSpec((8, ATT_KV), lambda j: (0, 0)),
        ],
        out_specs=[
            pl.BlockSpec((tm, A_QKV), lambda j: (_gdn_tile_perm(j, geom), 0)),
            pl.BlockSpec((tm, LANE), lambda j: (_gdn_tile_perm(j, geom), 0)),
            pl.BlockSpec((ATT_Q, tm), lambda j: (0, j)),
            pl.BlockSpec((ATT_KV_HEADS, tm, HEAD_DIM), lambda j: (0, j, 0)),
            pl.BlockSpec((ATT_KV_HEADS * V_ROWS, tm), lambda j: (0, j)),
        ],
        out_shape=[
            jax.ShapeDtypeStruct((n, A_QKV), F32),
            jax.ShapeDtypeStruct((n, LANE), F32),
            jax.ShapeDtypeStruct((ATT_Q, n), BF16),
            jax.ShapeDtypeStruct((ATT_KV_HEADS, n, HEAD_DIM), BF16),
            jax.ShapeDtypeStruct((ATT_KV_HEADS * V_ROWS, n), BF16),
        ],
        compiler_params=_cparams(("parallel",)),
        name="mixer_prep",
    )(p, p, p, p, p, p, p, cos_t, sin_t, conv_a, gparams, qg, kg)


def _head_of_lane(shape):
    return lax.broadcasted_iota(jnp.int32, shape, 1) // 64


def _block_diag(xb):
    hl = _head_of_lane(xb.shape)
    zero = jnp.zeros_like(xb)
    return jnp.concatenate([jnp.where(hl == h, xb, zero) for h in range(GDN_HEADS)], axis=0)


def _dot_exact3(a_bf, x):
    h1 = x.astype(BF16)
    r1 = x - h1.astype(F32)
    h2 = r1.astype(BF16)
    h3 = (r1 - h2.astype(F32)).astype(BF16)
    return _dot(a_bf, h1) + _dot(a_bf, h2) + _dot(a_bf, h3)


def _gdn_local(insts):
    cs = GDN_CHUNK
    hl = _head_of_lane((cs, GW))
    row = lax.broadcasted_iota(jnp.int32, (cs, GW), 0)
    col = lax.broadcasted_iota(jnp.int32, (cs, GW), 1) % 64
    ti = lax.broadcasted_iota(jnp.int32, (cs, cs), 0)
    tj = lax.broadcasted_iota(jnp.int32, (cs, cs), 1)
    same16 = (row // 16) == (col // 16)
    same32 = (row // 32) == (col // 32)
    eye = jnp.where(row == col, 1.0, 0.0)
    nt = (((1,), (1,)), ((), ()))

    def widen(gates, base):
        out = jnp.zeros((cs, GW), F32)
        for h in range(GDN_HEADS):
            out = jnp.where(hl == h, gates[:, base + h:base + h + 1], out)
        return out

    for it in insts:
        rev = it["rev"]
        d = 1 if rev else 0
        it["la_w"] = widen(it["gates"], GDN_HEADS * d)
        it["beta_w"] = widen(it["gates"], 2 * GDN_HEADS + GDN_HEADS * d)
        it["tri"] = jnp.where((tj >= ti) if rev else (tj <= ti), 1.0, 0.0).astype(BF16)
        it["incl"] = (row <= col) if rev else (row >= col)
        it["strict"] = (row < col) if rev else (row > col)
        it["kb"] = it["k"].astype(BF16)
        it["kbd"] = _block_diag(it["kb"])
    for it in insts:
        kq = lax.dot_general(jnp.concatenate([it["kb"], it["q"].astype(BF16)], axis=0), it["kbd"], nt,
                             preferred_element_type=F32)
        it["kk_w"], it["qk_w"] = kq[0:cs, :], kq[cs:2 * cs, :]
    for it in insts:
        it["gcc"] = _dot_exact3(it["tri"], it["la_w"])
    for it in insts:
        rev, la_w, incl = it["rev"], it["la_w"], it["incl"]
        gcr = jnp.sum(jnp.where((row >= col) if rev else (row <= col), la_w, 0.0), axis=0, keepdims=True)
        it["decay"] = jnp.where(incl, jnp.exp(jnp.where(incl, it["gcc"] - gcr, 0.0)), 0.0)
        it["lower"] = jnp.where(it["strict"], it["beta_w"] * it["kk_w"] * it["decay"], 0.0)
        pw = jnp.where(same16, it["lower"], 0.0)
        it["t"] = eye - pw
        it["pwb"] = pw.astype(BF16)
    for it in insts:
        it["pwb"] = _dot(it["pwb"], _block_diag(it["pwb"])).astype(BF16)
    for _ in range(2):
        for it in insts:
            both = _dot(jnp.concatenate([it["pwb"], it["t"].astype(BF16)], axis=0), _block_diag(it["pwb"]))
            it["pwb"] = both[0:cs, :].astype(BF16)
            it["t"] = it["t"] + both[cs:2 * cs, :]
    for it in insts:
        it["t"] = it["t"] + _dot(it["t"].astype(BF16), _block_diag(it["pwb"]))
    for level in range(2):
        for it in insts:
            off = (jnp.where(same32 & jnp.logical_not(same16), it["lower"], 0.0) if level == 0
                   else jnp.where(same32, 0.0, it["lower"]))
            it["tb"] = it["t"].astype(BF16)
            it["tc"] = _dot(it["tb"], _block_diag(off.astype(BF16)))
        for it in insts:
            it["t"] = it["t"] - _dot(it["tc"].astype(BF16), _block_diag(it["tb"]))
    for it in insts:
        it["tb"] = it["t"].astype(BF16)
        it["egc"] = jnp.exp(it["gcc"])
        it["u"] = _dot(it["tb"], _block_diag((it["v"] * it["beta_w"]).astype(BF16)))
    for it in insts:
        it["w"] = _dot(it["tb"], _block_diag((it["k"] * (it["beta_w"] * it["egc"])).astype(BF16)))
    for it in insts:
        gcc = it["gcc"]
        gl = gcc[0:1, :] if it["rev"] else gcc[cs - 1:cs, :]
        it["qkm"] = jnp.where(it["incl"], it["qk_w"] * it["decay"], 0.0).astype(BF16)
        it["q_dec"] = (it["q"] * it["egc"]).astype(BF16)
        it["k_dec"] = (it["k"] * jnp.exp(gl - gcc)).astype(BF16)
        it["a_last"] = jnp.exp(gl)
        it["wb"] = it["w"].astype(BF16)


def _gdn_recurrence(group):
    r2 = lax.broadcasted_iota(jnp.int32, (GW, GW), 0) // 64
    c2 = lax.broadcasted_iota(jnp.int32, (GW, GW), 1) // 64
    tn = (((0,), (0,)), ((), ()))
    for it in group:
        it["s"] = it["s_ref"][...]
        it["sb"] = it["s"].astype(BF16)
    for it in group:
        it["wq_s"] = _dot(jnp.concatenate([it["wb"], it["q_dec"]], axis=0), it["sb"])
    for it in group:
        it["vb"] = (it["u"] - it["wq_s"][0:GDN_CHUNK, :]).astype(BF16)
    for it in group:
        it["o"] = it["wq_s"][GDN_CHUNK:2 * GDN_CHUNK, :] + _dot(it["qkm"], _block_diag(it["vb"]))
    for it in group:
        upd = lax.dot_general(it["k_dec"], it["vb"], tn, preferred_element_type=F32)
        it["s_ref"][...] = it["s"] * it["a_last"] + jnp.where(r2 == c2, upd, 0.0)
    for it in group:
        it["o_ref"][it["rows"], :] = it["o"]


def _gdn_kernel(xf_ref, gf_ref, xb_ref, gb_ref, of_ref, ob_ref, *s_refs, seq_rows):
    @pl.when(pl.program_id(1) == 0)
    def _():
        for s_ref in s_refs:
            s_ref[...] = jnp.zeros_like(s_ref)

    n_chunks = seq_rows // GDN_CHUNK
    scans = []
    for slot in range(xf_ref.shape[0] // seq_rows):
        for x_ref, g_ref, o_ref, rev in ((xf_ref, gf_ref, of_ref, False), (xb_ref, gb_ref, ob_ref, True)):
            seq = []
            for ci in (reversed(range(n_chunks)) if rev else range(n_chunks)):
                r0 = slot * seq_rows + ci * GDN_CHUNK
                rows = slice(r0, r0 + GDN_CHUNK)
                seq.append(dict(rev=rev, rows=rows, o_ref=o_ref, s_ref=s_refs[len(scans)], gates=g_ref[rows, :],
                                q=x_ref[rows, 0:GW], k=x_ref[rows, GW:2 * GW], v=x_ref[rows, 2 * GW:3 * GW]))
            scans.append(seq)
    _gdn_local([it for group in zip(*scans) for it in group])
    for group in zip(*scans):
        _gdn_recurrence(list(group))


def _gdn(gqkv, gates, geom):
    n = gqkv.shape[0]
    cs, nb = geom["gdn_rows"], geom["gdn_nb"]
    nl_c, nc_c = geom["l"] // cs, geom["lc"] // cs
    lat_blocks = geom["nl"] // (cs * nb)
    steps = nl_c + nc_c

    def fwd(g, s):
        return jnp.where(s < nc_c, lat_blocks + g * nc_c + s, g * nl_c + s - nc_c)

    def bwd(g, s):
        return jnp.where(s < nc_c, lat_blocks + g * nc_c + (nc_c - 1 - s), g * nl_c + (nl_c - 1 - (s - nc_c)))

    return pl.pallas_call(
        functools.partial(_gdn_kernel, seq_rows=cs),
        grid=(geom["b"] // nb, steps),
        in_specs=[
            pl.BlockSpec((nb * cs, A_QKV), lambda g, s: (fwd(g, s), 0)),
            pl.BlockSpec((nb * cs, LANE), lambda g, s: (fwd(g, s), 0)),
            pl.BlockSpec((nb * cs, A_QKV), lambda g, s: (bwd(g, s), 0)),
            pl.BlockSpec((nb * cs, LANE), lambda g, s: (bwd(g, s), 0)),
        ],
        out_specs=[
            pl.BlockSpec((nb * cs, GW), lambda g, s: (fwd(g, s), 0)),
            pl.BlockSpec((nb * cs, GW), lambda g, s: (bwd(g, s), 0)),
        ],
        out_shape=[jax.ShapeDtypeStruct((n, GW), F32), jax.ShapeDtypeStruct((n, GW), F32)],
        scratch_shapes=[pltpu.VMEM((GW, GW), F32)] * (2 * nb),
        compiler_params=_cparams(("parallel", "arbitrary")),
        name="gdn_scan",
    )(gqkv, gates, gqkv, gates)


def _attn_kernel(*refs, lat_queries):
    if lat_queries:
        qt_ref, kc_ref, kl_ref, vc_ref, vl_ref, o_ref, acc_ref, st_ref = refs
    else:
        qt_ref, kc_ref, vc_ref, o_ref, acc_ref = refs
    tq = qt_ref.shape[1]
    acc_ref[...] = jnp.zeros(acc_ref.shape, F32)
    qrows = [slice(g * HEAD_DIM, (g + 1) * HEAD_DIM) for g in range(ATT_GROUPS)]
    arows = [slice(g * V_ROWS, (g + 1) * V_ROWS) for g in range(ATT_GROUPS)]

    def scores(k_ref, off):
        kc = k_ref[pl.ds(off, TK), :]
        return [_dot(kc, qt_ref[r, :]) for r in qrows]

    def update(v_ref, off, sts, ms):
        vc = v_ref[:, pl.ds(off, TK)]
        new_ms, ps, alphas = [], [], []
        for st, m_old in zip(sts, ms):
            m_new = jnp.maximum(m_old, jnp.max(st, axis=0, keepdims=True))
            ps.append(jnp.exp2(st - m_new).astype(BF16))
            alphas.append(jnp.exp2(m_old - m_new))
            new_ms.append(m_new)
        pvs = [_dot(vc, p) for p in ps]
        for r, alpha, pv in zip(arows, alphas, pvs):
            acc_ref[r, :] = alpha * acc_ref[r, :] + pv
        return tuple(new_ms)

    ms = (jnp.full((1, tq), -jnp.inf, F32),) * ATT_GROUPS
    n_ctx = kc_ref.shape[0] // TK
    cur = scores(kc_ref, 0)
    for c in range(1, n_ctx):
        nxt = scores(kc_ref, c * TK)
        ms = update(vc_ref, (c - 1) * TK, cur, ms)
        cur = nxt
    if lat_queries:
        n_lat = kl_ref.shape[0] // TK
        nxt = scores(kl_ref, 0)
        ms = update(vc_ref, (n_ctx - 1) * TK, cur, ms)
        for g in range(ATT_GROUPS):
            st_ref[g] = nxt[g]
        per_iter = math.gcd(ATT_CHUNKS_PER_ITER, n_lat)

        def body(i, ms):
            cur = [st_ref[g] for g in range(ATT_GROUPS)]
            for c in range(per_iter):
                off = pl.multiple_of((i * per_iter + c) * TK, TK)
                off_next = pl.multiple_of(jnp.minimum(off + TK, (n_lat - 1) * TK), TK)
                nxt = scores(kl_ref, off_next)
                ms = update(vl_ref, off, cur, ms)
                cur = nxt
            for g in range(ATT_GROUPS):
                st_ref[g] = cur[g]
            return ms

        ms = lax.fori_loop(0, n_lat // per_iter, body, ms)
    else:
        ms = update(vc_ref, (n_ctx - 1) * TK, cur, ms)

    out = [acc_ref[r, :][0:HEAD_DIM, :] / acc_ref[r, :][HEAD_DIM:HEAD_DIM + 1, :] for r in arows]
    o_ref[...] = jnp.concatenate(out, axis=0).T.astype(o_ref.dtype)


def _attention(qt, kh, vt, lat_queries, geom):
    b, l, lc, nl = geom["b"], geom["l"], geom["lc"], geom["nl"]
    tq = math.gcd(TQ_LAT, l) if lat_queries else geom["tq"]
    nq = (l if lat_queries else lc) // tq
    qbase = 0 if lat_queries else nl // tq
    gw = ATT_GROUPS * HEAD_DIM
    k_ctx = pl.BlockSpec((None, lc, HEAD_DIM), lambda bb, h, i: (h, nl // lc + bb, 0))
    v_ctx = pl.BlockSpec((V_ROWS, lc), lambda bb, h, i: (h, nl // lc + bb))
    k_lat = pl.BlockSpec((None, l, HEAD_DIM), lambda bb, h, i: (h, bb, 0))
    v_lat = pl.BlockSpec((V_ROWS, l), lambda bb, h, i: (h, bb))
    scratch = [pltpu.VMEM((ATT_GROUPS * V_ROWS, tq), F32)]
    if lat_queries:
        kv_specs, kv_args = [k_ctx, k_lat, v_ctx, v_lat], (kh, kh, vt, vt)
        scratch.append(pltpu.VMEM((ATT_GROUPS, TK, tq), F32))
    else:
        kv_specs, kv_args = [k_ctx, v_ctx], (kh, vt)
    return pl.pallas_call(
        functools.partial(_attn_kernel, lat_queries=lat_queries),
        grid=(b, ATT_KV_HEADS, nq),
        in_specs=[pl.BlockSpec((gw, tq), lambda bb, h, i: (h, qbase + bb * nq + i))] + kv_specs,
        out_specs=pl.BlockSpec((tq, gw), lambda bb, h, i: (bb * nq + i, h)),
        out_shape=jax.ShapeDtypeStruct((b * nq * tq, ATT_Q), BF16),
        scratch_shapes=scratch,
        compiler_params=_cparams(("parallel", "parallel", "arbitrary")),
        name="attention_lat" if lat_queries else "attention_ctx",
    )(qt, *kv_args)


def _outproj_kernel(x_ref, of_ref, ob_ref, z_ref, ybl_ref, ybc_ref, cb_ref, cc_ref, ch_ref,
                    ccp_ref, chp_ref, ccn_ref, chn_ref, mod_ref, w_ref, gg_ref, cw_ref,
                    o_ref, *, geom):
    j = pl.program_id(0)
    tm = x_ref.shape[0]
    first, last = _seq_flags(j, tm, geom)
    o = of_ref[...] + ob_ref[...]
    ms = _group_sum(o * o, _group_ones(GW)) * (1.0 / GDN_DV)
    ya = o * lax.rsqrt(ms + EPS) * gg_ref[0:1, :] * _silu(z_ref[...])
    yb = jnp.where(j * tm < geom["nl"], ybl_ref[...], ybc_ref[...])
    u = cc_ref[...] * ch_ref[...]
    hp = jnp.where(first, 0.0, ccp_ref[7:8, :] * chp_ref[7:8, :])
    hn = jnp.where(last, 0.0, ccn_ref[0:1, :] * chn_ref[0:1, :])
    yc = cb_ref[...] * _conv3(u, hp, hn, cw_ref)
    acc = _dot(ya.astype(BF16), w_ref[0:A_Z, :])
    acc += _dot(yb, w_ref[A_Z:A_Z + ATT_Q, :])
    acc += _dot(yc.astype(BF16), w_ref[A_Z + ATT_Q:, :])
    o_ref[...] = x_ref[...] + mod_ref[2:3, :] * acc


def _outproj(xs, of, ob, p, yb_lat, yb_ctx, mod, w, gg, conv_c, n_rows, geom):
    d = xs.shape[1]
    tm = geom["tm_cv"]
    nlt, tpb = geom["nl"] // tm, geom["l"] // tm
    nct = yb_ctx.shape[0] // tm
    hb = tm // 8
    nblk8 = xs.shape[0] // 8
    col = lambda w_, c: pl.BlockSpec((tm, w_), lambda j: (j, c // w_))
    prev = lambda c: pl.BlockSpec((8, SC_CH), lambda j: (jnp.maximum(j * hb - 1, 0), c // SC_CH))
    nxt = lambda c: pl.BlockSpec((8, SC_CH), lambda j: (jnp.minimum((j + 1) * hb, nblk8 - 1), c // SC_CH))
    gdn_o = pl.BlockSpec((tm, GW), lambda j: (_gdn_tile_perm(j, geom), 0))
    return pl.pallas_call(
        functools.partial(_outproj_kernel, geom=geom),
        grid=(n_rows // tm,),
        in_specs=[
            pl.BlockSpec((tm, d), lambda j: (j, 0)),
            gdn_o, gdn_o,
            col(A_Z, C_Z),
            pl.BlockSpec((tm, ATT_Q), lambda j: (jnp.minimum(j, nlt - 1), 0)),
            pl.BlockSpec((tm, ATT_Q), lambda j: (jnp.clip(j - nlt, 0, nct - 1), 0)),
            col(SC_CH, C_CB), col(SC_CH, C_CC), col(SC_CH, C_CH),
            prev(C_CC), prev(C_CH), nxt(C_CC), nxt(C_CH),
            pl.BlockSpec((None, 8, d), lambda j: (_mod_row(j, nlt, tpb), 0, 0)),
            pl.BlockSpec((d, d), lambda j: (0, 0)),
            pl.BlockSpec((8, GW), lambda j: (0, 0)),
            pl.BlockSpec((8, SC_CH), lambda j: (0, 0)),
        ],
        out_specs=pl.BlockSpec((tm, d), lambda j: (j, 0)),
        out_shape=jax.ShapeDtypeStruct((n_rows, d), F32),
        compiler_params=_cparams(("parallel",)),
        name="outproj",
    )(xs, of, ob, p, yb_lat, yb_ctx, p, p, p, p, p, p, p, mod, w, gg, conv_c)


def _expert_of_step(j, e, ne):
    return jnp.where(j % 2 == 0, e, ne - 1 - e)


def _experts_kernel(*refs, routed, final_norm):
    refs = list(refs)
    x_ref, g_ref, mod_ref = refs[0:3]
    del refs[0:3]
    r_ref = refs.pop(0) if routed else None
    gf_ref = refs.pop(0) if final_norm else None
    wgu_ref, wd_ref, o_ref, h_ref, acc_ref, a_ref = refs[0:6]
    comb_ref = refs[6] if routed else None
    e = pl.program_id(1)
    ne = pl.num_programs(1) - 1
    ffe = wd_ref.shape[0]

    def gate_up():
        gu = _dot(h_ref[...], wgu_ref[...])
        a_ref[...] = (_silu(gu[:, :ffe]) * gu[:, ffe:]).astype(BF16)

    def down(a_prev):
        y = _dot(a_prev, wd_ref[...])
        if routed:
            comb = comb_ref[...]
            lane = lax.broadcasted_iota(jnp.int32, comb.shape, 1)
            prev_expert = _expert_of_step(pl.program_id(0), e - 1, ne)
            y = jnp.sum(jnp.where(lane == prev_expert, comb, 0.0), axis=-1, keepdims=True) * y
        acc_ref[...] += y

    @pl.when(e == 0)
    def _():
        h = _norm_mod(x_ref[...], g_ref[...], mod_ref[3:4, :], mod_ref[4:5, :])
        h_ref[...] = h.astype(BF16)
        acc_ref[...] = jnp.zeros_like(acc_ref)
        if routed:
            logits = _dot_split3(h, r_ref[...])
            lane = lax.broadcasted_iota(jnp.int32, logits.shape, 1).astype(F32)
            valid = lane < N_EXPERTS
            logits = jnp.where(valid, logits, -jnp.inf)
            ex = jnp.exp(logits - jnp.max(logits, axis=-1, keepdims=True))
            probs = jnp.where(valid, ex / jnp.sum(ex, axis=-1, keepdims=True), -1.0)
            p1 = jnp.max(probs, axis=-1, keepdims=True)
            i1 = jnp.min(jnp.where(probs == p1, lane, float(LANE)), axis=-1, keepdims=True)
            rest = jnp.where(lane == i1, -1.0, probs)
            p2 = jnp.max(rest, axis=-1, keepdims=True)
            i2 = jnp.min(jnp.where(rest == p2, lane, float(LANE)), axis=-1, keepdims=True)
            den = p1 + p2
            comb_ref[...] = jnp.where(lane == i1, p1 / den, 0.0) + jnp.where(lane == i2, p2 / den, 0.0)
        gate_up()

    @pl.when(jnp.logical_and(e > 0, e < ne))
    def _():
        a_prev = a_ref[...]
        down(a_prev)
        gate_up()

    @pl.when(e == ne)
    def _():
        down(a_ref[...])
        y = x_ref[...] + mod_ref[5:6, :] * acc_ref[...]
        if final_norm:
            y = y * lax.rsqrt(jnp.mean(y * y, axis=-1, keepdims=True) + EPS) * gf_ref[...]
        o_ref[...] = y


def _experts(xs, g, mod, router, wgu, wd, n_rows, geom, final_g=None):
    d = xs.shape[1]
    ne, ffe = wd.shape[0], wd.shape[1]
    tm = geom["tm_mm"]
    nlt, tpb = geom["nl"] // tm, geom["l"] // tm
    routed = router is not None
    in_specs = [
        pl.BlockSpec((tm, d), lambda j, e: (j, 0)),
        pl.BlockSpec((1, d), lambda j, e: (0, 0)),
        pl.BlockSpec((None, 8, d), lambda j, e: (_mod_row(j, nlt, tpb), 0, 0)),
    ]
    scratch = [pltpu.VMEM((tm, d), BF16), pltpu.VMEM((tm, d), F32), pltpu.VMEM((tm, ffe), BF16)]
    args = [xs, g, mod]
    if routed:
        in_specs.append(pl.BlockSpec((d, LANE), lambda j, e: (0, 0)))
        scratch.append(pltpu.VMEM((tm, LANE), F32))
        args.append(router)
    if final_g is not None:
        in_specs.append(pl.BlockSpec((1, d), lambda j, e: (0, 0)))
        args.append(final_g)
    in_specs += [
        pl.BlockSpec((None, d, 2 * ffe), lambda j, e: (_expert_of_step(j, jnp.minimum(e, ne - 1), ne), 0, 0)),
        pl.BlockSpec((None, ffe, d), lambda j, e: (_expert_of_step(j, jnp.maximum(e - 1, 0), ne), 0, 0)),
    ]
    return pl.pallas_call(
        functools.partial(_experts_kernel, routed=routed, final_norm=final_g is not None),
        grid=(n_rows // tm, ne + 1),
        in_specs=in_specs,
        out_specs=pl.BlockSpec((tm, d), lambda j, e: (j, 0)),
        out_shape=jax.ShapeDtypeStruct((n_rows, d), F32),
        scratch_shapes=scratch,
        compiler_params=_cparams(("parallel", "arbitrary")),
        name="moe" if routed else "ffn",
    )(*args, wgu, wd)


def _rope_tables(l, tm):
    quarter = HEAD_DIM // 4
    inv_freq = ROPE_THETA ** (-jnp.arange(quarter, dtype=F32) / quarter)
    t = jnp.arange(l, dtype=jnp.int32)
    ang_r = (t // GRID_W).astype(F32)[:, None] * inv_freq[None, :]
    ang_c = (t % GRID_W).astype(F32)[:, None] * inv_freq[None, :]
    cos = jnp.concatenate([jnp.cos(ang_r)] * 2 + [jnp.cos(ang_c)] * 2, axis=1)
    sin = jnp.concatenate([-jnp.sin(ang_r), jnp.sin(ang_r), -jnp.sin(ang_c), jnp.sin(ang_c)], axis=1)
    cos = jnp.tile(cos, (1, LANE // HEAD_DIM))
    sin = jnp.tile(sin, (1, LANE // HEAD_DIM))
    cos = jnp.concatenate([jnp.ones((tm, LANE), F32), cos], axis=0)
    sin = jnp.concatenate([jnp.zeros((tm, LANE), F32), sin], axis=0)
    return cos, sin


def _pad_rows(a, rows=8):
    return jnp.pad(a, ((0, rows - a.shape[0]), (0, 0)))


def _permute_w_in(w):
    o = 0
    parts = {}
    for name, size in (("qkv", A_QKV), ("z", A_Z), ("a", 2 * GDN_HEADS), ("b", 2 * GDN_HEADS),
                       ("q", ATT_Q), ("k", ATT_KV), ("v", ATT_KV), ("cb", SC_CH), ("cc", SC_CH), ("ch", SC_CH)):
        parts[name] = w[:, o:o + size]
        o += size
    pad = jnp.zeros((w.shape[0], P_COLS - C_AB - 4 * GDN_HEADS), w.dtype)
    cols = [parts[k] for k in ("qkv", "z", "q", "k", "v", "cb", "cc", "ch", "a", "b")] + [pad]
    return jnp.concatenate(cols, axis=1).astype(BF16)


def _split_ffn(wg, wu, wd):
    d, ff = wg.shape
    ffe = ff // FFN_SPLIT
    wgu = jnp.concatenate([wg.reshape(d, FFN_SPLIT, ffe), wu.reshape(d, FFN_SPLIT, ffe)], axis=-1)
    return jnp.swapaxes(wgu, 0, 1).astype(BF16), wd.reshape(FFN_SPLIT, ffe, d).astype(BF16)


def kernel(x, c, ctx, c_ctx, w_mod, b_mod, norm1, norm2, w_in, conv_a, a_log, dt_bias, gdn_norm,
           q_norm, k_norm, conv_c, w_out, ffn_w_gate, ffn_w_up, ffn_w_down, router,
           moe_w_gate, moe_w_up, moe_w_down, norm_f):
    b, l, d = x.shape
    lc = ctx.shape[1]
    depth = w_mod.shape[0]
    nl, nc = b * l, b * lc
    tm_mm = math.gcd(TM_MM, math.gcd(l, nc))
    tm_cv = math.gcd(TM_CV, math.gcd(l, lc))
    tq = math.gcd(TQ, math.gcd(l, lc))
    gdn_rows = math.gcd(GDN_CHUNKS_PER_STEP * GDN_CHUNK, math.gcd(l, lc))
    gdn_nb = GDN_BATCH_PER_STEP if b % GDN_BATCH_PER_STEP == 0 else 1
    assert l % GRID_W == 0 and l % TK == 0 and lc % TK == 0 and nl % lc == 0 and gdn_rows == tm_cv
    geom = dict(b=b, l=l, lc=lc, nl=nl, nc=nc, tm_mm=tm_mm, tm_cv=tm_cv, tq=tq,
                gdn_rows=gdn_rows, gdn_nb=gdn_nb)
    n = nl + nc

    mod = _modulation(c, c_ctx, w_mod, b_mod)
    cos_t, sin_t = _rope_tables(l, tm_cv)
    xs = jnp.concatenate([x.reshape(nl, d), ctx.reshape(nc, d)], axis=0)

    for li in range(depth):
        last = li == depth - 1
        rows_out = nl if last else n
        w_in_p = _permute_w_in(w_in[li])
        gparams = _pad_rows(jnp.stack([jnp.pad(a_log[li].reshape(-1), (0, LANE - 2 * GDN_HEADS)),
                                       jnp.pad(dt_bias[li].reshape(-1), (0, LANE - 2 * GDN_HEADS))]))
        p = _inproj(xs, norm1[li][None, :], mod[li], w_in_p, geom)
        gqkv, gates, qt, kh, vt = _prep(
            p, cos_t, sin_t, _pad_rows(conv_a[li]), gparams,
            _pad_rows(jnp.tile(q_norm[li], ATT_HEADS)[None, :]),
            _pad_rows(jnp.tile(k_norm[li], ATT_KV_HEADS)[None, :]), geom)
        of, ob = _gdn(gqkv, gates, geom)
        yb_lat = _attention(qt, kh, vt, True, geom)
        yb_ctx = yb_lat if last else _attention(qt, kh, vt, False, geom)
        xs = _outproj(xs, of, ob, p, yb_lat, yb_ctx, mod[li], w_out[li].astype(BF16),
                      _pad_rows(jnp.tile(gdn_norm[li], GDN_HEADS)[None, :]), _pad_rows(conv_c[li]),
                      rows_out, geom)
        i = li // 2
        final_g = norm_f[None, :] if last else None
        if li % 2 == 0:
            wgu, wd = _split_ffn(ffn_w_gate[i], ffn_w_up[i], ffn_w_down[i])
            xs = _experts(xs, norm2[li][None, :], mod[li], None, wgu, wd, rows_out, geom, final_g)
        else:
            wgu = jnp.concatenate([moe_w_gate[i], moe_w_up[i]], axis=-1).astype(BF16)
            xs = _experts(xs, norm2[li][None, :], mod[li],
                          jnp.pad(router[i], ((0, 0), (0, LANE - N_EXPERTS))), wgu,
                          moe_w_down[i].astype(BF16), rows_out, geom, final_g)
    return xs.reshape(b, l, d)
```

```python
import functools
import math

import jax
import jax.numpy as jnp
from jax import lax
from jax.experimental import pallas as pl
from jax.experimental.pallas import tpu as pltpu

F32 = jnp.float32
BF16 = jnp.bfloat16
EPS = 1e-6

GRID_W = 64
HEAD_DIM = 64
GDN_HEADS = 4
GDN_DK = 64
GDN_DV = 64
GDN_CHUNK = 64
ATT_HEADS = 8
ATT_KV_HEADS = 2
ATT_GROUPS = ATT_HEADS // ATT_KV_HEADS
ROPE_THETA = 10000.0
SC_CH = 256
N_EXPERTS = 8
TOP_K = 2
A_QKV = GDN_HEADS * (2 * GDN_DK + GDN_DV)
A_Z = GDN_HEADS * GDN_DV
GW = GDN_HEADS * GDN_DK
ATT_Q = ATT_HEADS * HEAD_DIM
ATT_KV = ATT_KV_HEADS * HEAD_DIM

C_QKV = 0
C_Z = 768
C_CB = 1024
C_CC = 1280
C_CH = 1536
C_AB = 1792
P_COLS = 1920
C_Q = P_COLS
C_K = C_Q + ATT_Q
C_V = C_K + ATT_KV
W_COLS = C_V + ATT_KV

VMEM_LIMIT = 56 * 1024 * 1024
LANE = 128

TM_MM = 512
TM_CV = 256
TQ = 256
TQ_LAT = 512
TK = 256
V_ROWS = HEAD_DIM + 16
ATT_CHUNKS_PER_ITER = 4
GDN_CHUNKS_PER_STEP = 4
GDN_BATCH_PER_STEP = 4
FFN_SPLIT = 2


def _cparams(sem):
    return pltpu.CompilerParams(dimension_semantics=sem, vmem_limit_bytes=VMEM_LIMIT)


def _sigmoid(x):
    return 1.0 / (1.0 + jnp.exp(-x))


def _silu(x):
    return x * _sigmoid(x)


def _dot(a, b):
    return jnp.dot(a, b, preferred_element_type=F32)


def _dot_split3(a, b):
    ah = a.astype(BF16)
    al = (a - ah.astype(F32)).astype(BF16)
    bh = b.astype(BF16)
    bl = (b - bh.astype(F32)).astype(BF16)
    return _dot(ah, bh) + _dot(al, bh) + _dot(ah, bl)


def _group_sum(x2, gmat):
    hi = x2.astype(BF16)
    lo = (x2 - hi.astype(F32)).astype(BF16)
    return _dot(hi, gmat) + _dot(lo, gmat)


def _group_ones(width):
    r = lax.broadcasted_iota(jnp.int32, (width, width), 0) // 64
    c = lax.broadcasted_iota(jnp.int32, (width, width), 1) // 64
    return jnp.where(r == c, 1.0, 0.0).astype(BF16)


def _mod_kernel(s_ref, w_ref, b_ref, o_ref):
    s = _silu(s_ref[...])
    o_ref[...] = jnp.dot(s, w_ref[...], precision=lax.Precision.HIGHEST,
                         preferred_element_type=F32) + b_ref[...]


def _modulation(c, c_ctx, w_mod, b_mod):
    depth, d, d6 = w_mod.shape
    rows = c.shape[0] + 1
    rpad = -(-rows // 8) * 8
    s = jnp.concatenate([c_ctx[None, :], c, jnp.zeros((rpad - rows, d), F32)], axis=0)
    tn = 1536
    out = pl.pallas_call(
        _mod_kernel,
        grid=(depth, d6 // tn),
        in_specs=[
            pl.BlockSpec((rpad, d), lambda l, j: (0, 0)),
            pl.BlockSpec((None, d, tn), lambda l, j: (l, 0, j)),
            pl.BlockSpec((None, 1, tn), lambda l, j: (l, 0, j)),
        ],
        out_specs=pl.BlockSpec((None, rpad, tn), lambda l, j: (l, 0, j)),
        out_shape=jax.ShapeDtypeStruct((depth, rpad, d6), F32),
        compiler_params=_cparams(("parallel", "parallel")),
        name="modulation",
    )(s, w_mod, b_mod.reshape(depth, 1, d6))
    m = out[:, :rows].reshape(depth, rows, 6, d)
    return jnp.pad(m, ((0, 0), (0, 0), (0, 2), (0, 0)))


def _norm_mod(x, g, shift, scale):
    ms = jnp.mean(x * x, axis=-1, keepdims=True)
    y = x * lax.rsqrt(ms + EPS) * g
    return y * (1.0 + scale) + shift


def _inproj_kernel(x_ref, g_ref, mod_ref, w_ref, cos_ref, sin_ref, qg_ref, kg_ref,
                   o_ref, qt_ref, kh_ref, vt_ref):
    tm = x_ref.shape[0]
    h = _norm_mod(x_ref[...], g_ref[...], mod_ref[0:1, :], mod_ref[1:2, :]).astype(BF16)
    qkv = _dot(h, w_ref[:, C_Q:W_COLS])
    for c0 in range(0, P_COLS, 256):
        c1 = min(c0 + 256, P_COLS)
        o_ref[:, c0:c1] = _dot(h, w_ref[:, c0:c1])
    q = qkv[:, 0:ATT_Q]
    k = qkv[:, ATT_Q:ATT_Q + ATT_KV]
    v = qkv[:, ATT_Q + ATT_KV:]
    cos = cos_ref[...]
    sin = sin_ref[...]
    lane128 = lax.broadcasted_iota(jnp.int32, cos.shape, 1)
    first_half = (lane128 % 32) < 16

    def rope(xb):
        r_lo = pltpu.roll(xb, 16, 1)
        r_hi = pltpu.roll(xb, 112, 1)
        return xb * cos + jnp.where(first_half, r_hi, r_lo) * sin

    q = q * lax.rsqrt(_group_sum(q * q, _group_ones(ATT_Q)) * (1.0 / HEAD_DIM) + EPS) * qg_ref[0:1, :]
    qr = jnp.concatenate([rope(q[:, c:c + LANE]) for c in range(0, ATT_Q, LANE)], axis=1)
    qt_ref[...] = (qr * (HEAD_DIM ** -0.5 * math.log2(math.e))).T.astype(BF16)
    k = k * lax.rsqrt(_group_sum(k * k, _group_ones(ATT_KV)) * (1.0 / HEAD_DIM) + EPS) * kg_ref[0:1, :]
    kr = rope(k).astype(BF16)
    for hd in range(ATT_KV_HEADS):
        kh_ref[hd] = kr[:, hd * HEAD_DIM:(hd + 1) * HEAD_DIM]
    vt = v.T
    pad_row = lax.broadcasted_iota(jnp.int32, (V_ROWS - HEAD_DIM, tm), 0)
    ones_rows = jnp.where(pad_row == 0, 1.0, 0.0)
    vt_ref[...] = jnp.concatenate(
        [part for hd in range(ATT_KV_HEADS) for part in (vt[hd * HEAD_DIM:(hd + 1) * HEAD_DIM, :], ones_rows)],
        axis=0).astype(BF16)


def _mod_row(j, n_lat_tiles, tiles_per_batch):
    return jnp.where(j < n_lat_tiles, 1 + j // tiles_per_batch, 0)


def _inproj(xs, g, mod, w, cos_t, sin_t, qg, kg, geom):
    n, d = xs.shape
    tm = geom["tm_mm"]
    nlt, tpb = geom["nl"] // tm, geom["l"] // tm

    def rope_idx(j):
        return jnp.where(j < nlt, 1 + j % tpb, 0)

    return pl.pallas_call(
        _inproj_kernel,
        grid=(n // tm,),
        in_specs=[
            pl.BlockSpec((tm, d), lambda j: (j, 0)),
            pl.BlockSpec((1, d), lambda j: (0, 0)),
            pl.BlockSpec((None, 8, d), lambda j: (_mod_row(j, nlt, tpb), 0, 0)),
            pl.BlockSpec((d, W_COLS), lambda j: (0, 0)),
            pl.BlockSpec((tm, LANE), lambda j: (rope_idx(j), 0)),
            pl.BlockSpec((tm, LANE), lambda j: (rope_idx(j), 0)),
            pl.BlockSpec((8, ATT_Q), lambda j: (0, 0)),
            pl.BlockSpec((8, ATT_KV), lambda j: (0, 0)),
        ],
        out_specs=[
            pl.BlockSpec((tm, P_COLS), lambda j: (j, 0)),
            pl.BlockSpec((ATT_Q, tm), lambda j: (0, j)),
            pl.BlockSpec((ATT_KV_HEADS, tm, HEAD_DIM), lambda j: (0, j, 0)),
            pl.BlockSpec((ATT_KV_HEADS * V_ROWS, tm), lambda j: (0, j)),
        ],
        out_shape=[
            jax.ShapeDtypeStruct((n, P_COLS), F32),
            jax.ShapeDtypeStruct((ATT_Q, n), BF16),
            jax.ShapeDtypeStruct((ATT_KV_HEADS, n, HEAD_DIM), BF16),
            jax.ShapeDtypeStruct((ATT_KV_HEADS * V_ROWS, n), BF16),
        ],
        compiler_params=_cparams(("parallel",)),
        name="inproj",
    )(xs, g, mod, w, cos_t, sin_t, qg, kg)


def _seq_flags(j, tm, geom):
    row0 = j * tm
    is_lat = row0 < geom["nl"]
    pos = jnp.where(is_lat, row0 % geom["l"], (row0 - geom["nl"]) % geom["lc"])
    slen = jnp.where(is_lat, geom["l"], geom["lc"])
    return pos == 0, pos + tm == slen


def _conv3(u, prev_row, next_row, w_ref):
    tm = u.shape[0]
    rid = lax.broadcasted_iota(jnp.int32, u.shape, 0)
    up = jnp.where(rid == 0, prev_row, pltpu.roll(u, 1, 0))
    un = jnp.where(rid == tm - 1, next_row, pltpu.roll(u, tm - 1, 0))
    return w_ref[0:1, :] * up + w_ref[1:2, :] * u + w_ref[2:3, :] * un


def _softplus(x):
    return jnp.maximum(x, 0.0) + jnp.log(1.0 + jnp.exp(-jnp.abs(x)))


def _gdn_tile_perm(j, geom):
    cs, nb = geom["gdn_rows"], geom["gdn_nb"]
    nlt, tpb, tpc = geom["nl"] // cs, geom["l"] // cs, geom["lc"] // cs
    jc = j - nlt
    lat = ((j // tpb // nb) * tpb + j % tpb) * nb + (j // tpb) % nb
    ctx = nlt + ((jc // tpc // nb) * tpc + jc % tpc) * nb + (jc // tpc) % nb
    return jnp.where(j < nlt, lat, ctx)


def _prep_kernel(qkv_ref, qkvp_ref, qkvn_ref, ab_ref, cw_ref, gp_ref, gqkv_ref, gate_ref, *, geom):
    j = pl.program_id(0)
    tm = qkv_ref.shape[0]
    first, last = _seq_flags(j, tm, geom)
    u = qkv_ref[...]
    hp = jnp.where(first, 0.0, qkvp_ref[7:8, :])
    hn = jnp.where(last, 0.0, qkvn_ref[0:1, :])
    s = _silu(_conv3(u, hp, hn, cw_ref))
    g256 = _group_ones(GW)
    qg = s[:, 0:GW]
    kg = s[:, GW:2 * GW]
    qg = qg * lax.rsqrt(_group_sum(qg * qg, g256) + EPS) * (GDN_DK ** -0.5)
    kg = kg * lax.rsqrt(_group_sum(kg * kg, g256) + EPS)
    gqkv_ref[:, 0:GW] = qg
    gqkv_ref[:, GW:2 * GW] = kg
    gqkv_ref[:, 2 * GW:3 * GW] = s[:, 2 * GW:3 * GW]
    ab = ab_ref[...]
    lane = lax.broadcasted_iota(jnp.int32, ab.shape, 1)
    log_a = -jnp.exp(gp_ref[0:1, :]) * _softplus(ab + gp_ref[1:2, :])
    gate_ref[...] = jnp.where(lane < 2 * GDN_HEADS, log_a, _sigmoid(ab))


def _prep(p, conv_a, gparams, geom):
    n = p.shape[0]
    tm = geom["tm_cv"]
    hb = tm // 8
    nblk8 = n // 8
    col = lambda w, c: pl.BlockSpec((tm, w), lambda j: (j, c // w))
    return pl.pallas_call(
        functools.partial(_prep_kernel, geom=geom),
        grid=(n // tm,),
        in_specs=[
            col(A_QKV, C_QKV),
            pl.BlockSpec((8, A_QKV), lambda j: (jnp.maximum(j * hb - 1, 0), 0)),
            pl.BlockSpec((8, A_QKV), lambda j: (jnp.minimum((j + 1) * hb, nblk8 - 1), 0)),
            col(LANE, C_AB),
            pl.BlockSpec((8, A_QKV), lambda j: (0, 0)),
            pl.BlockSpec((8, LANE), lambda j: (0, 0)),
        ],
        out_specs=[
            pl.BlockSpec((tm, A_QKV), lambda j: (_gdn_tile_perm(j, geom), 0)),
            pl.BlockSpec((tm, LANE), lambda j: (_gdn_tile_perm(j, geom), 0)),
        ],
        out_shape=[jax.ShapeDtypeStruct((n, A_QKV), F32), jax.ShapeDtypeStruct((n, LANE), F32)],
        compiler_params=_cparams(("parallel",)),
        name="mixer_prep",
    )(p, p, p, p, conv_a, gparams)


def _head_of_lane(shape):
    return lax.broadcasted_iota(jnp.int32, shape, 1) // 64


def _block_diag(xb):
    hl = _head_of_lane(xb.shape)
    zero = jnp.zeros_like(xb)
    return jnp.concatenate([jnp.where(hl == h, xb, zero) for h in range(GDN_HEADS)], axis=0)


def _dot_exact3(a_bf, x):
    h1 = x.astype(BF16)
    r1 = x - h1.astype(F32)
    h2 = r1.astype(BF16)
    h3 = (r1 - h2.astype(F32)).astype(BF16)
    return _dot(a_bf, h1) + _dot(a_bf, h2) + _dot(a_bf, h3)


def _gdn_local(insts):
    cs = GDN_CHUNK
    hl = _head_of_lane((cs, GW))
    row = lax.broadcasted_iota(jnp.int32, (cs, GW), 0)
    col = lax.broadcasted_iota(jnp.int32, (cs, GW), 1) % 64
    ti = lax.broadcasted_iota(jnp.int32, (cs, cs), 0)
    tj = lax.broadcasted_iota(jnp.int32, (cs, cs), 1)
    same16 = (row // 16) == (col // 16)
    same32 = (row // 32) == (col // 32)
    eye = jnp.where(row == col, 1.0, 0.0)
    nt = (((1,), (1,)), ((), ()))

    def widen(gates, base):
        out = jnp.zeros((cs, GW), F32)
        for h in range(GDN_HEADS):
            out = jnp.where(hl == h, gates[:, base + h:base + h + 1], out)
        return out

    for it in insts:
        rev = it["rev"]
        d = 1 if rev else 0
        it["la_w"] = widen(it["gates"], GDN_HEADS * d)
        it["beta_w"] = widen(it["gates"], 2 * GDN_HEADS + GDN_HEADS * d)
        it["tri"] = jnp.where((tj >= ti) if rev else (tj <= ti), 1.0, 0.0).astype(BF16)
        it["incl"] = (row <= col) if rev else (row >= col)
        it["strict"] = (row < col) if rev else (row > col)
        it["kb"] = it["k"].astype(BF16)
        it["kbd"] = _block_diag(it["kb"])
    for it in insts:
        kq = lax.dot_general(jnp.concatenate([it["kb"], it["q"].astype(BF16)], axis=0), it["kbd"], nt,
                             preferred_element_type=F32)
        it["kk_w"], it["qk_w"] = kq[0:cs, :], kq[cs:2 * cs, :]
    for it in insts:
        it["gcc"] = _dot_exact3(it["tri"], it["la_w"])
    for it in insts:
        rev, la_w, incl = it["rev"], it["la_w"], it["incl"]
        gcr = jnp.sum(jnp.where((row >= col) if rev else (row <= col), la_w, 0.0), axis=0, keepdims=True)
        it["decay"] = jnp.where(incl, jnp.exp(jnp.where(incl, it["gcc"] - gcr, 0.0)), 0.0)
        it["lower"] = jnp.where(it["strict"], it["beta_w"] * it["kk_w"] * it["decay"], 0.0)
        pw = jnp.where(same16, it["lower"], 0.0)
        it["t"] = eye - pw
        it["pwb"] = pw.astype(BF16)
    for it in insts:
        it["pwb"] = _dot(it["pwb"], _block_diag(it["pwb"])).astype(BF16)
    for _ in range(2):
        for it in insts:
            both = _dot(jnp.concatenate([it["pwb"], it["t"].astype(BF16)], axis=0), _block_diag(it["pwb"]))
            it["pwb"] = both[0:cs, :].astype(BF16)
            it["t"] = it["t"] + both[cs:2 * cs, :]
    for it in insts:
        it["t"] = it["t"] + _dot(it["t"].astype(BF16), _block_diag(it["pwb"]))
    for level in range(2):
        for it in insts:
            off = (jnp.where(same32 & jnp.logical_not(same16), it["lower"], 0.0) if level == 0
                   else jnp.where(same32, 0.0, it["lower"]))
            it["tb"] = it["t"].astype(BF16)
            it["tc"] = _dot(it["tb"], _block_diag(off.astype(BF16)))
        for it in insts:
            it["t"] = it["t"] - _dot(it["tc"].astype(BF16), _block_diag(it["tb"]))
    for it in insts:
        it["tb"] = it["t"].astype(BF16)
        it["egc"] = jnp.exp(it["gcc"])
        it["u"] = _dot(it["tb"], _block_diag((it["v"] * it["beta_w"]).astype(BF16)))
    for it in insts:
        it["w"] = _dot(it["tb"], _block_diag((it["k"] * (it["beta_w"] * it["egc"])).astype(BF16)))
    for it in insts:
        gcc = it["gcc"]
        gl = gcc[0:1, :] if it["rev"] else gcc[cs - 1:cs, :]
        it["qkm"] = jnp.where(it["incl"], it["qk_w"] * it["decay"], 0.0).astype(BF16)
        it["q_dec"] = (it["q"] * it["egc"]).astype(BF16)
        it["k_dec"] = (it["k"] * jnp.exp(gl - gcc)).astype(BF16)
        it["a_last"] = jnp.exp(gl)
        it["wb"] = it["w"].astype(BF16)


def _gdn_recurrence(group):
    r2 = lax.broadcasted_iota(jnp.int32, (GW, GW), 0) // 64
    c2 = lax.broadcasted_iota(jnp.int32, (GW, GW), 1) // 64
    tn = (((0,), (0,)), ((), ()))
    for it in group:
        it["s"] = it["s_ref"][...]
        it["sb"] = it["s"].astype(BF16)
    for it in group:
        it["wq_s"] = _dot(jnp.concatenate([it["wb"], it["q_dec"]], axis=0), it["sb"])
    for it in group:
        it["vb"] = (it["u"] - it["wq_s"][0:GDN_CHUNK, :]).astype(BF16)
    for it in group:
        it["o"] = it["wq_s"][GDN_CHUNK:2 * GDN_CHUNK, :] + _dot(it["qkm"], _block_diag(it["vb"]))
    for it in group:
        upd = lax.dot_general(it["k_dec"], it["vb"], tn, preferred_element_type=F32)
        it["s_ref"][...] = it["s"] * it["a_last"] + jnp.where(r2 == c2, upd, 0.0)
    for it in group:
        it["o_ref"][it["rows"], :] = it["o"]


def _gdn_kernel(xf_ref, gf_ref, xb_ref, gb_ref, of_ref, ob_ref, *s_refs, seq_rows):
    @pl.when(pl.program_id(1) == 0)
    def _():
        for s_ref in s_refs:
            s_ref[...] = jnp.zeros_like(s_ref)

    n_chunks = seq_rows // GDN_CHUNK
    scans = []
    for slot in range(xf_ref.shape[0] // seq_rows):
        for x_ref, g_ref, o_ref, rev in ((xf_ref, gf_ref, of_ref, False), (xb_ref, gb_ref, ob_ref, True)):
            seq = []
            for ci in (reversed(range(n_chunks)) if rev else range(n_chunks)):
                r0 = slot * seq_rows + ci * GDN_CHUNK
                rows = slice(r0, r0 + GDN_CHUNK)
                seq.append(dict(rev=rev, rows=rows, o_ref=o_ref, s_ref=s_refs[len(scans)], gates=g_ref[rows, :],
                                q=x_ref[rows, 0:GW], k=x_ref[rows, GW:2 * GW], v=x_ref[rows, 2 * GW:3 * GW]))
            scans.append(seq)
    _gdn_local([it for group in zip(*scans) for it in group])
    for group in zip(*scans):
        _gdn_recurrence(list(group))


def _gdn(gqkv, gates, geom):
    n = gqkv.shape[0]
    cs, nb = geom["gdn_rows"], geom["gdn_nb"]
    nl_c, nc_c = geom["l"] // cs, geom["lc"] // cs
    lat_blocks = geom["nl"] // (cs * nb)
    steps = nl_c + nc_c

    def fwd(g, s):
        return jnp.where(s < nc_c, lat_blocks + g * nc_c + s, g * nl_c + s - nc_c)

    def bwd(g, s):
        return jnp.where(s < nc_c, lat_blocks + g * nc_c + (nc_c - 1 - s), g * nl_c + (nl_c - 1 - (s - nc_c)))

    return pl.pallas_call(
        functools.partial(_gdn_kernel, seq_rows=cs),
        grid=(geom["b"] // nb, steps),
        in_specs=[
            pl.BlockSpec((nb * cs, A_QKV), lambda g, s: (fwd(g, s), 0)),
            pl.BlockSpec((nb * cs, LANE), lambda g, s: (fwd(g, s), 0)),
            pl.BlockSpec((nb * cs, A_QKV), lambda g, s: (bwd(g, s), 0)),
            pl.BlockSpec((nb * cs, LANE), lambda g, s: (bwd(g, s), 0)),
        ],
        out_specs=[
            pl.BlockSpec((nb * cs, GW), lambda g, s: (fwd(g, s), 0)),
            pl.BlockSpec((nb * cs, GW), lambda g, s: (bwd(g, s), 0)),
        ],
        out_shape=[jax.ShapeDtypeStruct((n, GW), F32), jax.ShapeDtypeStruct((n, GW), F32)],
        scratch_shapes=[pltpu.VMEM((GW, GW), F32)] * (2 * nb),
        compiler_params=_cparams(("parallel", "arbitrary")),
        name="gdn_scan",
    )(gqkv, gates, gqkv, gates)


def _attn_kernel(*refs, lat_queries):
    if lat_queries:
        qt_ref, kc_ref, kl_ref, vc_ref, vl_ref, o_ref, acc_ref, st_ref = refs
    else:
        qt_ref, kc_ref, vc_ref, o_ref, acc_ref = refs
    tq = qt_ref.shape[1]
    acc_ref[...] = jnp.zeros(acc_ref.shape, F32)
    qrows = [slice(g * HEAD_DIM, (g + 1) * HEAD_DIM) for g in range(ATT_GROUPS)]
    arows = [slice(g * V_ROWS, (g + 1) * V_ROWS) for g in range(ATT_GROUPS)]

    def scores(k_ref, off):
        kc = k_ref[pl.ds(off, TK), :]
        return [_dot(kc, qt_ref[r, :]) for r in qrows]

    def update(v_ref, off, sts, ms):
        vc = v_ref[:, pl.ds(off, TK)]
        new_ms, ps, alphas = [], [], []
        for st, m_old in zip(sts, ms):
            m_new = jnp.maximum(m_old, jnp.max(st, axis=0, keepdims=True))
            ps.append(jnp.exp2(st - m_new).astype(BF16))
            alphas.append(jnp.exp2(m_old - m_new))
            new_ms.append(m_new)
        pvs = [_dot(vc, p) for p in ps]
        for r, alpha, pv in zip(arows, alphas, pvs):
            acc_ref[r, :] = alpha * acc_ref[r, :] + pv
        return tuple(new_ms)

    ms = (jnp.full((1, tq), -jnp.inf, F32),) * ATT_GROUPS
    n_ctx = kc_ref.shape[0] // TK
    cur = scores(kc_ref, 0)
    for c in range(1, n_ctx):
        nxt = scores(kc_ref, c * TK)
        ms = update(vc_ref, (c - 1) * TK, cur, ms)
        cur = nxt
    if lat_queries:
        n_lat = kl_ref.shape[0] // TK
        nxt = scores(kl_ref, 0)
        ms = update(vc_ref, (n_ctx - 1) * TK, cur, ms)
        for g in range(ATT_GROUPS):
            st_ref[g] = nxt[g]
        per_iter = math.gcd(ATT_CHUNKS_PER_ITER, n_lat)

        def body(i, ms):
            cur = [st_ref[g] for g in range(ATT_GROUPS)]
            for c in range(per_iter):
                off = pl.multiple_of((i * per_iter + c) * TK, TK)
                off_next = pl.multiple_of(jnp.minimum(off + TK, (n_lat - 1) * TK), TK)
                nxt = scores(kl_ref, off_next)
                ms = update(vl_ref, off, cur, ms)
                cur = nxt
            for g in range(ATT_GROUPS):
                st_ref[g] = cur[g]
            return ms

        ms = lax.fori_loop(0, n_lat // per_iter, body, ms)
    else:
        ms = update(vc_ref, (n_ctx - 1) * TK, cur, ms)

    out = [acc_ref[r, :][0:HEAD_DIM, :] / acc_ref[r, :][HEAD_DIM:HEAD_DIM + 1, :] for r in arows]
    o_ref[...] = jnp.concatenate(out, axis=0).T.astype(o_ref.dtype)


def _attention(qt, kh, vt, lat_queries, geom):
    b, l, lc, nl = geom["b"], geom["l"], geom["lc"], geom["nl"]
    tq = math.gcd(TQ_LAT, l) if lat_queries else geom["tq"]
    nq = (l if lat_queries else lc) // tq
    qbase = 0 if lat_queries else nl // tq
    gw = ATT_GROUPS * HEAD_DIM
    k_ctx = pl.BlockSpec((None, lc, HEAD_DIM), lambda bb, h, i: (h, nl // lc + bb, 0))
    v_ctx = pl.BlockSpec((V_ROWS, lc), lambda bb, h, i: (h, nl // lc + bb))
    k_lat = pl.BlockSpec((None, l, HEAD_DIM), lambda bb, h, i: (h, bb, 0))
    v_lat = pl.BlockSpec((V_ROWS, l), lambda bb, h, i: (h, bb))
    scratch = [pltpu.VMEM((ATT_GROUPS * V_ROWS, tq), F32)]
    if lat_queries:
        kv_specs, kv_args = [k_ctx, k_lat, v_ctx, v_lat], (kh, kh, vt, vt)
        scratch.append(pltpu.VMEM((ATT_GROUPS, TK, tq), F32))
    else:
        kv_specs, kv_args = [k_ctx, v_ctx], (kh, vt)
    return pl.pallas_call(
        functools.partial(_attn_kernel, lat_queries=lat_queries),
        grid=(b, ATT_KV_HEADS, nq),
        in_specs=[pl.BlockSpec((gw, tq), lambda bb, h, i: (h, qbase + bb * nq + i))] + kv_specs,
        out_specs=pl.BlockSpec((tq, gw), lambda bb, h, i: (bb * nq + i, h)),
        out_shape=jax.ShapeDtypeStruct((b * nq * tq, ATT_Q), BF16),
        scratch_shapes=scratch,
        compiler_params=_cparams(("parallel", "parallel", "arbitrary")),
        name="attention_lat" if lat_queries else "attention_ctx",
    )(qt, *kv_args)


def _outproj_kernel(x_ref, of_ref, ob_ref, z_ref, ybl_ref, ybc_ref, cb_ref, cc_ref, ch_ref,
                    ccp_ref, chp_ref, ccn_ref, chn_ref, mod_ref, w_ref, gg_ref, cw_ref,
                    o_ref, *, geom):
    j = pl.program_id(0)
    tm = x_ref.shape[0]
    first, last = _seq_flags(j, tm, geom)
    o = of_ref[...] + ob_ref[...]
    ms = _group_sum(o * o, _group_ones(GW)) * (1.0 / GDN_DV)
    ya = o * lax.rsqrt(ms + EPS) * gg_ref[0:1, :] * _silu(z_ref[...])
    yb = jnp.where(j * tm < geom["nl"], ybl_ref[...], ybc_ref[...])
    u = cc_ref[...] * ch_ref[...]
    hp = jnp.where(first, 0.0, ccp_ref[7:8, :] * chp_ref[7:8, :])
    hn = jnp.where(last, 0.0, ccn_ref[0:1, :] * chn_ref[0:1, :])
    yc = cb_ref[...] * _conv3(u, hp, hn, cw_ref)
    acc = _dot(ya.astype(BF16), w_ref[0:A_Z, :])
    acc += _dot(yb, w_ref[A_Z:A_Z + ATT_Q, :])
    acc += _dot(yc.astype(BF16), w_ref[A_Z + ATT_Q:, :])
    o_ref[...] = x_ref[...] + mod_ref[2:3, :] * acc


def _outproj(xs, of, ob, p, yb_lat, yb_ctx, mod, w, gg, conv_c, n_rows, geom):
    d = xs.shape[1]
    tm = geom["tm_cv"]
    nlt, tpb = geom["nl"] // tm, geom["l"] // tm
    nct = yb_ctx.shape[0] // tm
    hb = tm // 8
    nblk8 = xs.shape[0] // 8
    col = lambda w_, c: pl.BlockSpec((tm, w_), lambda j: (j, c // w_))
    prev = lambda c: pl.BlockSpec((8, SC_CH), lambda j: (jnp.maximum(j * hb - 1, 0), c // SC_CH))
    nxt = lambda c: pl.BlockSpec((8, SC_CH), lambda j: (jnp.minimum((j + 1) * hb, nblk8 - 1), c // SC_CH))
    gdn_o = pl.BlockSpec((tm, GW), lambda j: (_gdn_tile_perm(j, geom), 0))
    return pl.pallas_call(
        functools.partial(_outproj_kernel, geom=geom),
        grid=(n_rows // tm,),
        in_specs=[
            pl.BlockSpec((tm, d), lambda j: (j, 0)),
            gdn_o, gdn_o,
            col(A_Z, C_Z),
            pl.BlockSpec((tm, ATT_Q), lambda j: (jnp.minimum(j, nlt - 1), 0)),
            pl.BlockSpec((tm, ATT_Q), lambda j: (jnp.clip(j - nlt, 0, nct - 1), 0)),
            col(SC_CH, C_CB), col(SC_CH, C_CC), col(SC_CH, C_CH),
            prev(C_CC), prev(C_CH), nxt(C_CC), nxt(C_CH),
            pl.BlockSpec((None, 8, d), lambda j: (_mod_row(j, nlt, tpb), 0, 0)),
            pl.BlockSpec((d, d), lambda j: (0, 0)),
            pl.BlockSpec((8, GW), lambda j: (0, 0)),
            pl.BlockSpec((8, SC_CH), lambda j: (0, 0)),
        ],
        out_specs=pl.BlockSpec((tm, d), lambda j: (j, 0)),
        out_shape=jax.ShapeDtypeStruct((n_rows, d), F32),
        compiler_params=_cparams(("parallel",)),
        name="outproj",
    )(xs, of, ob, p, yb_lat, yb_ctx, p, p, p, p, p, p, p, mod, w, gg, conv_c)


def _expert_of_step(j, e, ne):
    return jnp.where(j % 2 == 0, e, ne - 1 - e)


def _experts_kernel(*refs, routed, final_norm):
    refs = list(refs)
    x_ref, g_ref, mod_ref = refs[0:3]
    del refs[0:3]
    r_ref = refs.pop(0) if routed else None
    gf_ref = refs.pop(0) if final_norm else None
    wgu_ref, wd_ref, o_ref, h_ref, acc_ref, a_ref = refs[0:6]
    comb_ref = refs[6] if routed else None
    e = pl.program_id(1)
    ne = pl.num_programs(1) - 1
    ffe = wd_ref.shape[0]

    def gate_up():
        gu = _dot(h_ref[...], wgu_ref[...])
        a_ref[...] = (_silu(gu[:, :ffe]) * gu[:, ffe:]).astype(BF16)

    def down(a_prev):
        y = _dot(a_prev, wd_ref[...])
        if routed:
            comb = comb_ref[...]
            lane = lax.broadcasted_iota(jnp.int32, comb.shape, 1)
            prev_expert = _expert_of_step(pl.program_id(0), e - 1, ne)
            y = jnp.sum(jnp.where(lane == prev_expert, comb, 0.0), axis=-1, keepdims=True) * y
        acc_ref[...] += y

    @pl.when(e == 0)
    def _():
        h = _norm_mod(x_ref[...], g_ref[...], mod_ref[3:4, :], mod_ref[4:5, :])
        h_ref[...] = h.astype(BF16)
        acc_ref[...] = jnp.zeros_like(acc_ref)
        if routed:
            logits = _dot_split3(h, r_ref[...])
            lane = lax.broadcasted_iota(jnp.int32, logits.shape, 1).astype(F32)
            valid = lane < N_EXPERTS
            logits = jnp.where(valid, logits, -jnp.inf)
            ex = jnp.exp(logits - jnp.max(logits, axis=-1, keepdims=True))
            probs = jnp.where(valid, ex / jnp.sum(ex, axis=-1, keepdims=True), -1.0)
            p1 = jnp.max(probs, axis=-1, keepdims=True)
            i1 = jnp.min(jnp.where(probs == p1, lane, float(LANE)), axis=-1, keepdims=True)
            rest = jnp.where(lane == i1, -1.0, probs)
            p2 = jnp.max(rest, axis=-1, keepdims=True)
            i2 = jnp.min(jnp.where(rest == p2, lane, float(LANE)), axis=-1, keepdims=True)
            den = p1 + p2
            comb_ref[...] = jnp.where(lane == i1, p1 / den, 0.0) + jnp.where(lane == i2, p2 / den, 0.0)
        gate_up()

    @pl.when(jnp.logical_and(e > 0, e < ne))
    def _():
        a_prev = a_ref[...]
        down(a_prev)
        gate_up()

    @pl.when(e == ne)
    def _():
        down(a_ref[...])
        y = x_ref[...] + mod_ref[5:6, :] * acc_ref[...]
        if final_norm:
            y = y * lax.rsqrt(jnp.mean(y * y, axis=-1, keepdims=True) + EPS) * gf_ref[...]
        o_ref[...] = y


def _experts(xs, g, mod, router, wgu, wd, n_rows, geom, final_g=None):
    d = xs.shape[1]
    ne, ffe = wd.shape[0], wd.shape[1]
    tm = geom["tm_mm"]
    nlt, tpb = geom["nl"] // tm, geom["l"] // tm
    routed = router is not None
    in_specs = [
        pl.BlockSpec((tm, d), lambda j, e: (j, 0)),
        pl.BlockSpec((1, d), lambda j, e: (0, 0)),
        pl.BlockSpec((None, 8, d), lambda j, e: (_mod_row(j, nlt, tpb), 0, 0)),
    ]
    scratch = [pltpu.VMEM((tm, d), BF16), pltpu.VMEM((tm, d), F32), pltpu.VMEM((tm, ffe), BF16)]
    args = [xs, g, mod]
    if routed:
        in_specs.append(pl.BlockSpec((d, LANE), lambda j, e: (0, 0)))
        scratch.append(pltpu.VMEM((tm, LANE), F32))
        args.append(router)
    if final_g is not None:
        in_specs.append(pl.BlockSpec((1, d), lambda j, e: (0, 0)))
        args.append(final_g)
    in_specs += [
        pl.BlockSpec((None, d, 2 * ffe), lambda j, e: (_expert_of_step(j, jnp.minimum(e, ne - 1), ne), 0, 0)),
        pl.BlockSpec((None, ffe, d), lambda j, e: (_expert_of_step(j, jnp.maximum(e - 1, 0), ne), 0, 0)),
    ]
    return pl.pallas_call(
        functools.partial(_experts_kernel, routed=routed, final_norm=final_g is not None),
        grid=(n_rows // tm, ne + 1),
        in_specs=in_specs,
        out_specs=pl.BlockSpec((tm, d), lambda j, e: (j, 0)),
        out_shape=jax.ShapeDtypeStruct((n_rows, d), F32),
        scratch_shapes=scratch,
        compiler_params=_cparams(("parallel", "arbitrary")),
        name="moe" if routed else "ffn",
    )(*args, wgu, wd)


def _rope_tables(l, tm):
    quarter = HEAD_DIM // 4
    inv_freq = ROPE_THETA ** (-jnp.arange(quarter, dtype=F32) / quarter)
    t = jnp.arange(l, dtype=jnp.int32)
    ang_r = (t // GRID_W).astype(F32)[:, None] * inv_freq[None, :]
    ang_c = (t % GRID_W).astype(F32)[:, None] * inv_freq[None, :]
    cos = jnp.concatenate([jnp.cos(ang_r)] * 2 + [jnp.cos(ang_c)] * 2, axis=1)
    sin = jnp.concatenate([-jnp.sin(ang_r), jnp.sin(ang_r), -jnp.sin(ang_c), jnp.sin(ang_c)], axis=1)
    cos = jnp.tile(cos, (1, LANE // HEAD_DIM))
    sin = jnp.tile(sin, (1, LANE // HEAD_DIM))
    cos = jnp.concatenate([jnp.ones((tm, LANE), F32), cos], axis=0)
    sin = jnp.concatenate([jnp.zeros((tm, LANE), F32), sin], axis=0)
    return cos, sin


def _pad_rows(a, rows=8):
    return jnp.pad(a, ((0, rows - a.shape[0]), (0, 0)))


def _permute_w_in(w):
    o = 0
    parts = {}
    for name, size in (("qkv", A_QKV), ("z", A_Z), ("a", 2 * GDN_HEADS), ("b", 2 * GDN_HEADS),
                       ("q", ATT_Q), ("k", ATT_KV), ("v", ATT_KV), ("cb", SC_CH), ("cc", SC_CH), ("ch", SC_CH)):
        parts[name] = w[:, o:o + size]
        o += size
    pad = jnp.zeros((w.shape[0], P_COLS - C_AB - 4 * GDN_HEADS), w.dtype)
    cols = [parts[k] for k in ("qkv", "z", "cb", "cc", "ch", "a", "b")] + [pad] + [parts[k] for k in ("q", "k", "v")]
    return jnp.concatenate(cols, axis=1).astype(BF16)


def _split_ffn(wg, wu, wd):
    d, ff = wg.shape
    ffe = ff // FFN_SPLIT
    wgu = jnp.concatenate([wg.reshape(d, FFN_SPLIT, ffe), wu.reshape(d, FFN_SPLIT, ffe)], axis=-1)
    return jnp.swapaxes(wgu, 0, 1).astype(BF16), wd.reshape(FFN_SPLIT, ffe, d).astype(BF16)


def kernel(x, c, ctx, c_ctx, w_mod, b_mod, norm1, norm2, w_in, conv_a, a_log, dt_bias, gdn_norm,
           q_norm, k_norm, conv_c, w_out, ffn_w_gate, ffn_w_up, ffn_w_down, router,
           moe_w_gate, moe_w_up, moe_w_down, norm_f):
    b, l, d = x.shape
    lc = ctx.shape[1]
    depth = w_mod.shape[0]
    nl, nc = b * l, b * lc
    tm_mm = math.gcd(TM_MM, math.gcd(l, nc))
    tm_cv = math.gcd(TM_CV, math.gcd(l, lc))
    tq = math.gcd(TQ, math.gcd(l, lc))
    gdn_rows = math.gcd(GDN_CHUNKS_PER_STEP * GDN_CHUNK, math.gcd(l, lc))
    gdn_nb = GDN_BATCH_PER_STEP if b % GDN_BATCH_PER_STEP == 0 else 1
    assert l % GRID_W == 0 and l % TK == 0 and lc % TK == 0 and nl % lc == 0 and gdn_rows == tm_cv
    geom = dict(b=b, l=l, lc=lc, nl=nl, nc=nc, tm_mm=tm_mm, tm_cv=tm_cv, tq=tq,
                gdn_rows=gdn_rows, gdn_nb=gdn_nb)
    n = nl + nc

    mod = _modulation(c, c_ctx, w_mod, b_mod)
    cos_t, sin_t = _rope_tables(l, tm_mm)
    xs = jnp.concatenate([x.reshape(nl, d), ctx.reshape(nc, d)], axis=0)

    for li in range(depth):
        last = li == depth - 1
        rows_out = nl if last else n
        w_in_p = _permute_w_in(w_in[li])
        gparams = _pad_rows(jnp.stack([jnp.pad(a_log[li].reshape(-1), (0, LANE - 2 * GDN_HEADS)),
                                       jnp.pad(dt_bias[li].reshape(-1), (0, LANE - 2 * GDN_HEADS))]))
        p, qt, kh, vt = _inproj(xs, norm1[li][None, :], mod[li], w_in_p, cos_t, sin_t,
                                _pad_rows(jnp.tile(q_norm[li], ATT_HEADS)[None, :]),
                                _pad_rows(jnp.tile(k_norm[li], ATT_KV_HEADS)[None, :]), geom)
        gqkv, gates = _prep(p, _pad_rows(conv_a[li]), gparams, geom)
        of, ob = _gdn(gqkv, gates, geom)
        yb_lat = _attention(qt, kh, vt, True, geom)
        yb_ctx = yb_lat if last else _attention(qt, kh, vt, False, geom)
        xs = _outproj(xs, of, ob, p, yb_lat, yb_ctx, mod[li], w_out[li].astype(BF16),
                      _pad_rows(jnp.tile(gdn_norm[li], GDN_HEADS)[None, :]), _pad_rows(conv_c[li]),
                      rows_out, geom)
        i = li // 2
        final_g = norm_f[None, :] if last else None
        if li % 2 == 0:
            wgu, wd = _split_ffn(ffn_w_gate[i], ffn_w_up[i], ffn_w_down[i])
            xs = _experts(xs, norm2[li][None, :], mod[li], None, wgu, wd, rows_out, geom, final_g)
        else:
            wgu = jnp.concatenate([moe_w_gate[i], moe_w_up[i]], axis=-1).astype(BF16)
            xs = _experts(xs, norm2[li][None, :], mod[li],
                          jnp.pad(router[i], ((0, 0), (0, LANE - N_EXPERTS))), wgu,
                          moe_w_down[i].astype(BF16), rows_out, geom, final_g)
    return xs.reshape(b, l, d)
```

```python
import functools
import math

import jax
import jax.numpy as jnp
from jax import lax
from jax.experimental import pallas as pl
from jax.experimental.pallas import tpu as pltpu

F32 = jnp.float32
BF16 = jnp.bfloat16
EPS = 1e-6

GRID_W = 64
HEAD_DIM = 64
GDN_HEADS = 4
GDN_DK = 64
GDN_DV = 64
GDN_CHUNK = 64
ATT_HEADS = 8
ATT_KV_HEADS = 2
ATT_GROUPS = ATT_HEADS // ATT_KV_HEADS
ROPE_THETA = 10000.0
SC_CH = 256
N_EXPERTS = 8
TOP_K = 2
A_QKV = GDN_HEADS * (2 * GDN_DK + GDN_DV)
A_Z = GDN_HEADS * GDN_DV
GW = GDN_HEADS * GDN_DK
ATT_Q = ATT_HEADS * HEAD_DIM
ATT_KV = ATT_KV_HEADS * HEAD_DIM

C_QKV = 0
C_Z = 768
C_CB = 1024
C_CC = 1280
C_CH = 1536
C_AB = 1792
P_COLS = 1920
C_Q = P_COLS
C_K = C_Q + ATT_Q
C_V = C_K + ATT_KV
W_COLS = C_V + ATT_KV

VMEM_LIMIT = 56 * 1024 * 1024
LANE = 128

TM_MM = 512
TM_CV = 256
TQ = 256
TQ_LAT = 512
TK = 256
V_ROWS = HEAD_DIM + 16
ATT_CHUNKS_PER_ITER = 8
GDN_CHUNKS_PER_STEP = 4
GDN_BATCH_PER_STEP = 4
FFN_SPLIT = 2


def _cparams(sem):
    return pltpu.CompilerParams(dimension_semantics=sem, vmem_limit_bytes=VMEM_LIMIT)


def _sigmoid(x):
    return 1.0 / (1.0 + jnp.exp(-x))


def _silu(x):
    return x * _sigmoid(x)


def _dot(a, b):
    return jnp.dot(a, b, preferred_element_type=F32)


def _dot_split3(a, b):
    ah = a.astype(BF16)
    al = (a - ah.astype(F32)).astype(BF16)
    bh = b.astype(BF16)
    bl = (b - bh.astype(F32)).astype(BF16)
    return _dot(ah, bh) + _dot(al, bh) + _dot(ah, bl)


def _group_sum(x2, gmat):
    hi = x2.astype(BF16)
    lo = (x2 - hi.astype(F32)).astype(BF16)
    return _dot(hi, gmat) + _dot(lo, gmat)


def _group_ones(width):
    r = lax.broadcasted_iota(jnp.int32, (width, width), 0) // 64
    c = lax.broadcasted_iota(jnp.int32, (width, width), 1) // 64
    return jnp.where(r == c, 1.0, 0.0).astype(BF16)


def _mod_kernel(s_ref, w_ref, b_ref, o_ref):
    s = _silu(s_ref[...])
    o_ref[...] = jnp.dot(s, w_ref[...], precision=lax.Precision.HIGHEST,
                         preferred_element_type=F32) + b_ref[...]


def _modulation(c, c_ctx, w_mod, b_mod):
    depth, d, d6 = w_mod.shape
    rows = c.shape[0] + 1
    rpad = -(-rows // 8) * 8
    s = jnp.concatenate([c_ctx[None, :], c, jnp.zeros((rpad - rows, d), F32)], axis=0)
    tn = 1536
    out = pl.pallas_call(
        _mod_kernel,
        grid=(depth, d6 // tn),
        in_specs=[
            pl.BlockSpec((rpad, d), lambda l, j: (0, 0)),
            pl.BlockSpec((None, d, tn), lambda l, j: (l, 0, j)),
            pl.BlockSpec((None, 1, tn), lambda l, j: (l, 0, j)),
        ],
        out_specs=pl.BlockSpec((None, rpad, tn), lambda l, j: (l, 0, j)),
        out_shape=jax.ShapeDtypeStruct((depth, rpad, d6), F32),
        compiler_params=_cparams(("parallel", "parallel")),
        name="modulation",
    )(s, w_mod, b_mod.reshape(depth, 1, d6))
    m = out[:, :rows].reshape(depth, rows, 6, d)
    return jnp.pad(m, ((0, 0), (0, 0), (0, 2), (0, 0)))


def _norm_mod(x, g, shift, scale):
    ms = jnp.mean(x * x, axis=-1, keepdims=True)
    y = x * lax.rsqrt(ms + EPS) * g
    return y * (1.0 + scale) + shift


def _inproj_kernel(x_ref, g_ref, mod_ref, w_ref, cos_ref, sin_ref, qg_ref, kg_ref,
                   o_ref, qt_ref, kh_ref, vt_ref):
    tm = x_ref.shape[0]
    h = _norm_mod(x_ref[...], g_ref[...], mod_ref[0:1, :], mod_ref[1:2, :]).astype(BF16)
    qkv = _dot(h, w_ref[:, C_Q:W_COLS])
    for c0 in range(0, P_COLS, 256):
        c1 = min(c0 + 256, P_COLS)
        o_ref[:, c0:c1] = _dot(h, w_ref[:, c0:c1])
    q = qkv[:, 0:ATT_Q]
    k = qkv[:, ATT_Q:ATT_Q + ATT_KV]
    v = qkv[:, ATT_Q + ATT_KV:]
    cos = cos_ref[...]
    sin = sin_ref[...]
    lane128 = lax.broadcasted_iota(jnp.int32, cos.shape, 1)
    first_half = (lane128 % 32) < 16

    def rope(xb):
        r_lo = pltpu.roll(xb, 16, 1)
        r_hi = pltpu.roll(xb, 112, 1)
        return xb * cos + jnp.where(first_half, r_hi, r_lo) * sin

    q = q * lax.rsqrt(_group_sum(q * q, _group_ones(ATT_Q)) * (1.0 / HEAD_DIM) + EPS) * qg_ref[0:1, :]
    qr = jnp.concatenate([rope(q[:, c:c + LANE]) for c in range(0, ATT_Q, LANE)], axis=1)
    qt_ref[...] = (qr * (HEAD_DIM ** -0.5 * math.log2(math.e))).T.astype(BF16)
    k = k * lax.rsqrt(_group_sum(k * k, _group_ones(ATT_KV)) * (1.0 / HEAD_DIM) + EPS) * kg_ref[0:1, :]
    kr = rope(k).astype(BF16)
    for hd in range(ATT_KV_HEADS):
        kh_ref[hd] = kr[:, hd * HEAD_DIM:(hd + 1) * HEAD_DIM]
    vt = v.T
    pad_row = lax.broadcasted_iota(jnp.int32, (V_ROWS - HEAD_DIM, tm), 0)
    ones_rows = jnp.where(pad_row == 0, 1.0, 0.0)
    vt_ref[...] = jnp.concatenate(
        [part for hd in range(ATT_KV_HEADS) for part in (vt[hd * HEAD_DIM:(hd + 1) * HEAD_DIM, :], ones_rows)],
        axis=0).astype(BF16)


def _mod_row(j, n_lat_tiles, tiles_per_batch):
    return jnp.where(j < n_lat_tiles, 1 + j // tiles_per_batch, 0)


def _inproj(xs, g, mod, w, cos_t, sin_t, qg, kg, geom):
    n, d = xs.shape
    tm = geom["tm_mm"]
    nlt, tpb = geom["nl"] // tm, geom["l"] // tm

    def rope_idx(j):
        return jnp.where(j < nlt, 1 + j % tpb, 0)

    return pl.pallas_call(
        _inproj_kernel,
        grid=(n // tm,),
        in_specs=[
            pl.BlockSpec((tm, d), lambda j: (j, 0)),
            pl.BlockSpec((1, d), lambda j: (0, 0)),
            pl.BlockSpec((None, 8, d), lambda j: (_mod_row(j, nlt, tpb), 0, 0)),
            pl.BlockSpec((d, W_COLS), lambda j: (0, 0)),
            pl.BlockSpec((tm, LANE), lambda j: (rope_idx(j), 0)),
            pl.BlockSpec((tm, LANE), lambda j: (rope_idx(j), 0)),
            pl.BlockSpec((8, ATT_Q), lambda j: (0, 0)),
            pl.BlockSpec((8, ATT_KV), lambda j: (0, 0)),
        ],
        out_specs=[
            pl.BlockSpec((tm, P_COLS), lambda j: (j, 0)),
            pl.BlockSpec((ATT_Q, tm), lambda j: (0, j)),
            pl.BlockSpec((ATT_KV_HEADS, tm, HEAD_DIM), lambda j: (0, j, 0)),
            pl.BlockSpec((ATT_KV_HEADS * V_ROWS, tm), lambda j: (0, j)),
        ],
        out_shape=[
            jax.ShapeDtypeStruct((n, P_COLS), F32),
            jax.ShapeDtypeStruct((ATT_Q, n), BF16),
            jax.ShapeDtypeStruct((ATT_KV_HEADS, n, HEAD_DIM), BF16),
            jax.ShapeDtypeStruct((ATT_KV_HEADS * V_ROWS, n), BF16),
        ],
        compiler_params=_cparams(("parallel",)),
        name="inproj",
    )(xs, g, mod, w, cos_t, sin_t, qg, kg)


def _seq_flags(j, tm, geom):
    row0 = j * tm
    is_lat = row0 < geom["nl"]
    pos = jnp.where(is_lat, row0 % geom["l"], (row0 - geom["nl"]) % geom["lc"])
    slen = jnp.where(is_lat, geom["l"], geom["lc"])
    return pos == 0, pos + tm == slen


def _conv3(u, prev_row, next_row, w_ref):
    tm = u.shape[0]
    rid = lax.broadcasted_iota(jnp.int32, u.shape, 0)
    up = jnp.where(rid == 0, prev_row, pltpu.roll(u, 1, 0))
    un = jnp.where(rid == tm - 1, next_row, pltpu.roll(u, tm - 1, 0))
    return w_ref[0:1, :] * up + w_ref[1:2, :] * u + w_ref[2:3, :] * un


def _softplus(x):
    return jnp.maximum(x, 0.0) + jnp.log(1.0 + jnp.exp(-jnp.abs(x)))


def _gdn_tile_perm(j, geom):
    cs, nb = geom["gdn_rows"], geom["gdn_nb"]
    nlt, tpb, tpc = geom["nl"] // cs, geom["l"] // cs, geom["lc"] // cs
    jc = j - nlt
    lat = ((j // tpb // nb) * tpb + j % tpb) * nb + (j // tpb) % nb
    ctx = nlt + ((jc // tpc // nb) * tpc + jc % tpc) * nb + (jc // tpc) % nb
    return jnp.where(j < nlt, lat, ctx)


def _prep_kernel(qkv_ref, qkvp_ref, qkvn_ref, ab_ref, cw_ref, gp_ref, gqkv_ref, gate_ref, *, geom):
    j = pl.program_id(0)
    tm = qkv_ref.shape[0]
    first, last = _seq_flags(j, tm, geom)
    u = qkv_ref[...]
    hp = jnp.where(first, 0.0, qkvp_ref[7:8, :])
    hn = jnp.where(last, 0.0, qkvn_ref[0:1, :])
    s = _silu(_conv3(u, hp, hn, cw_ref))
    g256 = _group_ones(GW)
    qg = s[:, 0:GW]
    kg = s[:, GW:2 * GW]
    qg = qg * lax.rsqrt(_group_sum(qg * qg, g256) + EPS) * (GDN_DK ** -0.5)
    kg = kg * lax.rsqrt(_group_sum(kg * kg, g256) + EPS)
    gqkv_ref[:, 0:GW] = qg
    gqkv_ref[:, GW:2 * GW] = kg
    gqkv_ref[:, 2 * GW:3 * GW] = s[:, 2 * GW:3 * GW]
    ab = ab_ref[...]
    lane = lax.broadcasted_iota(jnp.int32, ab.shape, 1)
    log_a = -jnp.exp(gp_ref[0:1, :]) * _softplus(ab + gp_ref[1:2, :])
    gate_ref[...] = jnp.where(lane < 2 * GDN_HEADS, log_a, _sigmoid(ab))


def _prep(p, conv_a, gparams, geom):
    n = p.shape[0]
    tm = geom["tm_cv"]
    hb = tm // 8
    nblk8 = n // 8
    col = lambda w, c: pl.BlockSpec((tm, w), lambda j: (j, c // w))
    return pl.pallas_call(
        functools.partial(_prep_kernel, geom=geom),
        grid=(n // tm,),
        in_specs=[
            col(A_QKV, C_QKV),
            pl.BlockSpec((8, A_QKV), lambda j: (jnp.maximum(j * hb - 1, 0), 0)),
            pl.BlockSpec((8, A_QKV), lambda j: (jnp.minimum((j + 1) * hb, nblk8 - 1), 0)),
            col(LANE, C_AB),
            pl.BlockSpec((8, A_QKV), lambda j: (0, 0)),
            pl.BlockSpec((8, LANE), lambda j: (0, 0)),
        ],
        out_specs=[
            pl.BlockSpec((tm, A_QKV), lambda j: (_gdn_tile_perm(j, geom), 0)),
            pl.BlockSpec((tm, LANE), lambda j: (_gdn_tile_perm(j, geom), 0)),
        ],
        out_shape=[jax.ShapeDtypeStruct((n, A_QKV), F32), jax.ShapeDtypeStruct((n, LANE), F32)],
        compiler_params=_cparams(("parallel",)),
        name="mixer_prep",
    )(p, p, p, p, conv_a, gparams)


def _head_of_lane(shape):
    return lax.broadcasted_iota(jnp.int32, shape, 1) // 64


def _block_diag(xb):
    hl = _head_of_lane(xb.shape)
    zero = jnp.zeros_like(xb)
    return jnp.concatenate([jnp.where(hl == h, xb, zero) for h in range(GDN_HEADS)], axis=0)


def _dot_exact3(a_bf, x):
    h1 = x.astype(BF16)
    r1 = x - h1.astype(F32)
    h2 = r1.astype(BF16)
    h3 = (r1 - h2.astype(F32)).astype(BF16)
    return _dot(a_bf, h1) + _dot(a_bf, h2) + _dot(a_bf, h3)


def _gdn_local(insts):
    cs = GDN_CHUNK
    hl = _head_of_lane((cs, GW))
    row = lax.broadcasted_iota(jnp.int32, (cs, GW), 0)
    col = lax.broadcasted_iota(jnp.int32, (cs, GW), 1) % 64
    ti = lax.broadcasted_iota(jnp.int32, (cs, cs), 0)
    tj = lax.broadcasted_iota(jnp.int32, (cs, cs), 1)
    same16 = (row // 16) == (col // 16)
    same32 = (row // 32) == (col // 32)
    eye = jnp.where(row == col, 1.0, 0.0)
    nt = (((1,), (1,)), ((), ()))

    def widen(gates, base):
        out = jnp.zeros((cs, GW), F32)
        for h in range(GDN_HEADS):
            out = jnp.where(hl == h, gates[:, base + h:base + h + 1], out)
        return out

    for it in insts:
        rev = it["rev"]
        d = 1 if rev else 0
        it["la_w"] = widen(it["gates"], GDN_HEADS * d)
        it["beta_w"] = widen(it["gates"], 2 * GDN_HEADS + GDN_HEADS * d)
        it["tri"] = jnp.where((tj >= ti) if rev else (tj <= ti), 1.0, 0.0).astype(BF16)
        it["incl"] = (row <= col) if rev else (row >= col)
        it["strict"] = (row < col) if rev else (row > col)
        it["kb"] = it["k"].astype(BF16)
        it["kbd"] = _block_diag(it["kb"])
    for it in insts:
        kq = lax.dot_general(jnp.concatenate([it["kb"], it["q"].astype(BF16)], axis=0), it["kbd"], nt,
                             preferred_element_type=F32)
        it["kk_w"], it["qk_w"] = kq[0:cs, :], kq[cs:2 * cs, :]
    for it in insts:
        it["gcc"] = _dot_exact3(it["tri"], it["la_w"])
    for it in insts:
        rev, la_w, incl = it["rev"], it["la_w"], it["incl"]
        gcr = jnp.sum(jnp.where((row >= col) if rev else (row <= col), la_w, 0.0), axis=0, keepdims=True)
        it["decay"] = jnp.where(incl, jnp.exp(jnp.where(incl, it["gcc"] - gcr, 0.0)), 0.0)
        it["lower"] = jnp.where(it["strict"], it["beta_w"] * it["kk_w"] * it["decay"], 0.0)
        pw = jnp.where(same16, it["lower"], 0.0)
        it["t"] = eye - pw
        it["pwb"] = pw.astype(BF16)
    for it in insts:
        it["pwb"] = _dot(it["pwb"], _block_diag(it["pwb"])).astype(BF16)
    for _ in range(2):
        for it in insts:
            both = _dot(jnp.concatenate([it["pwb"], it["t"].astype(BF16)], axis=0), _block_diag(it["pwb"]))
            it["pwb"] = both[0:cs, :].astype(BF16)
            it["t"] = it["t"] + both[cs:2 * cs, :]
    for it in insts:
        it["t"] = it["t"] + _dot(it["t"].astype(BF16), _block_diag(it["pwb"]))
    for level in range(2):
        for it in insts:
            off = (jnp.where(same32 & jnp.logical_not(same16), it["lower"], 0.0) if level == 0
                   else jnp.where(same32, 0.0, it["lower"]))
            it["tb"] = it["t"].astype(BF16)
            it["tc"] = _dot(it["tb"], _block_diag(off.astype(BF16)))
        for it in insts:
            it["t"] = it["t"] - _dot(it["tc"].astype(BF16), _block_diag(it["tb"]))
    for it in insts:
        it["tb"] = it["t"].astype(BF16)
        it["egc"] = jnp.exp(it["gcc"])
        it["u"] = _dot(it["tb"], _block_diag((it["v"] * it["beta_w"]).astype(BF16)))
    for it in insts:
        it["w"] = _dot(it["tb"], _block_diag((it["k"] * (it["beta_w"] * it["egc"])).astype(BF16)))
    for it in insts:
        gcc = it["gcc"]
        gl = gcc[0:1, :] if it["rev"] else gcc[cs - 1:cs, :]
        it["qkm"] = jnp.where(it["incl"], it["qk_w"] * it["decay"], 0.0).astype(BF16)
        it["q_dec"] = (it["q"] * it["egc"]).astype(BF16)
        it["k_dec"] = (it["k"] * jnp.exp(gl - gcc)).astype(BF16)
        it["a_last"] = jnp.exp(gl)
        it["wb"] = it["w"].astype(BF16)


def _gdn_recurrence(group):
    r2 = lax.broadcasted_iota(jnp.int32, (GW, GW), 0) // 64
    c2 = lax.broadcasted_iota(jnp.int32, (GW, GW), 1) // 64
    tn = (((0,), (0,)), ((), ()))
    for it in group:
        it["s"] = it["s_ref"][...]
        it["sb"] = it["s"].astype(BF16)
    for it in group:
        it["wq_s"] = _dot(jnp.concatenate([it["wb"], it["q_dec"]], axis=0), it["sb"])
    for it in group:
        it["vb"] = (it["u"] - it["wq_s"][0:GDN_CHUNK, :]).astype(BF16)
    for it in group:
        it["o"] = it["wq_s"][GDN_CHUNK:2 * GDN_CHUNK, :] + _dot(it["qkm"], _block_diag(it["vb"]))
    for it in group:
        upd = lax.dot_general(it["k_dec"], it["vb"], tn, preferred_element_type=F32)
        it["s_ref"][...] = it["s"] * it["a_last"] + jnp.where(r2 == c2, upd, 0.0)
    for it in group:
        it["o_ref"][it["rows"], :] = it["o"]


def _gdn_kernel(xf_ref, gf_ref, xb_ref, gb_ref, of_ref, ob_ref, *s_refs, seq_rows):
    @pl.when(pl.program_id(1) == 0)
    def _():
        for s_ref in s_refs:
            s_ref[...] = jnp.zeros_like(s_ref)

    n_chunks = seq_rows // GDN_CHUNK
    scans = []
    for slot in range(xf_ref.shape[0] // seq_rows):
        for x_ref, g_ref, o_ref, rev in ((xf_ref, gf_ref, of_ref, False), (xb_ref, gb_ref, ob_ref, True)):
            seq = []
            for ci in (reversed(range(n_chunks)) if rev else range(n_chunks)):
                r0 = slot * seq_rows + ci * GDN_CHUNK
                rows = slice(r0, r0 + GDN_CHUNK)
                seq.append(dict(rev=rev, rows=rows, o_ref=o_ref, s_ref=s_refs[len(scans)], gates=g_ref[rows, :],
                                q=x_ref[rows, 0:GW], k=x_ref[rows, GW:2 * GW], v=x_ref[rows, 2 * GW:3 * GW]))
            scans.append(seq)
    _gdn_local([it for group in zip(*scans) for it in group])
    for group in zip(*scans):
        _gdn_recurrence(list(group))


def _gdn(gqkv, gates, geom):
    n = gqkv.shape[0]
    cs, nb = geom["gdn_rows"], geom["gdn_nb"]
    nl_c, nc_c = geom["l"] // cs, geom["lc"] // cs
    lat_blocks = geom["nl"] // (cs * nb)
    steps = nl_c + nc_c

    def fwd(g, s):
        return jnp.where(s < nc_c, lat_blocks + g * nc_c + s, g * nl_c + s - nc_c)

    def bwd(g, s):
        return jnp.where(s < nc_c, lat_blocks + g * nc_c + (nc_c - 1 - s), g * nl_c + (nl_c - 1 - (s - nc_c)))

    return pl.pallas_call(
        functools.partial(_gdn_kernel, seq_rows=cs),
        grid=(geom["b"] // nb, steps),
        in_specs=[
            pl.BlockSpec((nb * cs, A_QKV), lambda g, s: (fwd(g, s), 0)),
            pl.BlockSpec((nb * cs, LANE), lambda g, s: (fwd(g, s), 0)),
            pl.BlockSpec((nb * cs, A_QKV), lambda g, s: (bwd(g, s), 0)),
            pl.BlockSpec((nb * cs, LANE), lambda g, s: (bwd(g, s), 0)),
        ],
        out_specs=[
            pl.BlockSpec((nb * cs, GW), lambda g, s: (fwd(g, s), 0)),
            pl.BlockSpec((nb * cs, GW), lambda g, s: (bwd(g, s), 0)),
        ],
        out_shape=[jax.ShapeDtypeStruct((n, GW), F32), jax.ShapeDtypeStruct((n, GW), F32)],
        scratch_shapes=[pltpu.VMEM((GW, GW), F32)] * (2 * nb),
        compiler_params=_cparams(("parallel", "arbitrary")),
        name="gdn_scan",
    )(gqkv, gates, gqkv, gates)


def _attn_kernel(*refs, lat_queries):
    if lat_queries:
        qt_ref, kc_ref, kl_ref, vc_ref, vl_ref, o_ref, acc_ref, st_ref = refs
    else:
        qt_ref, kc_ref, vc_ref, o_ref, acc_ref = refs
    tq = qt_ref.shape[1]
    acc_ref[...] = jnp.zeros(acc_ref.shape, F32)
    qrows = [slice(g * HEAD_DIM, (g + 1) * HEAD_DIM) for g in range(ATT_GROUPS)]
    arows = [slice(g * V_ROWS, (g + 1) * V_ROWS) for g in range(ATT_GROUPS)]

    def scores(k_ref, off):
        kc = k_ref[pl.ds(off, TK), :]
        return [_dot(kc, qt_ref[r, :]) for r in qrows]

    def update(v_ref, off, sts, ms):
        vc = v_ref[:, pl.ds(off, TK)]
        new_ms, ps, alphas = [], [], []
        for st, m_old in zip(sts, ms):
            m_new = jnp.maximum(m_old, jnp.max(st, axis=0, keepdims=True))
            ps.append(jnp.exp2(st - m_new).astype(BF16))
            alphas.append(jnp.exp2(m_old - m_new))
            new_ms.append(m_new)
        pvs = [_dot(vc, p) for p in ps]
        for r, alpha, pv in zip(arows, alphas, pvs):
            acc_ref[r, :] = alpha * acc_ref[r, :] + pv
        return tuple(new_ms)

    ms = (jnp.full((1, tq), -jnp.inf, F32),) * ATT_GROUPS
    n_ctx = kc_ref.shape[0] // TK
    cur = scores(kc_ref, 0)
    for c in range(1, n_ctx):
        nxt = scores(kc_ref, c * TK)
        ms = update(vc_ref, (c - 1) * TK, cur, ms)
        cur = nxt
    if lat_queries:
        n_lat = kl_ref.shape[0] // TK
        nxt = scores(kl_ref, 0)
        ms = update(vc_ref, (n_ctx - 1) * TK, cur, ms)
        for g in range(ATT_GROUPS):
            st_ref[g] = nxt[g]
        per_iter = math.gcd(ATT_CHUNKS_PER_ITER, n_lat)

        def body(i, ms):
            cur = [st_ref[g] for g in range(ATT_GROUPS)]
            for c in range(per_iter):
                off = pl.multiple_of((i * per_iter + c) * TK, TK)
                off_next = pl.multiple_of(jnp.minimum(off + TK, (n_lat - 1) * TK), TK)
                nxt = scores(kl_ref, off_next)
                ms = update(vl_ref, off, cur, ms)
                cur = nxt
            for g in range(ATT_GROUPS):
                st_ref[g] = cur[g]
            return ms

        ms = lax.fori_loop(0, n_lat // per_iter, body, ms)
    else:
        ms = update(vc_ref, (n_ctx - 1) * TK, cur, ms)

    out = [acc_ref[r, :][0:HEAD_DIM, :] / acc_ref[r, :][HEAD_DIM:HEAD_DIM + 1, :] for r in arows]
    o_ref[...] = jnp.concatenate(out, axis=0).T.astype(o_ref.dtype)


def _attention(qt, kh, vt, lat_queries, geom):
    b, l, lc, nl = geom["b"], geom["l"], geom["lc"], geom["nl"]
    tq = math.gcd(TQ_LAT, l) if lat_queries else geom["tq"]
    nq = (l if lat_queries else lc) // tq
    qbase = 0 if lat_queries else nl // tq
    gw = ATT_GROUPS * HEAD_DIM
    k_ctx = pl.BlockSpec((None, lc, HEAD_DIM), lambda bb, h, i: (h, nl // lc + bb, 0))
    v_ctx = pl.BlockSpec((V_ROWS, lc), lambda bb, h, i: (h, nl // lc + bb))
    k_lat = pl.BlockSpec((None, l, HEAD_DIM), lambda bb, h, i: (h, bb, 0))
    v_lat = pl.BlockSpec((V_ROWS, l), lambda bb, h, i: (h, bb))
    scratch = [pltpu.VMEM((ATT_GROUPS * V_ROWS, tq), F32)]
    if lat_queries:
        kv_specs, kv_args = [k_ctx, k_lat, v_ctx, v_lat], (kh, kh, vt, vt)
        scratch.append(pltpu.VMEM((ATT_GROUPS, TK, tq), F32))
    else:
        kv_specs, kv_args = [k_ctx, v_ctx], (kh, vt)
    return pl.pallas_call(
        functools.partial(_attn_kernel, lat_queries=lat_queries),
        grid=(b, ATT_KV_HEADS, nq),
        in_specs=[pl.BlockSpec((gw, tq), lambda bb, h, i: (h, qbase + bb * nq + i))] + kv_specs,
        out_specs=pl.BlockSpec((tq, gw), lambda bb, h, i: (bb * nq + i, h)),
        out_shape=jax.ShapeDtypeStruct((b * nq * tq, ATT_Q), BF16),
        scratch_shapes=scratch,
        compiler_params=_cparams(("parallel", "parallel", "arbitrary")),
        name="attention_lat" if lat_queries else "attention_ctx",
    )(qt, *kv_args)


def _outproj_kernel(x_ref, of_ref, ob_ref, z_ref, ybl_ref, ybc_ref, cb_ref, cc_ref, ch_ref,
                    ccp_ref, chp_ref, ccn_ref, chn_ref, mod_ref, w_ref, gg_ref, cw_ref,
                    o_ref, *, geom):
    j = pl.program_id(0)
    tm = x_ref.shape[0]
    first, last = _seq_flags(j, tm, geom)
    o = of_ref[...] + ob_ref[...]
    ms = _group_sum(o * o, _group_ones(GW)) * (1.0 / GDN_DV)
    ya = o * lax.rsqrt(ms + EPS) * gg_ref[0:1, :] * _silu(z_ref[...])
    yb = jnp.where(j * tm < geom["nl"], ybl_ref[...], ybc_ref[...])
    u = cc_ref[...] * ch_ref[...]
    hp = jnp.where(first, 0.0, ccp_ref[7:8, :] * chp_ref[7:8, :])
    hn = jnp.where(last, 0.0, ccn_ref[0:1, :] * chn_ref[0:1, :])
    yc = cb_ref[...] * _conv3(u, hp, hn, cw_ref)
    acc = _dot(ya.astype(BF16), w_ref[0:A_Z, :])
    acc += _dot(yb, w_ref[A_Z:A_Z + ATT_Q, :])
    acc += _dot(yc.astype(BF16), w_ref[A_Z + ATT_Q:, :])
    o_ref[...] = x_ref[...] + mod_ref[2:3, :] * acc


def _outproj(xs, of, ob, p, yb_lat, yb_ctx, mod, w, gg, conv_c, n_rows, geom):
    d = xs.shape[1]
    tm = geom["tm_cv"]
    nlt, tpb = geom["nl"] // tm, geom["l"] // tm
    nct = yb_ctx.shape[0] // tm
    hb = tm // 8
    nblk8 = xs.shape[0] // 8
    col = lambda w_, c: pl.BlockSpec((tm, w_), lambda j: (j, c // w_))
    prev = lambda c: pl.BlockSpec((8, SC_CH), lambda j: (jnp.maximum(j * hb - 1, 0), c // SC_CH))
    nxt = lambda c: pl.BlockSpec((8, SC_CH), lambda j: (jnp.minimum((j + 1) * hb, nblk8 - 1), c // SC_CH))
    gdn_o = pl.BlockSpec((tm, GW), lambda j: (_gdn_tile_perm(j, geom), 0))
    return pl.pallas_call(
        functools.partial(_outproj_kernel, geom=geom),
        grid=(n_rows // tm,),
        in_specs=[
            pl.BlockSpec((tm, d), lambda j: (j, 0)),
            gdn_o, gdn_o,
            col(A_Z, C_Z),
            pl.BlockSpec((tm, ATT_Q), lambda j: (jnp.minimum(j, nlt - 1), 0)),
            pl.BlockSpec((tm, ATT_Q), lambda j: (jnp.clip(j - nlt, 0, nct - 1), 0)),
            col(SC_CH, C_CB), col(SC_CH, C_CC), col(SC_CH, C_CH),
            prev(C_CC), prev(C_CH), nxt(C_CC), nxt(C_CH),
            pl.BlockSpec((None, 8, d), lambda j: (_mod_row(j, nlt, tpb), 0, 0)),
            pl.BlockSpec((d, d), lambda j: (0, 0)),
            pl.BlockSpec((8, GW), lambda j: (0, 0)),
            pl.BlockSpec((8, SC_CH), lambda j: (0, 0)),
        ],
        out_specs=pl.BlockSpec((tm, d), lambda j: (j, 0)),
        out_shape=jax.ShapeDtypeStruct((n_rows, d), F32),
        compiler_params=_cparams(("parallel",)),
        name="outproj",
    )(xs, of, ob, p, yb_lat, yb_ctx, p, p, p, p, p, p, p, mod, w, gg, conv_c)


def _expert_of_step(j, e, ne):
    return jnp.where(j % 2 == 0, e, ne - 1 - e)


def _experts_kernel(*refs, routed, final_norm):
    refs = list(refs)
    x_ref, g_ref, mod_ref = refs[0:3]
    del refs[0:3]
    r_ref = refs.pop(0) if routed else None
    gf_ref = refs.pop(0) if final_norm else None
    wgu_ref, wd_ref, o_ref, h_ref, acc_ref, a_ref = refs[0:6]
    comb_ref = refs[6] if routed else None
    e = pl.program_id(1)
    ne = pl.num_programs(1) - 1
    ffe = wd_ref.shape[0]

    def gate_up():
        gu = _dot(h_ref[...], wgu_ref[...])
        a_ref[...] = (_silu(gu[:, :ffe]) * gu[:, ffe:]).astype(BF16)

    def down(a_prev):
        y = _dot(a_prev, wd_ref[...])
        if routed:
            comb = comb_ref[...]
            lane = lax.broadcasted_iota(jnp.int32, comb.shape, 1)
            prev_expert = _expert_of_step(pl.program_id(0), e - 1, ne)
            y = jnp.sum(jnp.where(lane == prev_expert, comb, 0.0), axis=-1, keepdims=True) * y
        acc_ref[...] += y

    @pl.when(e == 0)
    def _():
        h = _norm_mod(x_ref[...], g_ref[...], mod_ref[3:4, :], mod_ref[4:5, :])
        h_ref[...] = h.astype(BF16)
        acc_ref[...] = jnp.zeros_like(acc_ref)
        if routed:
            logits = _dot_split3(h, r_ref[...])
            lane = lax.broadcasted_iota(jnp.int32, logits.shape, 1).astype(F32)
            valid = lane < N_EXPERTS
            logits = jnp.where(valid, logits, -jnp.inf)
            ex = jnp.exp(logits - jnp.max(logits, axis=-1, keepdims=True))
            probs = jnp.where(valid, ex / jnp.sum(ex, axis=-1, keepdims=True), -1.0)
            p1 = jnp.max(probs, axis=-1, keepdims=True)
            i1 = jnp.min(jnp.where(probs == p1, lane, float(LANE)), axis=-1, keepdims=True)
            rest = jnp.where(lane == i1, -1.0, probs)
            p2 = jnp.max(rest, axis=-1, keepdims=True)
            i2 = jnp.min(jnp.where(rest == p2, lane, float(LANE)), axis=-1, keepdims=True)
            den = p1 + p2
            comb_ref[...] = jnp.where(lane == i1, p1 / den, 0.0) + jnp.where(lane == i2, p2 / den, 0.0)
        gate_up()

    @pl.when(jnp.logical_and(e > 0, e < ne))
    def _():
        a_prev = a_ref[...]
        down(a_prev)
        gate_up()

    @pl.when(e == ne)
    def _():
        down(a_ref[...])
        y = x_ref[...] + mod_ref[5:6, :] * acc_ref[...]
        if final_norm:
            y = y * lax.rsqrt(jnp.mean(y * y, axis=-1, keepdims=True) + EPS) * gf_ref[...]
        o_ref[...] = y


def _experts(xs, g, mod, router, wgu, wd, n_rows, geom, final_g=None):
    d = xs.shape[1]
    ne, ffe = wd.shape[0], wd.shape[1]
    tm = geom["tm_mm"]
    nlt, tpb = geom["nl"] // tm, geom["l"] // tm
    routed = router is not None
    in_specs = [
        pl.BlockSpec((tm, d), lambda j, e: (j, 0)),
        pl.BlockSpec((1, d), lambda j, e: (0, 0)),
        pl.BlockSpec((None, 8, d), lambda j, e: (_mod_row(j, nlt, tpb), 0, 0)),
    ]
    scratch = [pltpu.VMEM((tm, d), BF16), pltpu.VMEM((tm, d), F32), pltpu.VMEM((tm, ffe), BF16)]
    args = [xs, g, mod]
    if routed:
        in_specs.append(pl.BlockSpec((d, LANE), lambda j, e: (0, 0)))
        scratch.append(pltpu.VMEM((tm, LANE), F32))
        args.append(router)
    if final_g is not None:
        in_specs.append(pl.BlockSpec((1, d), lambda j, e: (0, 0)))
        args.append(final_g)
    in_specs += [
        pl.BlockSpec((None, d, 2 * ffe), lambda j, e: (_expert_of_step(j, jnp.minimum(e, ne - 1), ne), 0, 0)),
        pl.BlockSpec((None, ffe, d), lambda j, e: (_expert_of_step(j, jnp.maximum(e - 1, 0), ne), 0, 0)),
    ]
    return pl.pallas_call(
        functools.partial(_experts_kernel, routed=routed, final_norm=final_g is not None),
        grid=(n_rows // tm, ne + 1),
        in_specs=in_specs,
        out_specs=pl.BlockSpec((tm, d), lambda j, e: (j, 0)),
        out_shape=jax.ShapeDtypeStruct((n_rows, d), F32),
        scratch_shapes=scratch,
        compiler_params=_cparams(("parallel", "arbitrary")),
        name="moe" if routed else "ffn",
    )(*args, wgu, wd)


def _rope_tables(l, tm):
    quarter = HEAD_DIM // 4
    inv_freq = ROPE_THETA ** (-jnp.arange(quarter, dtype=F32) / quarter)
    t = jnp.arange(l, dtype=jnp.int32)
    ang_r = (t // GRID_W).astype(F32)[:, None] * inv_freq[None, :]
    ang_c = (t % GRID_W).astype(F32)[:, None] * inv_freq[None, :]
    cos = jnp.concatenate([jnp.cos(ang_r)] * 2 + [jnp.cos(ang_c)] * 2, axis=1)
    sin = jnp.concatenate([-jnp.sin(ang_r), jnp.sin(ang_r), -jnp.sin(ang_c), jnp.sin(ang_c)], axis=1)
    cos = jnp.tile(cos, (1, LANE // HEAD_DIM))
    sin = jnp.tile(sin, (1, LANE // HEAD_DIM))
    cos = jnp.concatenate([jnp.ones((tm, LANE), F32), cos], axis=0)
    sin = jnp.concatenate([jnp.zeros((tm, LANE), F32), sin], axis=0)
    return cos, sin


def _pad_rows(a, rows=8):
    return jnp.pad(a, ((0, rows - a.shape[0]), (0, 0)))


def _permute_w_in(w):
    o = 0
    parts = {}
    for name, size in (("qkv", A_QKV), ("z", A_Z), ("a", 2 * GDN_HEADS), ("b", 2 * GDN_HEADS),
                       ("q", ATT_Q), ("k", ATT_KV), ("v", ATT_KV), ("cb", SC_CH), ("cc", SC_CH), ("ch", SC_CH)):
        parts[name] = w[:, o:o + size]
        o += size
    pad = jnp.zeros((w.shape[0], P_COLS - C_AB - 4 * GDN_HEADS), w.dtype)
    cols = [parts[k] for k in ("qkv", "z", "cb", "cc", "ch", "a", "b")] + [pad] + [parts[k] for k in ("q", "k", "v")]
    return jnp.concatenate(cols, axis=1).astype(BF16)


def _split_ffn(wg, wu, wd):
    d, ff = wg.shape
    ffe = ff // FFN_SPLIT
    wgu = jnp.concatenate([wg.reshape(d, FFN_SPLIT, ffe), wu.reshape(d, FFN_SPLIT, ffe)], axis=-1)
    return jnp.swapaxes(wgu, 0, 1).astype(BF16), wd.reshape(FFN_SPLIT, ffe, d).astype(BF16)


def kernel(x, c, ctx, c_ctx, w_mod, b_mod, norm1, norm2, w_in, conv_a, a_log, dt_bias, gdn_norm,
           q_norm, k_norm, conv_c, w_out, ffn_w_gate, ffn_w_up, ffn_w_down, router,
           moe_w_gate, moe_w_up, moe_w_down, norm_f):
    b, l, d = x.shape
    lc = ctx.shape[1]
    depth = w_mod.shape[0]
    nl, nc = b * l, b * lc
    tm_mm = math.gcd(TM_MM, math.gcd(l, nc))
    tm_cv = math.gcd(TM_CV, math.gcd(l, lc))
    tq = math.gcd(TQ, math.gcd(l, lc))
    gdn_rows = math.gcd(GDN_CHUNKS_PER_STEP * GDN_CHUNK, math.gcd(l, lc))
    gdn_nb = GDN_BATCH_PER_STEP if b % GDN_BATCH_PER_STEP == 0 else 1
    assert l % GRID_W == 0 and l % TK == 0 and lc % TK == 0 and nl % lc == 0 and gdn_rows == tm_cv
    geom = dict(b=b, l=l, lc=lc, nl=nl, nc=nc, tm_mm=tm_mm, tm_cv=tm_cv, tq=tq,
                gdn_rows=gdn_rows, gdn_nb=gdn_nb)
    n = nl + nc

    mod = _modulation(c, c_ctx, w_mod, b_mod)
    cos_t, sin_t = _rope_tables(l, tm_mm)
    xs = jnp.concatenate([x.reshape(nl, d), ctx.reshape(nc, d)], axis=0)

    for li in range(depth):
        last = li == depth - 1
        rows_out = nl if last else n
        w_in_p = _permute_w_in(w_in[li])
        gparams = _pad_rows(jnp.stack([jnp.pad(a_log[li].reshape(-1), (0, LANE - 2 * GDN_HEADS)),
                                       jnp.pad(dt_bias[li].reshape(-1), (0, LANE - 2 * GDN_HEADS))]))
        p, qt, kh, vt = _inproj(xs, norm1[li][None, :], mod[li], w_in_p, cos_t, sin_t,
                                _pad_rows(jnp.tile(q_norm[li], ATT_HEADS)[None, :]),
                                _pad_rows(jnp.tile(k_norm[li], ATT_KV_HEADS)[None, :]), geom)
        gqkv, gates = _prep(p, _pad_rows(conv_a[li]), gparams, geom)
        of, ob = _gdn(gqkv, gates, geom)
        yb_lat = _attention(qt, kh, vt, True, geom)
        yb_ctx = yb_lat if last else _attention(qt, kh, vt, False, geom)
        xs = _outproj(xs, of, ob, p, yb_lat, yb_ctx, mod[li], w_out[li].astype(BF16),
                      _pad_rows(jnp.tile(gdn_norm[li], GDN_HEADS)[None, :]), _pad_rows(conv_c[li]),
                      rows_out, geom)
        i = li // 2
        final_g = norm_f[None, :] if last else None
        if li % 2 == 0:
            wgu, wd = _split_ffn(ffn_w_gate[i], ffn_w_up[i], ffn_w_down[i])
            xs = _experts(xs, norm2[li][None, :], mod[li], None, wgu, wd, rows_out, geom, final_g)
        else:
            wgu = jnp.concatenate([moe_w_gate[i], moe_w_up[i]], axis=-1).astype(BF16)
            xs = _experts(xs, norm2[li][None, :], mod[li],
                          jnp.pad(router[i], ((0, 0), (0, LANE - N_EXPERTS))), wgu,
                          moe_w_down[i].astype(BF16), rows_out, geom, final_g)
    return xs.reshape(b, l, d)
```

```python
import functools
import math

import jax
import jax.numpy as jnp
from jax import lax
from jax.experimental import pallas as pl
from jax.experimental.pallas import tpu as pltpu

F32 = jnp.float32
BF16 = jnp.bfloat16
EPS = 1e-6

GRID_W = 64
HEAD_DIM = 64
GDN_HEADS = 4
GDN_DK = 64
GDN_DV = 64
GDN_CHUNK = 64
ATT_HEADS = 8
ATT_KV_HEADS = 2
ATT_GROUPS = ATT_HEADS // ATT_KV_HEADS
ROPE_THETA = 10000.0
SC_CH = 256
N_EXPERTS = 8
TOP_K = 2
A_QKV = GDN_HEADS * (2 * GDN_DK + GDN_DV)
A_Z = GDN_HEADS * GDN_DV
GW = GDN_HEADS * GDN_DK
ATT_Q = ATT_HEADS * HEAD_DIM
ATT_KV = ATT_KV_HEADS * HEAD_DIM

C_QKV = 0
C_Z = 768
C_CB = 1024
C_CC = 1280
C_CH = 1536
C_AB = 1792
P_COLS = 1920
C_Q = P_COLS
C_K = C_Q + ATT_Q
C_V = C_K + ATT_KV
W_COLS = C_V + ATT_KV

VMEM_LIMIT = 56 * 1024 * 1024
LANE = 128

TM_MM = 512
TM_EXPERTS = 1024
TM_CV = 256
TQ = 256
TQ_LAT = 512
TK = 256
V_ROWS = HEAD_DIM + 16
ATT_CHUNKS_PER_ITER = 8
GDN_CHUNKS_PER_STEP = 4
GDN_BATCH_PER_STEP = 4
FFN_SPLIT = 2


def _cparams(sem):
    return pltpu.CompilerParams(dimension_semantics=sem, vmem_limit_bytes=VMEM_LIMIT)


def _sigmoid(x):
    return 1.0 / (1.0 + jnp.exp(-x))


def _silu(x):
    return x * _sigmoid(x)


def _dot(a, b):
    return jnp.dot(a, b, preferred_element_type=F32)


def _dot_split3(a, b):
    ah = a.astype(BF16)
    al = (a - ah.astype(F32)).astype(BF16)
    bh = b.astype(BF16)
    bl = (b - bh.astype(F32)).astype(BF16)
    return _dot(ah, bh) + _dot(al, bh) + _dot(ah, bl)


def _group_sum(x2, gmat):
    hi = x2.astype(BF16)
    lo = (x2 - hi.astype(F32)).astype(BF16)
    return _dot(hi, gmat) + _dot(lo, gmat)


def _group_ones(width):
    r = lax.broadcasted_iota(jnp.int32, (width, width), 0) // 64
    c = lax.broadcasted_iota(jnp.int32, (width, width), 1) // 64
    return jnp.where(r == c, 1.0, 0.0).astype(BF16)


def _mod_kernel(s_ref, w_ref, b_ref, o_ref):
    s = _silu(s_ref[...])
    o_ref[...] = jnp.dot(s, w_ref[...], precision=lax.Precision.HIGHEST,
                         preferred_element_type=F32) + b_ref[...]


def _modulation(c, c_ctx, w_mod, b_mod):
    depth, d, d6 = w_mod.shape
    rows = c.shape[0] + 1
    rpad = -(-rows // 8) * 8
    s = jnp.concatenate([c_ctx[None, :], c, jnp.zeros((rpad - rows, d), F32)], axis=0)
    tn = 1536
    out = pl.pallas_call(
        _mod_kernel,
        grid=(depth, d6 // tn),
        in_specs=[
            pl.BlockSpec((rpad, d), lambda l, j: (0, 0)),
            pl.BlockSpec((None, d, tn), lambda l, j: (l, 0, j)),
            pl.BlockSpec((None, 1, tn), lambda l, j: (l, 0, j)),
        ],
        out_specs=pl.BlockSpec((None, rpad, tn), lambda l, j: (l, 0, j)),
        out_shape=jax.ShapeDtypeStruct((depth, rpad, d6), F32),
        compiler_params=_cparams(("parallel", "parallel")),
        name="modulation",
    )(s, w_mod, b_mod.reshape(depth, 1, d6))
    m = out[:, :rows].reshape(depth, rows, 6, d)
    return jnp.pad(m, ((0, 0), (0, 0), (0, 2), (0, 0)))


def _norm_mod(x, g, shift, scale):
    ms = jnp.mean(x * x, axis=-1, keepdims=True)
    y = x * lax.rsqrt(ms + EPS) * g
    return y * (1.0 + scale) + shift


def _inproj_kernel(x_ref, g_ref, mod_ref, w_ref, cos_ref, sin_ref, qg_ref, kg_ref,
                   o_ref, qt_ref, kh_ref, vt_ref):
    tm = x_ref.shape[0]
    h = _norm_mod(x_ref[...], g_ref[...], mod_ref[0:1, :], mod_ref[1:2, :]).astype(BF16)
    qkv = _dot(h, w_ref[:, C_Q:W_COLS])
    for c0 in range(0, P_COLS, 256):
        c1 = min(c0 + 256, P_COLS)
        o_ref[:, c0:c1] = _dot(h, w_ref[:, c0:c1])
    q = qkv[:, 0:ATT_Q]
    k = qkv[:, ATT_Q:ATT_Q + ATT_KV]
    v = qkv[:, ATT_Q + ATT_KV:]
    cos = cos_ref[...]
    sin = sin_ref[...]
    lane128 = lax.broadcasted_iota(jnp.int32, cos.shape, 1)
    first_half = (lane128 % 32) < 16

    def rope(xb):
        r_lo = pltpu.roll(xb, 16, 1)
        r_hi = pltpu.roll(xb, 112, 1)
        return xb * cos + jnp.where(first_half, r_hi, r_lo) * sin

    q = q * lax.rsqrt(_group_sum(q * q, _group_ones(ATT_Q)) * (1.0 / HEAD_DIM) + EPS) * qg_ref[0:1, :]
    qr = jnp.concatenate([rope(q[:, c:c + LANE]) for c in range(0, ATT_Q, LANE)], axis=1)
    qt_ref[...] = (qr * (HEAD_DIM ** -0.5 * math.log2(math.e))).T.astype(BF16)
    k = k * lax.rsqrt(_group_sum(k * k, _group_ones(ATT_KV)) * (1.0 / HEAD_DIM) + EPS) * kg_ref[0:1, :]
    kr = rope(k).astype(BF16)
    for hd in range(ATT_KV_HEADS):
        kh_ref[hd] = kr[:, hd * HEAD_DIM:(hd + 1) * HEAD_DIM]
    vt = v.T
    pad_row = lax.broadcasted_iota(jnp.int32, (V_ROWS - HEAD_DIM, tm), 0)
    ones_rows = jnp.where(pad_row == 0, 1.0, 0.0)
    vt_ref[...] = jnp.concatenate(
        [part for hd in range(ATT_KV_HEADS) for part in (vt[hd * HEAD_DIM:(hd + 1) * HEAD_DIM, :], ones_rows)],
        axis=0).astype(BF16)


def _mod_row(j, n_lat_tiles, tiles_per_batch):
    return jnp.where(j < n_lat_tiles, 1 + j // tiles_per_batch, 0)


def _inproj(xs, g, mod, w, cos_t, sin_t, qg, kg, geom):
    n, d = xs.shape
    tm = geom["tm_mm"]
    nlt, tpb = geom["nl"] // tm, geom["l"] // tm

    def rope_idx(j):
        return jnp.where(j < nlt, 1 + j % tpb, 0)

    return pl.pallas_call(
        _inproj_kernel,
        grid=(n // tm,),
        in_specs=[
            pl.BlockSpec((tm, d), lambda j: (j, 0)),
            pl.BlockSpec((1, d), lambda j: (0, 0)),
            pl.BlockSpec((None, 8, d), lambda j: (_mod_row(j, nlt, tpb), 0, 0)),
            pl.BlockSpec((d, W_COLS), lambda j: (0, 0)),
            pl.BlockSpec((tm, LANE), lambda j: (rope_idx(j), 0)),
            pl.BlockSpec((tm, LANE), lambda j: (rope_idx(j), 0)),
            pl.BlockSpec((8, ATT_Q), lambda j: (0, 0)),
            pl.BlockSpec((8, ATT_KV), lambda j: (0, 0)),
        ],
        out_specs=[
            pl.BlockSpec((tm, P_COLS), lambda j: (j, 0)),
            pl.BlockSpec((ATT_Q, tm), lambda j: (0, j)),
            pl.BlockSpec((ATT_KV_HEADS, tm, HEAD_DIM), lambda j: (0, j, 0)),
            pl.BlockSpec((ATT_KV_HEADS * V_ROWS, tm), lambda j: (0, j)),
        ],
        out_shape=[
            jax.ShapeDtypeStruct((n, P_COLS), F32),
            jax.ShapeDtypeStruct((ATT_Q, n), BF16),
            jax.ShapeDtypeStruct((ATT_KV_HEADS, n, HEAD_DIM), BF16),
            jax.ShapeDtypeStruct((ATT_KV_HEADS * V_ROWS, n), BF16),
        ],
        compiler_params=_cparams(("parallel",)),
        name="inproj",
    )(xs, g, mod, w, cos_t, sin_t, qg, kg)


def _seq_flags(j, tm, geom):
    row0 = j * tm
    is_lat = row0 < geom["nl"]
    pos = jnp.where(is_lat, row0 % geom["l"], (row0 - geom["nl"]) % geom["lc"])
    slen = jnp.where(is_lat, geom["l"], geom["lc"])
    return pos == 0, pos + tm == slen


def _conv3(u, prev_row, next_row, w_ref):
    tm = u.shape[0]
    rid = lax.broadcasted_iota(jnp.int32, u.shape, 0)
    up = jnp.where(rid == 0, prev_row, pltpu.roll(u, 1, 0))
    un = jnp.where(rid == tm - 1, next_row, pltpu.roll(u, tm - 1, 0))
    return w_ref[0:1, :] * up + w_ref[1:2, :] * u + w_ref[2:3, :] * un


def _softplus(x):
    return jnp.maximum(x, 0.0) + jnp.log(1.0 + jnp.exp(-jnp.abs(x)))


def _gdn_tile_perm(j, geom):
    cs, nb = geom["gdn_rows"], geom["gdn_nb"]
    nlt, tpb, tpc = geom["nl"] // cs, geom["l"] // cs, geom["lc"] // cs
    jc = j - nlt
    lat = ((j // tpb // nb) * tpb + j % tpb) * nb + (j // tpb) % nb
    ctx = nlt + ((jc // tpc // nb) * tpc + jc % tpc) * nb + (jc // tpc) % nb
    return jnp.where(j < nlt, lat, ctx)


def _prep_kernel(qkv_ref, qkvp_ref, qkvn_ref, ab_ref, cw_ref, gp_ref, gqkv_ref, gate_ref, *, geom):
    j = pl.program_id(0)
    tm = qkv_ref.shape[0]
    first, last = _seq_flags(j, tm, geom)
    u = qkv_ref[...]
    hp = jnp.where(first, 0.0, qkvp_ref[7:8, :])
    hn = jnp.where(last, 0.0, qkvn_ref[0:1, :])
    s = _silu(_conv3(u, hp, hn, cw_ref))
    g256 = _group_ones(GW)
    qg = s[:, 0:GW]
    kg = s[:, GW:2 * GW]
    qg = qg * lax.rsqrt(_group_sum(qg * qg, g256) + EPS) * (GDN_DK ** -0.5)
    kg = kg * lax.rsqrt(_group_sum(kg * kg, g256) + EPS)
    gqkv_ref[:, 0:GW] = qg
    gqkv_ref[:, GW:2 * GW] = kg
    gqkv_ref[:, 2 * GW:3 * GW] = s[:, 2 * GW:3 * GW]
    ab = ab_ref[...]
    lane = lax.broadcasted_iota(jnp.int32, ab.shape, 1)
    log_a = -jnp.exp(gp_ref[0:1, :]) * _softplus(ab + gp_ref[1:2, :])
    gate_ref[...] = jnp.where(lane < 2 * GDN_HEADS, log_a, _sigmoid(ab))


def _prep(p, conv_a, gparams, geom):
    n = p.shape[0]
    tm = geom["tm_cv"]
    hb = tm // 8
    nblk8 = n // 8
    col = lambda w, c: pl.BlockSpec((tm, w), lambda j: (j, c // w))
    return pl.pallas_call(
        functools.partial(_prep_kernel, geom=geom),
        grid=(n // tm,),
        in_specs=[
            col(A_QKV, C_QKV),
            pl.BlockSpec((8, A_QKV), lambda j: (jnp.maximum(j * hb - 1, 0), 0)),
            pl.BlockSpec((8, A_QKV), lambda j: (jnp.minimum((j + 1) * hb, nblk8 - 1), 0)),
            col(LANE, C_AB),
            pl.BlockSpec((8, A_QKV), lambda j: (0, 0)),
            pl.BlockSpec((8, LANE), lambda j: (0, 0)),
        ],
        out_specs=[
            pl.BlockSpec((tm, A_QKV), lambda j: (_gdn_tile_perm(j, geom), 0)),
            pl.BlockSpec((tm, LANE), lambda j: (_gdn_tile_perm(j, geom), 0)),
        ],
        out_shape=[jax.ShapeDtypeStruct((n, A_QKV), F32), jax.ShapeDtypeStruct((n, LANE), F32)],
        compiler_params=_cparams(("parallel",)),
        name="mixer_prep",
    )(p, p, p, p, conv_a, gparams)


def _head_of_lane(shape):
    return lax.broadcasted_iota(jnp.int32, shape, 1) // 64


def _block_diag(xb):
    hl = _head_of_lane(xb.shape)
    zero = jnp.zeros_like(xb)
    return jnp.concatenate([jnp.where(hl == h, xb, zero) for h in range(GDN_HEADS)], axis=0)


def _dot_exact3(a_bf, x):
    h1 = x.astype(BF16)
    r1 = x - h1.astype(F32)
    h2 = r1.astype(BF16)
    h3 = (r1 - h2.astype(F32)).astype(BF16)
    return _dot(a_bf, h1) + _dot(a_bf, h2) + _dot(a_bf, h3)


def _gdn_local(insts):
    cs = GDN_CHUNK
    hl = _head_of_lane((cs, GW))
    row = lax.broadcasted_iota(jnp.int32, (cs, GW), 0)
    col = lax.broadcasted_iota(jnp.int32, (cs, GW), 1) % 64
    ti = lax.broadcasted_iota(jnp.int32, (cs, cs), 0)
    tj = lax.broadcasted_iota(jnp.int32, (cs, cs), 1)
    same16 = (row // 16) == (col // 16)
    same32 = (row // 32) == (col // 32)
    eye = jnp.where(row == col, 1.0, 0.0)
    nt = (((1,), (1,)), ((), ()))

    def widen(gates, base):
        out = jnp.zeros((cs, GW), F32)
        for h in range(GDN_HEADS):
            out = jnp.where(hl == h, gates[:, base + h:base + h + 1], out)
        return out

    for it in insts:
        rev = it["rev"]
        d = 1 if rev else 0
        it["la_w"] = widen(it["gates"], GDN_HEADS * d)
        it["beta_w"] = widen(it["gates"], 2 * GDN_HEADS + GDN_HEADS * d)
        it["tri"] = jnp.where((tj >= ti) if rev else (tj <= ti), 1.0, 0.0).astype(BF16)
        it["incl"] = (row <= col) if rev else (row >= col)
        it["strict"] = (row < col) if rev else (row > col)
        it["kb"] = it["k"].astype(BF16)
        it["kbd"] = _block_diag(it["kb"])
    for it in insts:
        kq = lax.dot_general(jnp.concatenate([it["kb"], it["q"].astype(BF16)], axis=0), it["kbd"], nt,
                             preferred_element_type=F32)
        it["kk_w"], it["qk_w"] = kq[0:cs, :], kq[cs:2 * cs, :]
    for it in insts:
        it["gcc"] = _dot_exact3(it["tri"], it["la_w"])
    for it in insts:
        rev, la_w, incl = it["rev"], it["la_w"], it["incl"]
        gcr = jnp.sum(jnp.where((row >= col) if rev else (row <= col), la_w, 0.0), axis=0, keepdims=True)
        it["decay"] = jnp.where(incl, jnp.exp(jnp.where(incl, it["gcc"] - gcr, 0.0)), 0.0)
        it["lower"] = jnp.where(it["strict"], it["beta_w"] * it["kk_w"] * it["decay"], 0.0)
        pw = jnp.where(same16, it["lower"], 0.0)
        it["t"] = eye - pw
        it["pwb"] = pw.astype(BF16)
    for it in insts:
        it["pwb"] = _dot(it["pwb"], _block_diag(it["pwb"])).astype(BF16)
    for _ in range(2):
        for it in insts:
            both = _dot(jnp.concatenate([it["pwb"], it["t"].astype(BF16)], axis=0), _block_diag(it["pwb"]))
            it["pwb"] = both[0:cs, :].astype(BF16)
            it["t"] = it["t"] + both[cs:2 * cs, :]
    for it in insts:
        it["t"] = it["t"] + _dot(it["t"].astype(BF16), _block_diag(it["pwb"]))
    for level in range(2):
        for it in insts:
            off = (jnp.where(same32 & jnp.logical_not(same16), it["lower"], 0.0) if level == 0
                   else jnp.where(same32, 0.0, it["lower"]))
            it["tb"] = it["t"].astype(BF16)
            it["tc"] = _dot(it["tb"], _block_diag(off.astype(BF16)))
        for it in insts:
            it["t"] = it["t"] - _dot(it["tc"].astype(BF16), _block_diag(it["tb"]))
    for it in insts:
        it["tb"] = it["t"].astype(BF16)
        it["egc"] = jnp.exp(it["gcc"])
        it["u"] = _dot(it["tb"], _block_diag((it["v"] * it["beta_w"]).astype(BF16)))
    for it in insts:
        it["w"] = _dot(it["tb"], _block_diag((it["k"] * (it["beta_w"] * it["egc"])).astype(BF16)))
    for it in insts:
        gcc = it["gcc"]
        gl = gcc[0:1, :] if it["rev"] else gcc[cs - 1:cs, :]
        it["qkm"] = jnp.where(it["incl"], it["qk_w"] * it["decay"], 0.0).astype(BF16)
        it["q_dec"] = (it["q"] * it["egc"]).astype(BF16)
        it["k_dec"] = (it["k"] * jnp.exp(gl - gcc)).astype(BF16)
        it["a_last"] = jnp.exp(gl)
        it["wb"] = it["w"].astype(BF16)


def _gdn_recurrence(group):
    r2 = lax.broadcasted_iota(jnp.int32, (GW, GW), 0) // 64
    c2 = lax.broadcasted_iota(jnp.int32, (GW, GW), 1) // 64
    tn = (((0,), (0,)), ((), ()))
    for it in group:
        it["s"] = it["s_ref"][...]
        it["sb"] = it["s"].astype(BF16)
    for it in group:
        it["wq_s"] = _dot(jnp.concatenate([it["wb"], it["q_dec"]], axis=0), it["sb"])
    for it in group:
        it["vb"] = (it["u"] - it["wq_s"][0:GDN_CHUNK, :]).astype(BF16)
    for it in group:
        it["o"] = it["wq_s"][GDN_CHUNK:2 * GDN_CHUNK, :] + _dot(it["qkm"], _block_diag(it["vb"]))
    for it in group:
        upd = lax.dot_general(it["k_dec"], it["vb"], tn, preferred_element_type=F32)
        it["s_ref"][...] = it["s"] * it["a_last"] + jnp.where(r2 == c2, upd, 0.0)
    for it in group:
        it["o_ref"][it["rows"], :] = it["o"]


def _gdn_kernel(xf_ref, gf_ref, xb_ref, gb_ref, of_ref, ob_ref, *s_refs, seq_rows):
    @pl.when(pl.program_id(1) == 0)
    def _():
        for s_ref in s_refs:
            s_ref[...] = jnp.zeros_like(s_ref)

    n_chunks = seq_rows // GDN_CHUNK
    scans = []
    for slot in range(xf_ref.shape[0] // seq_rows):
        for x_ref, g_ref, o_ref, rev in ((xf_ref, gf_ref, of_ref, False), (xb_ref, gb_ref, ob_ref, True)):
            seq = []
            for ci in (reversed(range(n_chunks)) if rev else range(n_chunks)):
                r0 = slot * seq_rows + ci * GDN_CHUNK
                rows = slice(r0, r0 + GDN_CHUNK)
                seq.append(dict(rev=rev, rows=rows, o_ref=o_ref, s_ref=s_refs[len(scans)], gates=g_ref[rows, :],
                                q=x_ref[rows, 0:GW], k=x_ref[rows, GW:2 * GW], v=x_ref[rows, 2 * GW:3 * GW]))
            scans.append(seq)
    _gdn_local([it for group in zip(*scans) for it in group])
    for group in zip(*scans):
        _gdn_recurrence(list(group))


def _gdn(gqkv, gates, geom):
    n = gqkv.shape[0]
    cs, nb = geom["gdn_rows"], geom["gdn_nb"]
    nl_c, nc_c = geom["l"] // cs, geom["lc"] // cs
    lat_blocks = geom["nl"] // (cs * nb)
    steps = nl_c + nc_c

    def fwd(g, s):
        return jnp.where(s < nc_c, lat_blocks + g * nc_c + s, g * nl_c + s - nc_c)

    def bwd(g, s):
        return jnp.where(s < nc_c, lat_blocks + g * nc_c + (nc_c - 1 - s), g * nl_c + (nl_c - 1 - (s - nc_c)))

    return pl.pallas_call(
        functools.partial(_gdn_kernel, seq_rows=cs),
        grid=(geom["b"] // nb, steps),
        in_specs=[
            pl.BlockSpec((nb * cs, A_QKV), lambda g, s: (fwd(g, s), 0)),
            pl.BlockSpec((nb * cs, LANE), lambda g, s: (fwd(g, s), 0)),
            pl.BlockSpec((nb * cs, A_QKV), lambda g, s: (bwd(g, s), 0)),
            pl.BlockSpec((nb * cs, LANE), lambda g, s: (bwd(g, s), 0)),
        ],
        out_specs=[
            pl.BlockSpec((nb * cs, GW), lambda g, s: (fwd(g, s), 0)),
            pl.BlockSpec((nb * cs, GW), lambda g, s: (bwd(g, s), 0)),
        ],
        out_shape=[jax.ShapeDtypeStruct((n, GW), F32), jax.ShapeDtypeStruct((n, GW), F32)],
        scratch_shapes=[pltpu.VMEM((GW, GW), F32)] * (2 * nb),
        compiler_params=_cparams(("parallel", "arbitrary")),
        name="gdn_scan",
    )(gqkv, gates, gqkv, gates)


def _attn_kernel(*refs, lat_queries):
    if lat_queries:
        qt_ref, kc_ref, kl_ref, vc_ref, vl_ref, o_ref, acc_ref, st_ref = refs
    else:
        qt_ref, kc_ref, vc_ref, o_ref, acc_ref = refs
    tq = qt_ref.shape[1]
    acc_ref[...] = jnp.zeros(acc_ref.shape, F32)
    qrows = [slice(g * HEAD_DIM, (g + 1) * HEAD_DIM) for g in range(ATT_GROUPS)]
    arows = [slice(g * V_ROWS, (g + 1) * V_ROWS) for g in range(ATT_GROUPS)]

    def scores(k_ref, off):
        kc = k_ref[pl.ds(off, TK), :]
        return [_dot(kc, qt_ref[r, :]) for r in qrows]

    def update(v_ref, off, sts, ms):
        vc = v_ref[:, pl.ds(off, TK)]
        new_ms, ps, alphas = [], [], []
        for st, m_old in zip(sts, ms):
            m_new = jnp.maximum(m_old, jnp.max(st, axis=0, keepdims=True))
            ps.append(jnp.exp2(st - m_new).astype(BF16))
            alphas.append(jnp.exp2(m_old - m_new))
            new_ms.append(m_new)
        pvs = [_dot(vc, p) for p in ps]
        for r, alpha, pv in zip(arows, alphas, pvs):
            acc_ref[r, :] = alpha * acc_ref[r, :] + pv
        return tuple(new_ms)

    ms = (jnp.full((1, tq), -jnp.inf, F32),) * ATT_GROUPS
    n_ctx = kc_ref.shape[0] // TK
    cur = scores(kc_ref, 0)
    for c in range(1, n_ctx):
        nxt = scores(kc_ref, c * TK)
        ms = update(vc_ref, (c - 1) * TK, cur, ms)
        cur = nxt
    if lat_queries:
        n_lat = kl_ref.shape[0] // TK
        nxt = scores(kl_ref, 0)
        ms = update(vc_ref, (n_ctx - 1) * TK, cur, ms)
        for g in range(ATT_GROUPS):
            st_ref[g] = nxt[g]
        per_iter = math.gcd(ATT_CHUNKS_PER_ITER, n_lat)

        def body(i, ms):
            cur = [st_ref[g] for g in range(ATT_GROUPS)]
            for c in range(per_iter):
                off = pl.multiple_of((i * per_iter + c) * TK, TK)
                off_next = pl.multiple_of(jnp.minimum(off + TK, (n_lat - 1) * TK), TK)
                nxt = scores(kl_ref, off_next)
                ms = update(vl_ref, off, cur, ms)
                cur = nxt
            for g in range(ATT_GROUPS):
                st_ref[g] = cur[g]
            return ms

        ms = lax.fori_loop(0, n_lat // per_iter, body, ms)
    else:
        ms = update(vc_ref, (n_ctx - 1) * TK, cur, ms)

    out = [acc_ref[r, :][0:HEAD_DIM, :] / acc_ref[r, :][HEAD_DIM:HEAD_DIM + 1, :] for r in arows]
    o_ref[...] = jnp.concatenate(out, axis=0).T.astype(o_ref.dtype)


def _attention(qt, kh, vt, lat_queries, geom):
    b, l, lc, nl = geom["b"], geom["l"], geom["lc"], geom["nl"]
    tq = math.gcd(TQ_LAT, l) if lat_queries else geom["tq"]
    nq = (l if lat_queries else lc) // tq
    qbase = 0 if lat_queries else nl // tq
    gw = ATT_GROUPS * HEAD_DIM
    k_ctx = pl.BlockSpec((None, lc, HEAD_DIM), lambda bb, h, i: (h, nl // lc + bb, 0))
    v_ctx = pl.BlockSpec((V_ROWS, lc), lambda bb, h, i: (h, nl // lc + bb))
    k_lat = pl.BlockSpec((None, l, HEAD_DIM), lambda bb, h, i: (h, bb, 0))
    v_lat = pl.BlockSpec((V_ROWS, l), lambda bb, h, i: (h, bb))
    scratch = [pltpu.VMEM((ATT_GROUPS * V_ROWS, tq), F32)]
    if lat_queries:
        kv_specs, kv_args = [k_ctx, k_lat, v_ctx, v_lat], (kh, kh, vt, vt)
        scratch.append(pltpu.VMEM((ATT_GROUPS, TK, tq), F32))
    else:
        kv_specs, kv_args = [k_ctx, v_ctx], (kh, vt)
    return pl.pallas_call(
        functools.partial(_attn_kernel, lat_queries=lat_queries),
        grid=(b, ATT_KV_HEADS, nq),
        in_specs=[pl.BlockSpec((gw, tq), lambda bb, h, i: (h, qbase + bb * nq + i))] + kv_specs,
        out_specs=pl.BlockSpec((tq, gw), lambda bb, h, i: (bb * nq + i, h)),
        out_shape=jax.ShapeDtypeStruct((b * nq * tq, ATT_Q), BF16),
        scratch_shapes=scratch,
        compiler_params=_cparams(("parallel", "parallel", "arbitrary")),
        name="attention_lat" if lat_queries else "attention_ctx",
    )(qt, *kv_args)


def _outproj_kernel(x_ref, of_ref, ob_ref, z_ref, ybl_ref, ybc_ref, cb_ref, cc_ref, ch_ref,
                    ccp_ref, chp_ref, ccn_ref, chn_ref, mod_ref, w_ref, gg_ref, cw_ref,
                    o_ref, *, geom):
    j = pl.program_id(0)
    tm = x_ref.shape[0]
    first, last = _seq_flags(j, tm, geom)
    o = of_ref[...] + ob_ref[...]
    ms = _group_sum(o * o, _group_ones(GW)) * (1.0 / GDN_DV)
    ya = o * lax.rsqrt(ms + EPS) * gg_ref[0:1, :] * _silu(z_ref[...])
    yb = jnp.where(j * tm < geom["nl"], ybl_ref[...], ybc_ref[...])
    u = cc_ref[...] * ch_ref[...]
    hp = jnp.where(first, 0.0, ccp_ref[7:8, :] * chp_ref[7:8, :])
    hn = jnp.where(last, 0.0, ccn_ref[0:1, :] * chn_ref[0:1, :])
    yc = cb_ref[...] * _conv3(u, hp, hn, cw_ref)
    acc = _dot(ya.astype(BF16), w_ref[0:A_Z, :])
    acc += _dot(yb, w_ref[A_Z:A_Z + ATT_Q, :])
    acc += _dot(yc.astype(BF16), w_ref[A_Z + ATT_Q:, :])
    o_ref[...] = x_ref[...] + mod_ref[2:3, :] * acc


def _outproj(xs, of, ob, p, yb_lat, yb_ctx, mod, w, gg, conv_c, n_rows, geom):
    d = xs.shape[1]
    tm = geom["tm_cv"]
    nlt, tpb = geom["nl"] // tm, geom["l"] // tm
    nct = yb_ctx.shape[0] // tm
    hb = tm // 8
    nblk8 = xs.shape[0] // 8
    col = lambda w_, c: pl.BlockSpec((tm, w_), lambda j: (j, c // w_))
    prev = lambda c: pl.BlockSpec((8, SC_CH), lambda j: (jnp.maximum(j * hb - 1, 0), c // SC_CH))
    nxt = lambda c: pl.BlockSpec((8, SC_CH), lambda j: (jnp.minimum((j + 1) * hb, nblk8 - 1), c // SC_CH))
    gdn_o = pl.BlockSpec((tm, GW), lambda j: (_gdn_tile_perm(j, geom), 0))
    return pl.pallas_call(
        functools.partial(_outproj_kernel, geom=geom),
        grid=(n_rows // tm,),
        in_specs=[
            pl.BlockSpec((tm, d), lambda j: (j, 0)),
            gdn_o, gdn_o,
            col(A_Z, C_Z),
            pl.BlockSpec((tm, ATT_Q), lambda j: (jnp.minimum(j, nlt - 1), 0)),
            pl.BlockSpec((tm, ATT_Q), lambda j: (jnp.clip(j - nlt, 0, nct - 1), 0)),
            col(SC_CH, C_CB), col(SC_CH, C_CC), col(SC_CH, C_CH),
            prev(C_CC), prev(C_CH), nxt(C_CC), nxt(C_CH),
            pl.BlockSpec((None, 8, d), lambda j: (_mod_row(j, nlt, tpb), 0, 0)),
            pl.BlockSpec((d, d), lambda j: (0, 0)),
            pl.BlockSpec((8, GW), lambda j: (0, 0)),
            pl.BlockSpec((8, SC_CH), lambda j: (0, 0)),
        ],
        out_specs=pl.BlockSpec((tm, d), lambda j: (j, 0)),
        out_shape=jax.ShapeDtypeStruct((n_rows, d), F32),
        compiler_params=_cparams(("parallel",)),
        name="outproj",
    )(xs, of, ob, p, yb_lat, yb_ctx, p, p, p, p, p, p, p, mod, w, gg, conv_c)


def _expert_of_step(j, e, ne):
    return jnp.where(j % 2 == 0, e, ne - 1 - e)


def _experts_kernel(*refs, routed, final_norm):
    refs = list(refs)
    x_ref, g_ref, mod_ref = refs[0:3]
    del refs[0:3]
    r_ref = refs.pop(0) if routed else None
    gf_ref = refs.pop(0) if final_norm else None
    wgu_ref, wd_ref, o_ref, h_ref, acc_ref, a_ref = refs[0:6]
    comb_ref = refs[6] if routed else None
    e = pl.program_id(1)
    ne = pl.num_programs(1) - 1
    ffe = wd_ref.shape[0]

    def gate_up():
        gu = _dot(h_ref[...], wgu_ref[...])
        a_ref[...] = (_silu(gu[:, :ffe]) * gu[:, ffe:]).astype(BF16)

    def down(a_prev):
        y = _dot(a_prev, wd_ref[...])
        if routed:
            comb = comb_ref[...]
            lane = lax.broadcasted_iota(jnp.int32, comb.shape, 1)
            prev_expert = _expert_of_step(pl.program_id(0), e - 1, ne)
            y = jnp.sum(jnp.where(lane == prev_expert, comb, 0.0), axis=-1, keepdims=True) * y
        acc_ref[...] += y

    @pl.when(e == 0)
    def _():
        h = _norm_mod(x_ref[...], g_ref[...], mod_ref[3:4, :], mod_ref[4:5, :])
        h_ref[...] = h.astype(BF16)
        acc_ref[...] = jnp.zeros_like(acc_ref)
        if routed:
            logits = _dot_split3(h, r_ref[...])
            lane = lax.broadcasted_iota(jnp.int32, logits.shape, 1).astype(F32)
            valid = lane < N_EXPERTS
            logits = jnp.where(valid, logits, -jnp.inf)
            ex = jnp.exp(logits - jnp.max(logits, axis=-1, keepdims=True))
            probs = jnp.where(valid, ex / jnp.sum(ex, axis=-1, keepdims=True), -1.0)
            p1 = jnp.max(probs, axis=-1, keepdims=True)
            i1 = jnp.min(jnp.where(probs == p1, lane, float(LANE)), axis=-1, keepdims=True)
            rest = jnp.where(lane == i1, -1.0, probs)
            p2 = jnp.max(rest, axis=-1, keepdims=True)
            i2 = jnp.min(jnp.where(rest == p2, lane, float(LANE)), axis=-1, keepdims=True)
            den = p1 + p2
            comb_ref[...] = jnp.where(lane == i1, p1 / den, 0.0) + jnp.where(lane == i2, p2 / den, 0.0)
        gate_up()

    @pl.when(jnp.logical_and(e > 0, e < ne))
    def _():
        a_prev = a_ref[...]
        down(a_prev)
        gate_up()

    @pl.when(e == ne)
    def _():
        down(a_ref[...])
        y = x_ref[...] + mod_ref[5:6, :] * acc_ref[...]
        if final_norm:
            y = y * lax.rsqrt(jnp.mean(y * y, axis=-1, keepdims=True) + EPS) * gf_ref[...]
        o_ref[...] = y


def _experts(xs, g, mod, router, wgu, wd, n_rows, geom, final_g=None):
    d = xs.shape[1]
    ne, ffe = wd.shape[0], wd.shape[1]
    tm = math.gcd(TM_EXPERTS, math.gcd(geom["l"], geom["nc"]))
    nlt, tpb = geom["nl"] // tm, geom["l"] // tm
    routed = router is not None
    in_specs = [
        pl.BlockSpec((tm, d), lambda j, e: (j, 0), pipeline_mode=pl.Buffered(1)),
        pl.BlockSpec((1, d), lambda j, e: (0, 0)),
        pl.BlockSpec((None, 8, d), lambda j, e: (_mod_row(j, nlt, tpb), 0, 0)),
    ]
    scratch = [pltpu.VMEM((tm, d), BF16), pltpu.VMEM((tm, d), F32), pltpu.VMEM((tm, ffe), BF16)]
    args = [xs, g, mod]
    if routed:
        in_specs.append(pl.BlockSpec((d, LANE), lambda j, e: (0, 0)))
        scratch.append(pltpu.VMEM((tm, LANE), F32))
        args.append(router)
    if final_g is not None:
        in_specs.append(pl.BlockSpec((1, d), lambda j, e: (0, 0)))
        args.append(final_g)
    in_specs += [
        pl.BlockSpec((None, d, 2 * ffe), lambda j, e: (_expert_of_step(j, jnp.minimum(e, ne - 1), ne), 0, 0)),
        pl.BlockSpec((None, ffe, d), lambda j, e: (_expert_of_step(j, jnp.maximum(e - 1, 0), ne), 0, 0)),
    ]
    return pl.pallas_call(
        functools.partial(_experts_kernel, routed=routed, final_norm=final_g is not None),
        grid=(n_rows // tm, ne + 1),
        in_specs=in_specs,
        out_specs=pl.BlockSpec((tm, d), lambda j, e: (j, 0), pipeline_mode=pl.Buffered(1)),
        out_shape=jax.ShapeDtypeStruct((n_rows, d), F32),
        scratch_shapes=scratch,
        compiler_params=_cparams(("parallel", "arbitrary")),
        name="moe" if routed else "ffn",
    )(*args, wgu, wd)


def _rope_tables(l, tm):
    quarter = HEAD_DIM // 4
    inv_freq = ROPE_THETA ** (-jnp.arange(quarter, dtype=F32) / quarter)
    t = jnp.arange(l, dtype=jnp.int32)
    ang_r = (t // GRID_W).astype(F32)[:, None] * inv_freq[None, :]
    ang_c = (t % GRID_W).astype(F32)[:, None] * inv_freq[None, :]
    cos = jnp.concatenate([jnp.cos(ang_r)] * 2 + [jnp.cos(ang_c)] * 2, axis=1)
    sin = jnp.concatenate([-jnp.sin(ang_r), jnp.sin(ang_r), -jnp.sin(ang_c), jnp.sin(ang_c)], axis=1)
    cos = jnp.tile(cos, (1, LANE // HEAD_DIM))
    sin = jnp.tile(sin, (1, LANE // HEAD_DIM))
    cos = jnp.concatenate([jnp.ones((tm, LANE), F32), cos], axis=0)
    sin = jnp.concatenate([jnp.zeros((tm, LANE), F32), sin], axis=0)
    return cos, sin


def _pad_rows(a, rows=8):
    return jnp.pad(a, ((0, rows - a.shape[0]), (0, 0)))


def _permute_w_in(w):
    o = 0
    parts = {}
    for name, size in (("qkv", A_QKV), ("z", A_Z), ("a", 2 * GDN_HEADS), ("b", 2 * GDN_HEADS),
                       ("q", ATT_Q), ("k", ATT_KV), ("v", ATT_KV), ("cb", SC_CH), ("cc", SC_CH), ("ch", SC_CH)):
        parts[name] = w[:, o:o + size]
        o += size
    pad = jnp.zeros((w.shape[0], P_COLS - C_AB - 4 * GDN_HEADS), w.dtype)
    cols = [parts[k] for k in ("qkv", "z", "cb", "cc", "ch", "a", "b")] + [pad] + [parts[k] for k in ("q", "k", "v")]
    return jnp.concatenate(cols, axis=1).astype(BF16)


def _split_ffn(wg, wu, wd):
    d, ff = wg.shape
    ffe = ff // FFN_SPLIT
    wgu = jnp.concatenate([wg.reshape(d, FFN_SPLIT, ffe), wu.reshape(d, FFN_SPLIT, ffe)], axis=-1)
    return jnp.swapaxes(wgu, 0, 1).astype(BF16), wd.reshape(FFN_SPLIT, ffe, d).astype(BF16)


def kernel(x, c, ctx, c_ctx, w_mod, b_mod, norm1, norm2, w_in, conv_a, a_log, dt_bias, gdn_norm,
           q_norm, k_norm, conv_c, w_out, ffn_w_gate, ffn_w_up, ffn_w_down, router,
           moe_w_gate, moe_w_up, moe_w_down, norm_f):
    b, l, d = x.shape
    lc = ctx.shape[1]
    depth = w_mod.shape[0]
    nl, nc = b * l, b * lc
    tm_mm = math.gcd(TM_MM, math.gcd(l, nc))
    tm_cv = math.gcd(TM_CV, math.gcd(l, lc))
    tq = math.gcd(TQ, math.gcd(l, lc))
    gdn_rows = math.gcd(GDN_CHUNKS_PER_STEP * GDN_CHUNK, math.gcd(l, lc))
    gdn_nb = GDN_BATCH_PER_STEP if b % GDN_BATCH_PER_STEP == 0 else 1
    assert l % GRID_W == 0 and l % TK == 0 and lc % TK == 0 and nl % lc == 0 and gdn_rows == tm_cv
    geom = dict(b=b, l=l, lc=lc, nl=nl, nc=nc, tm_mm=tm_mm, tm_cv=tm_cv, tq=tq,
                gdn_rows=gdn_rows, gdn_nb=gdn_nb)
    n = nl + nc

    mod = _modulation(c, c_ctx, w_mod, b_mod)
    cos_t, sin_t = _rope_tables(l, tm_mm)
    xs = jnp.concatenate([x.reshape(nl, d), ctx.reshape(nc, d)], axis=0)

    for li in range(depth):
        last = li == depth - 1
        rows_out = nl if last else n
        w_in_p = _permute_w_in(w_in[li])
        gparams = _pad_rows(jnp.stack([jnp.pad(a_log[li].reshape(-1), (0, LANE - 2 * GDN_HEADS)),
                                       jnp.pad(dt_bias[li].reshape(-1), (0, LANE - 2 * GDN_HEADS))]))
        p, qt, kh, vt = _inproj(xs, norm1[li][None, :], mod[li], w_in_p, cos_t, sin_t,
                                _pad_rows(jnp.tile(q_norm[li], ATT_HEADS)[None, :]),
                                _pad_rows(jnp.tile(k_norm[li], ATT_KV_HEADS)[None, :]), geom)
        gqkv, gates = _prep(p, _pad_rows(conv_a[li]), gparams, geom)
        of, ob = _gdn(gqkv, gates, geom)
        yb_lat = _attention(qt, kh, vt, True, geom)
        yb_ctx = yb_lat if last else _attention(qt, kh, vt, False, geom)
        xs = _outproj(xs, of, ob, p, yb_lat, yb_ctx, mod[li], w_out[li].astype(BF16),
                      _pad_rows(jnp.tile(gdn_norm[li], GDN_HEADS)[None, :]), _pad_rows(conv_c[li]),
                      rows_out, geom)
        i = li // 2
        final_g = norm_f[None, :] if last else None
        if li % 2 == 0:
            wgu, wd = _split_ffn(ffn_w_gate[i], ffn_w_up[i], ffn_w_down[i])
            xs = _experts(xs, norm2[li][None, :], mod[li], None, wgu, wd, rows_out, geom, final_g)
        else:
            wgu = jnp.concatenate([moe_w_gate[i], moe_w_up[i]], axis=-1).astype(BF16)
            xs = _experts(xs, norm2[li][None, :], mod[li],
                          jnp.pad(router[i], ((0, 0), (0, LANE - N_EXPERTS))), wgu,
                          moe_w_down[i].astype(BF16), rows_out, geom, final_g)
    return xs.reshape(b, l, d)
```
